```python
import math
import jax
import jax.numpy as jnp
from jax import lax
import numpy as np

D_MODEL = 1024
BATCH = 16
SEQ = 256
DEPTH = 2
DEC_BATCH = 4
DEC_SEQ = 4096
PAST_LEN = 512

GRID_W = 64
EPS = 1e-6
D_A = 512
SHORT_K = 3
EMB_DIM = 33
FILT_ORDER = 64
DECAY_TARGET = 1e-2
FAST_DECAY_PCT = 0.3
SLOW_DECAY_PCT = 1.5
MIN_DECAY = math.log(DECAY_TARGET) / SLOW_DECAY_PCT
MAX_DECAY = math.log(DECAY_TARGET) / FAST_DECAY_PCT
N_HEADS_B = 8
Q_RANK = 256
KV_RANK = 256
NOPE_DIM = 64
ROPE_DIM = 32
V_DIM = 64
ROPE_BASE = 10000.0
D_B = N_HEADS_B * V_DIM
QK_DIM = NOPE_DIM + ROPE_DIM
ATTN_SCALE = 1.0 / math.sqrt(QK_DIM)
Q_BLOCK = 128
D_C = 1024
N_GROUPS_C = 8
G_C = D_C // N_GROUPS_C
N_EVEN = (DEPTH + 1) // 2
N_ODD = DEPTH // 2
IN_E = 4 * D_A + Q_RANK + KV_RANK + ROPE_DIM + D_B
SPLIT_E = [3 * D_A, 4 * D_A, 4 * D_A + Q_RANK, 4 * D_A + Q_RANK + KV_RANK, 4 * D_A + Q_RANK + KV_RANK + ROPE_DIM]

kernel_name = 'hybrid_hyena_mla_fnet_prefix_step'


def rmsnorm(x, g):
    xf = x.astype(jnp.float32)
    y = xf * lax.rsqrt(jnp.mean(xf * xf, axis=-1, keepdims=True) + EPS)
    return (y * g.astype(jnp.float32)).astype(x.dtype)


def ada_params(cond, w_mod, b_mod):
    mod = jax.nn.silu(cond) @ w_mod + b_mod
    shift, scale, gate = jnp.split(mod[:, None, :], 3, axis=-1)
    return shift, scale, gate


def short_conv(u, w, b):
    L = u.shape[1]
    pad = SHORT_K // 2
    up = jnp.pad(u, ((0, 0), (pad, pad), (0, 0)))
    return b + sum(up[:, k:k + L] * w[k] for k in range(SHORT_K))


def hyena_filters(L, w1, b1, w2, b2, w3):
    f32 = jnp.float32
    t = jnp.linspace(0.0, 1.0, L, dtype=f32)[:, None]
    w = 2.0 * math.pi * jnp.arange(L, dtype=f32)[:, None] / L
    bands = (EMB_DIM - 1) // 2
    f = jnp.linspace(1e-4, bands - 1, bands, dtype=f32)[None, :]
    z = jnp.concatenate([t, jnp.cos(f * w), -jnp.sin(f * w)], axis=-1)
    h = jnp.sin(z @ w1.astype(f32) + b1.astype(f32))
    h = jnp.sin(h @ w2.astype(f32) + b2.astype(f32))
    h = (h @ w3.astype(f32)).reshape(L, 2, D_A)
    deltas = jnp.abs(jnp.linspace(MIN_DECAY, MAX_DECAY, D_A, dtype=f32))
    h = h * jnp.exp(-t * deltas)[:, None, :]
    h = h / jnp.sum(jnp.abs(h), axis=(0, 1), keepdims=True)
    return h[:, 0], h[:, 1]


def hyena(u, conv_w, conv_b, filt, skip):
    L = u.shape[1]
    u = short_conv(u, conv_w, conv_b)
    x0, x1, v = jnp.split(u, 3, axis=-1)
    v = v * x1
    h_fwd, h_bwd = hyena_filters(L, *filt)
    kern = jnp.concatenate([h_fwd, jnp.zeros((1, D_A), jnp.float32), h_bwd[:0:-1]], axis=0)
    vf = jnp.fft.rfft(v.astype(jnp.float32), n=2 * L, axis=1)
    kf = jnp.fft.rfft(kern, n=2 * L, axis=0)
    y = jnp.fft.irfft(vf * kf[None], n=2 * L, axis=1)[:, :L]
    y = y.astype(u.dtype) + v * skip
    return y * x0


def axial_rope_tables(L):
    rows = L // GRID_W
    row = jnp.repeat(jnp.arange(rows, dtype=jnp.float32), GRID_W)
    col = jnp.tile(jnp.arange(GRID_W, dtype=jnp.float32), rows)
    half = ROPE_DIM // 2
    inv = jnp.power(ROPE_BASE, -jnp.arange(0, half, 2, dtype=jnp.float32) / half)
    ang = jnp.stack([row[:, None] * inv, col[:, None] * inv], axis=1)
    return jnp.cos(ang), jnp.sin(ang)


def apply_axial_rope(x, cos, sin):
    nf = ROPE_DIM // 4
    xr = x.astype(jnp.float32).reshape(x.shape[:-1] + (2, 2, nf))
    shp = (cos.shape[0],) + (1,) * (x.ndim - 3) + (2, nf)
    c = cos.reshape(shp)
    s = sin.reshape(shp)
    x1, x2 = xr[..., 0, :], xr[..., 1, :]
    out = jnp.stack([x1 * c - x2 * s, x1 * s + x2 * c], axis=-2)
    return out.reshape(x.shape).astype(x.dtype)


def block_attention(q, k, v):
    B, Lq, H, Dq = q.shape
    nb = Lq // Q_BLOCK
    qb = q.reshape(B, nb, Q_BLOCK, H, Dq).transpose(1, 0, 2, 3, 4)

    def one_block(qi):
        s = jnp.einsum('bqhd,bkhd->bhqk', qi, k).astype(jnp.float32) * ATTN_SCALE
        p = jax.nn.softmax(s, axis=-1)
        return jnp.einsum('bhqk,bkhd->bqhd', p.astype(v.dtype), v)

    o = lax.map(one_block, qb)
    return o.transpose(1, 0, 2, 3, 4).reshape(B, Lq, H, v.shape[-1])


def even_mixer(h, ew, ctx_kv):
    (w_in, conv_w, conv_b, f_w1, f_b1, f_w2, f_b2, f_w3, skip, qg, kvg, w_uq, w_ukv, w_out) = ew
    B, L, _ = h.shape
    proj = h @ w_in
    u_a, g_a, cq, ckv, krope, g_b = jnp.split(proj, SPLIT_E, axis=-1)
    y_a = hyena(u_a, conv_w, conv_b, (f_w1, f_b1, f_w2, f_b2, f_w3), skip)
    cq = rmsnorm(cq, qg)
    ckv = rmsnorm(ckv, kvg)
    q = (cq @ w_uq).reshape(B, L, N_HEADS_B, QK_DIM)
    if ctx_kv is None:
        keys_c, keys_r = ckv, krope
    else:
        cos, sin = axial_rope_tables(L)
        q = jnp.concatenate([q[..., :NOPE_DIM], apply_axial_rope(q[..., NOPE_DIM:], cos, sin)], axis=-1)
        ctx_ckv, ctx_krope = ctx_kv
        keys_c = jnp.concatenate([ctx_ckv, ckv], axis=1)
        keys_r = jnp.concatenate([ctx_krope, apply_axial_rope(krope, cos, sin)], axis=1)
    Lk = keys_c.shape[1]
    kv = (keys_c @ w_ukv).reshape(B, Lk, N_HEADS_B, NOPE_DIM + V_DIM)
    k = jnp.concatenate([kv[..., :NOPE_DIM], jnp.broadcast_to(keys_r[:, :, None, :], (B, Lk, N_HEADS_B, ROPE_DIM))], axis=-1)
    v = kv[..., NOPE_DIM:]
    o_b = block_attention(q, k, v).reshape(B, L, D_B)
    y = jnp.concatenate([y_a * jax.nn.silu(g_a), o_b * jax.nn.silu(g_b)], axis=-1) @ w_out
    return y, ckv, krope


def odd_mixer(h, w_in, fnet_w, w_out):
    B, L, _ = h.shape
    u, g = jnp.split(h @ w_in, 2, axis=-1)
    uf = u.astype(jnp.float32).reshape(B, L, N_GROUPS_C, G_C)
    m = jnp.fft.fft2(uf, axes=(1, 3), norm='ortho').real.astype(h.dtype)
    y = jnp.einsum('blgc,gcd->blgd', m, fnet_w).reshape(B, L, D_C)
    return (y * jax.nn.silu(g)) @ w_out


def setup_inputs(seed: int = 0) -> dict:
    key = jax.random.key(seed)
    ks = jax.random.split(key, 32)
    f32 = jnp.float32

    def nrm(k, shape, s):
        return jax.random.normal(k, shape, f32) * s

    return {
        'x_prompt': nrm(ks[0], (BATCH, SEQ, D_MODEL), 1.0),
        'x_sample': nrm(ks[1], (DEC_BATCH, DEC_SEQ, D_MODEL), 1.0),
        'c': nrm(ks[2], (DEC_BATCH, D_MODEL), 1.0),
        'cache_ckv': nrm(ks[3], (DEC_BATCH, N_EVEN, PAST_LEN, KV_RANK), 1.0),
        'cache_krope': nrm(ks[4], (DEC_BATCH, N_EVEN, PAST_LEN, ROPE_DIM), 1.0),
        'c_ctx': nrm(ks[5], (D_MODEL,), 1.0),
        'norm_g': 1.0 + nrm(ks[6], (DEPTH, D_MODEL), 0.02),
        'w_mod': nrm(ks[7], (DEPTH, D_MODEL, 3 * D_MODEL), D_MODEL ** -0.5),
        'b_mod': nrm(ks[8], (DEPTH, 3 * D_MODEL), 0.02),
        'w_in_e': nrm(ks[9], (N_EVEN, D_MODEL, IN_E), D_MODEL ** -0.5),
        'conv_w': nrm(ks[10], (N_EVEN, SHORT_K, 3 * D_A), SHORT_K ** -0.5),
        'conv_b': nrm(ks[11], (N_EVEN, 3 * D_A), 0.02),
        'filt_w1': nrm(ks[12], (N_EVEN, EMB_DIM, FILT_ORDER), EMB_DIM ** -0.5),
        'filt_b1': nrm(ks[13], (N_EVEN, FILT_ORDER), 0.02),
        'filt_w2': nrm(ks[14], (N_EVEN, FILT_ORDER, FILT_ORDER), FILT_ORDER ** -0.5),
        'filt_b2': nrm(ks[15], (N_EVEN, FILT_ORDER), 0.02),
        'filt_w3': nrm(ks[16], (N_EVEN, FILT_ORDER, 2 * D_A), FILT_ORDER ** -0.5),
        'hyena_skip': nrm(ks[17], (N_EVEN, D_A), 1.0),
        'q_norm_g': 1.0 + nrm(ks[18], (N_EVEN, Q_RANK), 0.02),
        'kv_norm_g': 1.0 + nrm(ks[19], (N_EVEN, KV_RANK), 0.02),
        'w_uq': nrm(ks[20], (N_EVEN, Q_RANK, N_HEADS_B * QK_DIM), Q_RANK ** -0.5),
        'w_ukv': nrm(ks[21], (N_EVEN, KV_RANK, N_HEADS_B * (NOPE_DIM + V_DIM)), KV_RANK ** -0.5),
        'w_out_e': nrm(ks[22], (N_EVEN, D_A + D_B, D_MODEL), (D_A + D_B) ** -0.5),
        'w_in_o': nrm(ks[23], (N_ODD, D_MODEL, 2 * D_C), D_MODEL ** -0.5),
        'fnet_w': nrm(ks[24], (N_ODD, N_GROUPS_C, G_C, G_C), G_C ** -0.5),
        'w_out_o': nrm(ks[25], (N_ODD, D_C, D_MODEL), D_C ** -0.5),
        'final_g': 1.0 + nrm(ks[26], (D_MODEL,), 0.02),
    }


def reference(x_prompt, x_sample, c, cache_ckv, cache_krope, c_ctx, norm_g, w_mod, b_mod,
              w_in_e, conv_w, conv_b, filt_w1, filt_b1, filt_w2, filt_b2, filt_w3, hyena_skip,
              q_norm_g, kv_norm_g, w_uq, w_ukv, w_out_e, w_in_o, fnet_w, w_out_o, final_g):
    xp = x_prompt
    xs = x_sample
    new_ckv = []
    new_krope = []
    for i in range(DEPTH):
        j = i // 2
        sh_p, sc_p, gt_p = ada_params(c_ctx[None, :], w_mod[i], b_mod[i])
        sh_s, sc_s, gt_s = ada_params(c, w_mod[i], b_mod[i])
        hp = rmsnorm(xp, norm_g[i]) * (1.0 + sc_p) + sh_p
        hs = rmsnorm(xs, norm_g[i]) * (1.0 + sc_s) + sh_s
        if i % 2 == 0:
            ew = (w_in_e[j], conv_w[j], conv_b[j], filt_w1[j], filt_b1[j], filt_w2[j], filt_b2[j],
                  filt_w3[j], hyena_skip[j], q_norm_g[j], kv_norm_g[j], w_uq[j], w_ukv[j], w_out_e[j])
            yp, ckv_p, krope_p = even_mixer(hp, ew, None)
            ys, _, _ = even_mixer(hs, ew, (cache_ckv[:, j], cache_krope[:, j]))
            new_ckv.append(ckv_p)
            new_krope.append(krope_p)
        else:
            yp = odd_mixer(hp, w_in_o[j], fnet_w[j], w_out_o[j])
            ys = odd_mixer(hs, w_in_o[j], fnet_w[j], w_out_o[j])
        xp = xp + gt_p * yp
        xs = xs + gt_s * ys
    y_prompt = rmsnorm(xp, final_g)
    y_sample = rmsnorm(xs, final_g)
    state_ckv = jnp.stack(new_ckv, axis=1)
    state_krope = jnp.stack(new_krope, axis=1)
    return (y_prompt, y_sample, state_ckv, state_krope)
```

```python
import functools
import math

import numpy as np
import jax
import jax.numpy as jnp
from jax import lax
from jax.experimental import pallas as pl
from jax.experimental.pallas import tpu as pltpu

F32 = jnp.float32
BF16 = jnp.bfloat16

D_MODEL = 1024
GRID_W = 64
EPS = 1e-6
D_A = 512
EMB_DIM = 33
DECAY_TARGET = 1e-2
MIN_DECAY = math.log(DECAY_TARGET) / 1.5
MAX_DECAY = math.log(DECAY_TARGET) / 0.3
N_HEADS = 8
Q_RANK = 256
KV_RANK = 256
NOPE_DIM = 64
ROPE_DIM = 32
V_DIM = 64
ROPE_BASE = 10000.0
D_B = N_HEADS * V_DIM
QK_DIM = NOPE_DIM + ROPE_DIM
ATTN_SCALE = 1.0 / math.sqrt(QK_DIM)
D_C = 1024
N_GROUPS_C = 8
G_C = D_C // N_GROUPS_C

LANES = 128
HEAD_PAD = 128
VMEM_LIMIT = 56 * 1024 * 1024

_C_UA, _C_GA, _C_CQ, _C_CKV, _C_GB, _C_KRA, _C_KRB, _C_END = (
    0, 1536, 2048, 2304, 2560, 3072, 3200, 3328)


def _params(sem):
    return pltpu.CompilerParams(dimension_semantics=sem, vmem_limit_bytes=VMEM_LIMIT)


def _rms(x):
    return x * lax.rsqrt(jnp.mean(x * x, axis=-1, keepdims=True) + EPS)


def _silu(g):
    return g * jax.nn.sigmoid(g)


def _dot(a, b):
    return jnp.dot(a, b, preferred_element_type=F32)


def _smem():
    return pl.BlockSpec(memory_space=pltpu.SMEM)


def _ada_kernel(cond_ref, w_ref, b_ref, o_ref):
    c = _silu(cond_ref[...])
    o_ref[...] = jnp.dot(c, w_ref[...], preferred_element_type=F32,
                         precision=lax.Precision.HIGHEST) + b_ref[...]


def _ada_params(cond, w_mod, b_mod):
    depth = w_mod.shape[0]
    n = cond.shape[0]
    tn = 1024
    return pl.pallas_call(
        _ada_kernel,
        grid=(depth, 3 * D_MODEL // tn),
        in_specs=[pl.BlockSpec((n, D_MODEL), lambda i, j: (0, 0)),
                  pl.BlockSpec((None, D_MODEL, tn), lambda i, j: (i, 0, j)),
                  pl.BlockSpec((None, 1, tn), lambda i, j: (i, 0, j))],
        out_specs=pl.BlockSpec((None, n, tn), lambda i, j: (i, 0, j)),
        out_shape=jax.ShapeDtypeStruct((depth, n, 3 * D_MODEL), F32),
        compiler_params=_params(("arbitrary", "arbitrary")),
        name="ada_params",
    )(cond, w_mod, b_mod.reshape(depth, 1, 3 * D_MODEL))


def _inproj_even_kernel(x_ref, sc_ref, sh_ref, w_ref, qg_ref, kvg_ref,
                        ua_ref, ga_ref, cq_ref, ckv_ref, gb_ref, kra_ref, krb_ref):
    h = (_rms(x_ref[...]) * sc_ref[...] + sh_ref[...]).astype(BF16)

    def proj(a, b):
        return _dot(h, w_ref[:, a:b])

    ua_ref[...] = proj(_C_UA, _C_GA)
    ga_ref[...] = proj(_C_GA, _C_CQ)
    cq_ref[...] = (_rms(proj(_C_CQ, _C_CKV)) * qg_ref[...]).astype(BF16)
    ckv_ref[...] = _rms(proj(_C_CKV, _C_GB)) * kvg_ref[...]
    gb_ref[...] = proj(_C_GB, _C_KRA)
    kra_ref[...] = proj(_C_KRA, _C_KRB)
    krb_ref[...] = proj(_C_KRB, _C_END)


def _inproj_even(x2d, sc, sh, w, qg, kvg, tiles_per_mod, tm):
    t = x2d.shape[0]
    row = lambda i: (i, 0)
    fixed = lambda i: (0, 0)
    mod = lambda i: (i // tiles_per_mod, 0, 0)
    widths = (3 * D_A, D_A, Q_RANK, KV_RANK, D_B, LANES, LANES)
    dtypes = (F32, F32, BF16, F32, F32, F32, F32)
    return pl.pallas_call(
        _inproj_even_kernel,
        grid=(t // tm,),
        in_specs=[pl.BlockSpec((tm, D_MODEL), row),
                  pl.BlockSpec((None, 1, D_MODEL), mod),
                  pl.BlockSpec((None, 1, D_MODEL), mod),
                  pl.BlockSpec((D_MODEL, _C_END), fixed),
                  pl.BlockSpec((1, Q_RANK), fixed),
                  pl.BlockSpec((1, KV_RANK), fixed)],
        out_specs=[pl.BlockSpec((tm, wd), row) for wd in widths],
        out_shape=[jax.ShapeDtypeStruct((t, wd), dt) for wd, dt in zip(widths, dtypes)],
        compiler_params=_params(("parallel",)),
        name="inproj_even",
    )(x2d, sc, sh, w, qg, kvg)


def _stage_a(ca_ref, sa_ref, x_ref, k1, nslab):
    ar = ca_ref[k1, 0] * x_ref[0]
    ai = sa_ref[k1, 0] * x_ref[0]
    for n1 in range(1, nslab):
        x = x_ref[n1]
        ar = ar + ca_ref[k1, n1] * x
        ai = ai + sa_ref[k1, n1] * x
    return ar, ai


def _spectrum_kernel(ca_ref, sa_ref, kern_ref, tw_ref, fh_ref, fl_ref, kf_ref, *, n1, n2):
    def body(k1, carry):
        ar, ai = _stage_a(ca_ref, sa_ref, kern_ref, k1, n1)
        twr = tw_ref[0, k1]
        twi = tw_ref[1, k1]
        a = jnp.concatenate([ar * twr - ai * twi, ar * twi + ai * twr], axis=0)
        a_hi = a.astype(BF16)
        a_lo = (a - a_hi.astype(F32)).astype(BF16)
        x = _dot(fh_ref[...], a_hi) + (_dot(fh_ref[...], a_lo) + _dot(fl_ref[...], a_hi))
        kf_ref[k1, 0] = x[:n2]
        kf_ref[k1, 1] = x[n2:]
        return carry

    lax.fori_loop(0, n1 // 2 + 1, body, 0)


def _hyena_kernel(ca_ref, sa_ref, cr_ref, ci_ref,
                  x0_ref, x1_ref, v_ref, ga_ref,
                  w0_ref, w1_ref, wv_ref, b0_ref, b1_ref, bv_ref, skip_ref,
                  kf_ref, tw_ref, ffwd_ref, finv_ref,
                  o_ref, v_scr, y_scr, *, seq, nslab, n2):
    ct = o_ref.shape[-1]
    row = lax.broadcasted_iota(jnp.int32, (n2, ct), 0)

    def short_conv(ref, w_ref, b_ref, s, start):
        u = ref[pl.ds(start, n2), :]
        prev = ref[pl.ds(jnp.maximum(start - 1, 0), 1), :]
        prev = jnp.where(s == 0, 0.0, prev)
        nxt = ref[pl.ds(jnp.minimum(start + n2, seq - 1), 1), :]
        nxt = jnp.where(s == nslab - 1, 0.0, nxt)
        up = jnp.where(row == 0, prev, pltpu.roll(u, 1, 0))
        un = jnp.where(row == n2 - 1, nxt, pltpu.roll(u, n2 - 1, 0))
        return b_ref[...] + up * w_ref[0:1, :] + u * w_ref[1:2, :] + un * w_ref[2:3, :]

    def prep(s, carry):
        start = pl.multiple_of(s * n2, n2)
        x0 = short_conv(x0_ref, w0_ref, b0_ref, s, start)
        x1 = short_conv(x1_ref, w1_ref, b1_ref, s, start)
        v = short_conv(v_ref, wv_ref, bv_ref, s, start) * x1
        v_scr[s] = v
        y_scr[s] = v * skip_ref[...]
        o_ref[pl.ds(start, n2), :] = x0 * _silu(ga_ref[pl.ds(start, n2), :])
        return carry

    lax.fori_loop(0, nslab, prep, 0)

    def freq(k1, carry):
        ar, ai = _stage_a(ca_ref, sa_ref, v_scr, k1, nslab)
        twr = tw_ref[0, k1]
        twi = tw_ref[1, k1]
        a = jnp.concatenate([ar * twr - ai * twi, ar * twi + ai * twr], axis=0)
        x = _dot(ffwd_ref[...], a.astype(BF16))
        xr, xi = x[:n2], x[n2:]
        kr = kf_ref[k1, 0]
        ki = kf_ref[k1, 1]
        z = jnp.concatenate([xr * kr - xi * ki, xr * ki + xi * kr], axis=0)
        b = _dot(finv_ref[...], z.astype(BF16))
        br, bi = b[:n2], b[n2:]
        bpr = br * twr + bi * twi
        bpi = bi * twr - br * twi
        for n1 in range(nslab):
            y_scr[n1] = y_scr[n1] + (cr_ref[k1, n1] * bpr - ci_ref[k1, n1] * bpi)
        return carry

    lax.fori_loop(0, nslab + 1, freq, 0)

    def fin(s, carry):
        start = pl.multiple_of(s * n2, n2)
        o_ref[pl.ds(start, n2), :] = (y_scr[s] * o_ref[pl.ds(start, n2), :])
        return carry

    lax.fori_loop(0, nslab, fin, 0)


def _dft_tables(n1, n2, nslab_fwd):
    n = n1 * n2
    h = n1 // 2
    k1 = np.arange(h + 1)[:, None]
    ang_f = 2.0 * np.pi * k1 * np.arange(nslab_fwd)[None, :] / n1
    ang_i = 2.0 * np.pi * k1 * np.arange(h)[None, :] / n1
    wgt = np.where((k1 == 0) | (k1 == h), 1.0, 2.0) / n
    ca = np.cos(ang_f).astype(np.float32)
    sa = (-np.sin(ang_f)).astype(np.float32)
    cr = (wgt * np.cos(ang_i)).astype(np.float32)
    ci = (wgt * np.sin(ang_i)).astype(np.float32)
    return jnp.asarray(ca), jnp.asarray(sa), jnp.asarray(cr), jnp.asarray(ci)


def _twiddles(n1, n2, ct):
    n = n1 * n2
    k1 = jnp.arange(n1 // 2 + 1, dtype=jnp.int32)[:, None]
    m = (k1 * jnp.arange(n2, dtype=jnp.int32)[None, :]) % n
    ang = m.astype(F32) * (2.0 * math.pi / n)
    tw = jnp.stack([jnp.cos(ang), -jnp.sin(ang)])
    return jnp.broadcast_to(tw[..., None], tw.shape + (ct,))


def _dft_cos_sin(n2, shift=0):
    idx = jnp.arange(n2, dtype=jnp.int32)
    m = ((idx[:, None] + shift) * idx[None, :]) % n2
    ang = m.astype(F32) * (2.0 * math.pi / n2)
    return jnp.cos(ang), jnp.sin(ang)


def _spectrum(kern, n1, n2, ct, ffwd_hi, ffwd_lo):
    ca, sa, _, _ = _dft_tables(n1, n2, n1)
    tw = _twiddles(n1, n2, ct)
    h = n1 // 2
    fixed2 = lambda c: (0, 0)
    return pl.pallas_call(
        functools.partial(_spectrum_kernel, n1=n1, n2=n2),
        grid=(D_A // ct,),
        in_specs=[_smem(), _smem(),
                  pl.BlockSpec((n1, n2, ct), lambda c: (0, 0, c)),
                  pl.BlockSpec((2, h + 1, n2, ct), lambda c: (0, 0, 0, 0)),
                  pl.BlockSpec((2 * n2, 2 * n2), fixed2),
                  pl.BlockSpec((2 * n2, 2 * n2), fixed2)],
        out_specs=pl.BlockSpec((h + 1, 2, n2, ct), lambda c: (0, 0, 0, c)),
        out_shape=jax.ShapeDtypeStruct((h + 1, 2, n2, D_A), F32),
        compiler_params=_params(("parallel",)),
        name="filter_spectrum",
    )(ca, sa, kern.reshape(n1, n2, D_A), tw, ffwd_hi, ffwd_lo)


def _hyena(ua, ga, conv_w, conv_b, skip, kf, n1, n2, ct, ffwd, finv):
    bsz, seq, _ = ua.shape
    nslab = n1 // 2
    nct = D_A // ct
    ca, sa, cr, ci = _dft_tables(n1, n2, nslab)
    tw = _twiddles(n1, n2, ct)
    cb = conv_b.reshape(1, 3 * D_A)
    sk = skip.reshape(1, D_A)
    part = lambda p: pl.BlockSpec((None, seq, ct), lambda c, b: (b, 0, p * nct + c))
    wpart = lambda p: pl.BlockSpec((3, ct), lambda c, b: (0, p * nct + c))
    bpart = lambda p: pl.BlockSpec((1, ct), lambda c, b: (0, p * nct + c))
    fixed2 = lambda c, b: (0, 0)
    return pl.pallas_call(
        functools.partial(_hyena_kernel, seq=seq, nslab=nslab, n2=n2),
        grid=(nct, bsz),
        in_specs=[_smem(), _smem(), _smem(), _smem(),
                  part(0), part(1), part(2),
                  pl.BlockSpec((None, seq, ct), lambda c, b: (b, 0, c)),
                  wpart(0), wpart(1), wpart(2), bpart(0), bpart(1), bpart(2),
                  pl.BlockSpec((1, ct), lambda c, b: (0, c)),
                  pl.BlockSpec((nslab + 1, 2, n2, ct), lambda c, b: (0, 0, 0, c)),
                  pl.BlockSpec((2, nslab + 1, n2, ct), lambda c, b: (0, 0, 0, 0)),
                  pl.BlockSpec((2 * n2, 2 * n2), fixed2),
                  pl.BlockSpec((2 * n2, 2 * n2), fixed2)],
        out_specs=pl.BlockSpec((None, seq, ct), lambda c, b: (b, 0, c)),
        out_shape=jax.ShapeDtypeStruct((bsz, seq, D_A), F32),
        scratch_shapes=[pltpu.VMEM((nslab, n2, ct), F32),
                        pltpu.VMEM((nslab, n2, ct), F32)],
        compiler_params=_params(("parallel", "parallel")),
        name="hyena",
    )(ca, sa, cr, ci, ua, ua, ua, ga, conv_w, conv_w, conv_w, cb, cb, cb, sk,
      kf, tw, ffwd, finv)


def _hyena_filter_kernel(z_ref, w1_ref, b1_ref, w2_ref, b2_ref, w3_ref, dec_ref, o_ref):
    hp = lax.Precision.HIGHEST
    h = jnp.sin(jnp.dot(z_ref[...], w1_ref[...], precision=hp,
                        preferred_element_type=F32) + b1_ref[...])
    h = jnp.sin(jnp.dot(h, w2_ref[...], precision=hp, preferred_element_type=F32) + b2_ref[...])
    o_ref[...] = jnp.dot(h, w3_ref[...], precision=hp, preferred_element_type=F32) * dec_ref[...]


def _hyena_kern(seq, w1, b1, w2, b2, w3):
    t = jnp.linspace(0.0, 1.0, seq, dtype=F32)[:, None]
    w = 2.0 * math.pi * jnp.arange(seq, dtype=F32)[:, None] / seq
    bands = (EMB_DIM - 1) // 2
    f = jnp.linspace(1e-4, bands - 1, bands, dtype=F32)[None, :]
    z = jnp.concatenate([t, jnp.cos(f * w), -jnp.sin(f * w)], axis=-1)
    zp = jnp.pad(z, ((0, 0), (0, LANES - EMB_DIM)))
    w1p = jnp.pad(w1, ((0, LANES - EMB_DIM), (0, 0)))
    deltas = jnp.abs(jnp.linspace(MIN_DECAY, MAX_DECAY, D_A, dtype=F32))
    decay = jnp.exp(-t * deltas)
    decay2 = jnp.concatenate([decay, decay], axis=-1)
    tm = min(seq, 512)
    fo = w1.shape[1]
    fixed = lambda i: (0, 0)
    h = pl.pallas_call(
        _hyena_filter_kernel,
        grid=(seq // tm,),
        in_specs=[pl.BlockSpec((tm, LANES), lambda i: (i, 0)),
                  pl.BlockSpec((LANES, fo), fixed),
                  pl.BlockSpec((1, fo), fixed),
                  pl.BlockSpec((fo, fo), fixed),
                  pl.BlockSpec((1, fo), fixed),
                  pl.BlockSpec((fo, 2 * D_A), fixed),
                  pl.BlockSpec((tm, 2 * D_A), lambda i: (i, 0))],
        out_specs=pl.BlockSpec((tm, 2 * D_A), lambda i: (i, 0)),
        out_shape=jax.ShapeDtypeStruct((seq, 2 * D_A), F32),
        compiler_params=_params(("parallel",)),
        name="hyena_filter",
    )(zp, w1p, b1.reshape(1, fo), w2, b2.reshape(1, fo), w3, decay2)
    h = h.reshape(seq, 2, D_A)
    h = h / jnp.sum(jnp.abs(h), axis=(0, 1), keepdims=True)
    h_fwd, h_bwd = h[:, 0], h[:, 1]
    return jnp.concatenate([h_fwd, jnp.zeros((1, D_A), F32), h_bwd[:0:-1]], axis=0)


def _qup_kernel(cq_ref, cos_ref, sin_ref, wa_ref, wb_ref, q_ref):
    cq = cq_ref[...]
    qa = _dot(cq, wa_ref[...])
    qb = _dot(cq, wb_ref[...])
    cos = jnp.concatenate([cos_ref[...]] * N_HEADS, axis=-1)
    sin = jnp.concatenate([sin_ref[...]] * N_HEADS, axis=-1)
    q_ref[...] = ((qa * cos + qb * sin) * ATTN_SCALE).astype(BF16)


def _qup(cq, cos, sin, wa, wb, tm):
    t = cq.shape[0]
    tiles_per_seq = cos.shape[0] // tm
    fixed = lambda i: (0, 0)
    pos = lambda i: (i % tiles_per_seq, 0)
    return pl.pallas_call(
        _qup_kernel,
        grid=(t // tm,),
        in_specs=[pl.BlockSpec((tm, Q_RANK), lambda i: (i, 0)),
                  pl.BlockSpec((tm, LANES), pos),
                  pl.BlockSpec((tm, LANES), pos),
                  pl.BlockSpec((Q_RANK, N_HEADS * HEAD_PAD), fixed),
                  pl.BlockSpec((Q_RANK, N_HEADS * HEAD_PAD), fixed)],
        out_specs=pl.BlockSpec((tm, N_HEADS * HEAD_PAD), lambda i: (i, 0)),
        out_shape=jax.ShapeDtypeStruct((t, N_HEADS * HEAD_PAD), BF16),
        compiler_params=_params(("parallel",)),
        name="q_up",
    )(cq, cos, sin, wa, wb)


def _kvup_kernel(ckv_ref, kra_ref, krb_ref, cos_ref, sin_ref, wk_ref, wv_ref, k_ref, v_ref):
    c = ckv_ref[...].astype(BF16)
    kr = (kra_ref[...] * cos_ref[...] + krb_ref[...] * sin_ref[...]).astype(BF16)
    kin = jnp.concatenate([c, kr], axis=-1)
    k_ref[...] = _dot(kin, wk_ref[...]).astype(BF16)
    v_ref[...] = _dot(c, wv_ref[...]).astype(BF16)


def _kvup(ckv, kra, krb, cos, sin, wk, wv, tm):
    t = ckv.shape[0]
    tiles_per_seq = cos.shape[0] // tm
    fixed = lambda i: (0, 0)
    row = lambda i: (i, 0)
    pos = lambda i: (i % tiles_per_seq, 0)
    return pl.pallas_call(
        _kvup_kernel,
        grid=(t // tm,),
        in_specs=[pl.BlockSpec((tm, KV_RANK), row),
                  pl.BlockSpec((tm, LANES), row),
                  pl.BlockSpec((tm, LANES), row),
                  pl.BlockSpec((tm, LANES), pos),
                  pl.BlockSpec((tm, LANES), pos),
                  pl.BlockSpec((KV_RANK + LANES, N_HEADS * HEAD_PAD), fixed),
                  pl.BlockSpec((KV_RANK, D_B), fixed)],
        out_specs=[pl.BlockSpec((tm, N_HEADS * HEAD_PAD), row),
                   pl.BlockSpec((tm, D_B), row)],
        out_shape=[jax.ShapeDtypeStruct((t, N_HEADS * HEAD_PAD), BF16),
                   jax.ShapeDtypeStruct((t, D_B), BF16)],
        compiler_params=_params(("parallel",)),
        name="kv_up",
    )(ckv, kra, krb, cos, sin, wk, wv)


def _attn_kernel(q_ref, k_ref, v_ref, gb_ref, o_ref):
    outs = []
    for hh in range(2):
        q = q_ref[:, hh * HEAD_PAD:(hh + 1) * HEAD_PAD]
        k = k_ref[:, hh * HEAD_PAD:(hh + 1) * HEAD_PAD]
        s = lax.dot_general(q, k, (((1,), (1,)), ((), ())), preferred_element_type=F32)
        p = jnp.exp(s - jnp.max(s, axis=-1, keepdims=True))
        denom = jnp.sum(p, axis=-1, keepdims=True)
        outs.append(_dot(p.astype(BF16), v_ref[...]) / denom)
    lane = lax.broadcasted_iota(jnp.int32, outs[0].shape, 1)
    o = jnp.where(lane < V_DIM, outs[0], outs[1])
    o_ref[...] = (o * _silu(gb_ref[...])).astype(BF16)


def _attention(q, k, v, gb, tq):
    bsz, lq, _ = q.shape
    lk = k.shape[1]
    return pl.pallas_call(
        _attn_kernel,
        grid=(bsz, N_HEADS // 2, lq // tq),
        in_specs=[pl.BlockSpec((None, tq, 2 * HEAD_PAD), lambda b, h, i: (b, i, h)),
                  pl.BlockSpec((None, lk, 2 * HEAD_PAD), lambda b, h, i: (b, 0, h)),
                  pl.BlockSpec((None, lk, 2 * V_DIM), lambda b, h, i: (b, 0, h)),
                  pl.BlockSpec((None, tq, 2 * V_DIM), lambda b, h, i: (b, i, h))],
        out_specs=pl.BlockSpec((None, tq, 2 * V_DIM), lambda b, h, i: (b, i, h)),
        out_shape=jax.ShapeDtypeStruct((bsz, lq, D_B), BF16),
        compiler_params=_params(("parallel", "parallel", "parallel")),
        name="mla_attention",
    )(q, k, v, gb)


def _mid_kernel(x_ref, ya_ref, ob_ref, gt_ref, wo_ref, sc_ref, sh_ref, wi_ref,
                x1_ref, u_ref, g_ref):
    y = _dot(ya_ref[...].astype(BF16), wo_ref[0:D_A, :]) + _dot(ob_ref[...], wo_ref[D_A:, :])
    x1 = x_ref[...] + gt_ref[...] * y
    x1_ref[...] = x1
    h = (_rms(x1) * sc_ref[...] + sh_ref[...]).astype(BF16)
    u_ref[...] = _dot(h, wi_ref[:, :D_C])
    g_ref[...] = _dot(h, wi_ref[:, D_C:])


def _mid(x2d, ya, ob, gt, wo, sc, sh, wi, tiles_per_mod, tm):
    t = x2d.shape[0]
    row = lambda i: (i, 0)
    fixed = lambda i: (0, 0)
    mod = lambda i: (i // tiles_per_mod, 0, 0)
    return pl.pallas_call(
        _mid_kernel,
        grid=(t // tm,),
        in_specs=[pl.BlockSpec((tm, D_MODEL), row),
                  pl.BlockSpec((tm, D_A), row),
                  pl.BlockSpec((tm, D_B), row),
                  pl.BlockSpec((None, 1, D_MODEL), mod),
                  pl.BlockSpec((D_A + D_B, D_MODEL), fixed),
                  pl.BlockSpec((None, 1, D_MODEL), mod),
                  pl.BlockSpec((None, 1, D_MODEL), mod),
                  pl.BlockSpec((D_MODEL, 2 * D_C), fixed)],
        out_specs=[pl.BlockSpec((tm, D_MODEL), row),
                   pl.BlockSpec((tm, D_C), row),
                   pl.BlockSpec((tm, D_C), row)],
        out_shape=[jax.ShapeDtypeStruct((t, D_MODEL), F32),
                   jax.ShapeDtypeStruct((t, D_C), F32),
                   jax.ShapeDtypeStruct((t, D_C), F32)],
        compiler_params=_params(("parallel",)),
        name="outproj_even_inproj_odd",
    )(x2d, ya, ob, gt, wo, sc, sh, wi)


def _fnet_weights_kernel(w_ref, cs_ref, o_ref):
    hp = lax.Precision.HIGHEST
    o_ref[...] = jnp.dot(cs_ref[...], w_ref[...], precision=hp,
                         preferred_element_type=F32).astype(BF16)


def _fnet_weights(fnet_w, seq):
    c, s = _dft_cos_sin(G_C)
    cs = jnp.concatenate([c, s], axis=0) * (1.0 / math.sqrt(seq * G_C))
    return pl.pallas_call(
        _fnet_weights_kernel,
        grid=(N_GROUPS_C,),
        in_specs=[pl.BlockSpec((None, G_C, G_C), lambda g: (g, 0, 0)),
                  pl.BlockSpec((2 * G_C, G_C), lambda g: (0, 0))],
        out_specs=pl.BlockSpec((None, 2 * G_C, G_C), lambda g: (g, 0, 0)),
        out_shape=jax.ShapeDtypeStruct((N_GROUPS_C, 2 * G_C, G_C), BF16),
        compiler_params=_params(("parallel",)),
        name="fnet_weights",
    )(fnet_w, cs)


def _fnet_long_kernel(ca_ref, sa_ref, u_ref, tw_ref, m_ref, ab_ref, o_ref, *, n1, n2):
    h = n1 // 2

    def group_map(xr, xi):
        return _dot(jnp.concatenate([xr, xi], axis=1).astype(BF16), ab_ref[...])

    def body(k1, carry):
        ar, ai = _stage_a(ca_ref, sa_ref, u_ref, k1, n1)
        twr = tw_ref[0, k1]
        twi = tw_ref[1, k1]
        a = jnp.concatenate([ar * twr - ai * twi, ar * twi + ai * twr], axis=0)
        x = _dot(m_ref[...], a.astype(BF16))
        o_ref[k1] = group_map(x[:n2], x[n2:2 * n2])

        @pl.when(jnp.logical_and(k1 > 0, k1 < h))
        def _():
            o_ref[n1 - k1] = group_map(x[2 * n2:3 * n2], x[3 * n2:])

        return carry

    lax.fori_loop(0, h + 1, body, 0)


def _fnet_long(u, ab, n1, n2):
    bsz, seq, _ = u.shape
    h = n1 // 2
    ca, sa, _, _ = _dft_tables(n1, n2, n1)
    tw = _twiddles(n1, n2, G_C)
    c, s = _dft_cos_sin(n2)
    ce, se = _dft_cos_sin(n2, shift=1)
    m = jnp.concatenate([jnp.concatenate([c, s], axis=1),
                         jnp.concatenate([-s, c], axis=1),
                         jnp.concatenate([ce, -se], axis=1),
                         jnp.concatenate([-se, -ce], axis=1)], axis=0).astype(BF16)
    out = pl.pallas_call(
        functools.partial(_fnet_long_kernel, n1=n1, n2=n2),
        grid=(bsz, N_GROUPS_C),
        in_specs=[_smem(), _smem(),
                  pl.BlockSpec((None, n1, n2, G_C), lambda b, g: (b, 0, 0, g)),
                  pl.BlockSpec((2, h + 1, n2, G_C), lambda b, g: (0, 0, 0, 0)),
                  pl.BlockSpec((4 * n2, 2 * n2), lambda b, g: (0, 0)),
                  pl.BlockSpec((None, 2 * G_C, G_C), lambda b, g: (g, 0, 0))],
        out_specs=pl.BlockSpec((None, n1, n2, G_C), lambda b, g: (b, 0, 0, g)),
        out_shape=jax.ShapeDtypeStruct((bsz, n1, n2, D_C), F32),
        compiler_params=_params(("parallel", "parallel")),
        name="fnet_long",
    )(ca, sa, u.reshape(bsz, n1, n2, D_C), tw, m, ab)
    return out.transpose(0, 2, 1, 3).reshape(bsz, seq, D_C)


def _fnet_short_kernel(u_ref, f_ref, ab_ref, o_ref, *, seq):
    x = _dot(f_ref[...], u_ref[...].astype(BF16))
    xr, xi = x[:seq], x[seq:]
    for g in range(N_GROUPS_C):
        sl = slice(g * G_C, (g + 1) * G_C)
        xin = jnp.concatenate([xr[:, sl], xi[:, sl]], axis=1).astype(BF16)
        o_ref[:, sl] = _dot(xin, ab_ref[g])


def _fnet_short(u, ab):
    bsz, seq, _ = u.shape
    c, s = _dft_cos_sin(seq)
    f = jnp.concatenate([c, -s], axis=0).astype(BF16)
    return pl.pallas_call(
        functools.partial(_fnet_short_kernel, seq=seq),
        grid=(bsz,),
        in_specs=[pl.BlockSpec((None, seq, D_C), lambda b: (b, 0, 0)),
                  pl.BlockSpec((2 * seq, seq), lambda b: (0, 0)),
                  pl.BlockSpec((N_GROUPS_C, 2 * G_C, G_C), lambda b: (0, 0, 0))],
        out_specs=pl.BlockSpec((None, seq, D_C), lambda b: (b, 0, 0)),
        out_shape=jax.ShapeDtypeStruct((bsz, seq, D_C), F32),
        compiler_params=_params(("parallel",)),
        name="fnet_short",
    )(u, f, ab)


def _final_kernel(x1_ref, y_ref, g_ref, gt_ref, wo_ref, fg_ref, o_ref):
    z = (y_ref[...] * _silu(g_ref[...])).astype(BF16)
    x2 = x1_ref[...] + gt_ref[...] * _dot(z, wo_ref[...])
    o_ref[...] = _rms(x2) * fg_ref[...]


def _final(x1, y, g, gt, wo, fg, tiles_per_mod, tm):
    t = x1.shape[0]
    row = lambda i: (i, 0)
    fixed = lambda i: (0, 0)
    mod = lambda i: (i // tiles_per_mod, 0, 0)
    return pl.pallas_call(
        _final_kernel,
        grid=(t // tm,),
        in_specs=[pl.BlockSpec((tm, D_MODEL), row),
                  pl.BlockSpec((tm, D_C), row),
                  pl.BlockSpec((tm, D_C), row),
                  pl.BlockSpec((None, 1, D_MODEL), mod),
                  pl.BlockSpec((D_C, D_MODEL), fixed),
                  pl.BlockSpec((1, D_MODEL), fixed)],
        out_specs=pl.BlockSpec((tm, D_MODEL), row),
        out_shape=jax.ShapeDtypeStruct((t, D_MODEL), F32),
        compiler_params=_params(("parallel",)),
        name="outproj_odd_final",
    )(x1, y, g, gt, wo, fg)


def _rope_swap(w):
    nf = ROPE_DIM // 4
    w4 = w.reshape(w.shape[:-1] + (2, 2, nf))
    return jnp.stack([-w4[..., 1, :], w4[..., 0, :]], axis=-2).reshape(w.shape)


def _pack_w_in_even(w):
    ua_ga_cq_ckv = w[:, :4 * D_A + Q_RANK + KV_RANK]
    o = 4 * D_A + Q_RANK + KV_RANK
    krope = w[:, o:o + ROPE_DIM]
    gb = w[:, o + ROPE_DIM:]
    zpad = jnp.zeros((w.shape[0], LANES - ROPE_DIM), w.dtype)
    return jnp.concatenate([ua_ga_cq_ckv, gb, krope, zpad, _rope_swap(krope), zpad],
                           axis=1).astype(BF16)


def _pack_w_uq(w):
    w3 = w.reshape(Q_RANK, N_HEADS, QK_DIM)
    zq = jnp.zeros((Q_RANK, N_HEADS, HEAD_PAD - QK_DIM), w.dtype)
    wa = jnp.concatenate([w3, zq], axis=-1)
    zn = jnp.zeros((Q_RANK, N_HEADS, NOPE_DIM), w.dtype)
    wb = jnp.concatenate([zn, _rope_swap(w3[..., NOPE_DIM:]), zq], axis=-1)
    shape = (Q_RANK, N_HEADS * HEAD_PAD)
    return wa.reshape(shape).astype(BF16), wb.reshape(shape).astype(BF16)


def _pack_w_ukv(w):
    w3 = w.reshape(KV_RANK, N_HEADS, NOPE_DIM + V_DIM)
    zk = jnp.zeros((KV_RANK, N_HEADS, HEAD_PAD - NOPE_DIM), w.dtype)
    wk_c = jnp.concatenate([w3[..., :NOPE_DIM], zk], axis=-1).reshape(KV_RANK, -1)
    eye = jnp.eye(LANES, ROPE_DIM, dtype=w.dtype)
    rope_rows = jnp.concatenate([jnp.zeros((LANES, NOPE_DIM), w.dtype), eye,
                                 jnp.zeros((LANES, HEAD_PAD - QK_DIM), w.dtype)], axis=-1)
    wk_r = jnp.tile(rope_rows, (1, N_HEADS))
    wk = jnp.concatenate([wk_c, wk_r], axis=0)
    wv = w3[..., NOPE_DIM:].reshape(KV_RANK, D_B)
    return wk.astype(BF16), wv.astype(BF16)


def _rope_lanes(seq):
    rows = seq // GRID_W
    r = jnp.repeat(jnp.arange(rows, dtype=F32), GRID_W)
    c = jnp.tile(jnp.arange(GRID_W, dtype=F32), rows)
    half = ROPE_DIM // 2
    inv = jnp.power(ROPE_BASE, -jnp.arange(0, half, 2, dtype=F32) / half)
    ang = jnp.concatenate([r[:, None] * inv] * 2 + [c[:, None] * inv] * 2, axis=-1)
    return jnp.cos(ang), jnp.sin(ang)


def _pad_lanes(x, offset):
    return jnp.pad(x, ((0, 0), (offset, LANES - offset - x.shape[1])))


def kernel(x_prompt, x_sample, c, cache_ckv, cache_krope, c_ctx, norm_g, w_mod, b_mod,
           w_in_e, conv_w, conv_b, filt_w1, filt_b1, filt_w2, filt_b2, filt_w3, hyena_skip,
           q_norm_g, kv_norm_g, w_uq, w_ukv, w_out_e, w_in_o, fnet_w, w_out_o, final_g):
    nb_p, seq_p, _ = x_prompt.shape
    nb_s, seq_s, _ = x_sample.shape
    past = cache_ckv.shape[2]
    tp = nb_p * seq_p
    ts = nb_s * seq_s
    tm = 512

    cond = jnp.concatenate([c_ctx[None, :], c, jnp.zeros((8 - 1 - nb_s, D_MODEL), F32)], axis=0)
    mod = _ada_params(cond, w_mod, b_mod)
    shift = mod[:, :, None, :D_MODEL]
    scale = (1.0 + mod[:, :, None, D_MODEL:2 * D_MODEL]) * norm_g[:, None, None, :]
    gate = mod[:, :, None, 2 * D_MODEL:]

    xp = x_prompt.reshape(tp, D_MODEL)
    xs = x_sample.reshape(ts, D_MODEL)

    w_in = _pack_w_in_even(w_in_e[0])
    qg = q_norm_g[0].reshape(1, Q_RANK)
    kvg = kv_norm_g[0].reshape(1, KV_RANK)
    pr_p = _inproj_even(xp, scale[0, :1], shift[0, :1], w_in, qg, kvg, tp // tm, tm)
    pr_s = _inproj_even(xs, scale[0, 1:1 + nb_s], shift[0, 1:1 + nb_s], w_in, qg, kvg,
                        seq_s // tm, tm)
    ua_p, ga_p, cq_p, ckv_p, gb_p, kra_p, krb_p = pr_p
    ua_s, ga_s, cq_s, ckv_s, gb_s, kra_s, krb_s = pr_s

    n2 = 256
    c2, s2 = _dft_cos_sin(n2)
    ffwd32 = jnp.concatenate([jnp.concatenate([c2, s2], axis=1),
                              jnp.concatenate([-s2, c2], axis=1)], axis=0)
    finv32 = jnp.concatenate([jnp.concatenate([c2, -s2], axis=1),
                              jnp.concatenate([s2, c2], axis=1)], axis=0)
    ffwd = ffwd32.astype(BF16)
    ffwd_lo = (ffwd32 - ffwd.astype(F32)).astype(BF16)
    finv = finv32.astype(BF16)
    filt = (filt_w1[0], filt_b1[0], filt_w2[0], filt_b2[0], filt_w3[0])
    ya = []
    for ua, ga, nb, seq, ct in ((ua_p, ga_p, nb_p, seq_p, D_A), (ua_s, ga_s, nb_s, seq_s, LANES)):
        n1 = 2 * seq // n2
        kern = _hyena_kern(seq, *filt)
        kf = _spectrum(kern, n1, n2, ct, ffwd, ffwd_lo)
        ya.append(_hyena(ua.reshape(nb, seq, 3 * D_A), ga.reshape(nb, seq, D_A),
                         conv_w[0], conv_b[0], hyena_skip[0], kf, n1, n2, ct, ffwd, finv))
    ya_p = ya[0].reshape(tp, D_A)
    ya_s = ya[1].reshape(ts, D_A)

    wq_a, wq_b = _pack_w_uq(w_uq[0])
    wk, wv = _pack_w_ukv(w_ukv[0])
    ones_q = _pad_lanes(jnp.ones((seq_p, QK_DIM), F32), 0)
    zeros_p = jnp.zeros((seq_p, LANES), F32)
    q_p = _qup(cq_p, ones_q, zeros_p, wq_a, wq_b, seq_p)
    k_p, v_p = _kvup(ckv_p, kra_p, krb_p, _pad_lanes(jnp.ones((seq_p, ROPE_DIM), F32), 0),
                     zeros_p, wk, wv, seq_p)
    ob_p = _attention(q_p.reshape(nb_p, seq_p, -1), k_p.reshape(nb_p, seq_p, -1),
                      v_p.reshape(nb_p, seq_p, -1), gb_p.reshape(nb_p, seq_p, D_B), seq_p)

    cos_r, sin_r = _rope_lanes(seq_s)
    cos_q = jnp.concatenate([jnp.ones((seq_s, NOPE_DIM), F32), cos_r,
                             jnp.zeros((seq_s, HEAD_PAD - QK_DIM), F32)], axis=1)
    sin_q = _pad_lanes(sin_r, NOPE_DIM)
    q_s = _qup(cq_s, cos_q, sin_q, wq_a, wq_b, tm)
    lk = past + seq_s
    cos_k = jnp.concatenate([_pad_lanes(jnp.ones((past, ROPE_DIM), F32), 0),
                             _pad_lanes(cos_r, 0)], axis=0)
    sin_k = jnp.concatenate([jnp.zeros((past, LANES), F32), _pad_lanes(sin_r, 0)], axis=0)
    ckv_all = jnp.concatenate([cache_ckv[:, 0], ckv_s.reshape(nb_s, seq_s, KV_RANK)], axis=1)
    ctx_kr = jnp.pad(cache_krope[:, 0], ((0, 0), (0, 0), (0, LANES - ROPE_DIM)))
    kra_all = jnp.concatenate([ctx_kr, kra_s.reshape(nb_s, seq_s, LANES)], axis=1)
    krb_all = jnp.concatenate([jnp.zeros_like(ctx_kr), krb_s.reshape(nb_s, seq_s, LANES)], axis=1)
    k_s, v_s = _kvup(ckv_all.reshape(nb_s * lk, KV_RANK), kra_all.reshape(nb_s * lk, LANES),
                     krb_all.reshape(nb_s * lk, LANES), cos_k, sin_k, wk, wv, tm)
    ob_s = _attention(q_s.reshape(nb_s, seq_s, -1), k_s.reshape(nb_s, lk, -1),
                      v_s.reshape(nb_s, lk, -1), gb_s.reshape(nb_s, seq_s, D_B), 256)

    wo_e = w_out_e[0].astype(BF16)
    wi_o = w_in_o[0].astype(BF16)
    x1_p, u_p, g_p = _mid(xp, ya_p, ob_p.reshape(tp, D_B), gate[0, :1], wo_e,
                          scale[1, :1], shift[1, :1], wi_o, tp // tm, tm)
    x1_s, u_s, g_s = _mid(xs, ya_s, ob_s.reshape(ts, D_B), gate[0, 1:1 + nb_s], wo_e,
                          scale[1, 1:1 + nb_s], shift[1, 1:1 + nb_s], wi_o, seq_s // tm, tm)

    y_p = _fnet_short(u_p.reshape(nb_p, seq_p, D_C), _fnet_weights(fnet_w[0], seq_p))
    y_s = _fnet_long(u_s.reshape(nb_s, seq_s, D_C), _fnet_weights(fnet_w[0], seq_s), 16,
                     seq_s // 16)
    wo_o = w_out_o[0].astype(BF16)
    fg = final_g.reshape(1, D_MODEL)
    out_p = _final(x1_p, y_p.reshape(tp, D_C), g_p, gate[1, :1], wo_o, fg, tp // tm, tm)
    out_s = _final(x1_s, y_s.reshape(ts, D_C), g_s, gate[1, 1:1 + nb_s], wo_o, fg,
                   seq_s // tm, tm)

    state_ckv = ckv_p.reshape(nb_p, 1, seq_p, KV_RANK)
    state_krope = kra_p[:, :ROPE_DIM].reshape(nb_p, 1, seq_p, ROPE_DIM)
    return (out_p.reshape(nb_p, seq_p, D_MODEL), out_s.reshape(nb_s, seq_s, D_MODEL),
            state_ckv, state_krope)
```

```python
import functools
import math

import numpy as np
import jax
import jax.numpy as jnp
from jax import lax
from jax.experimental import pallas as pl
from jax.experimental.pallas import tpu as pltpu

F32 = jnp.float32
BF16 = jnp.bfloat16

D_MODEL = 1024
GRID_W = 64
EPS = 1e-6
D_A = 512
EMB_DIM = 33
DECAY_TARGET = 1e-2
MIN_DECAY = math.log(DECAY_TARGET) / 1.5
MAX_DECAY = math.log(DECAY_TARGET) / 0.3
N_HEADS = 8
Q_RANK = 256
KV_RANK = 256
NOPE_DIM = 64
ROPE_DIM = 32
V_DIM = 64
ROPE_BASE = 10000.0
D_B = N_HEADS * V_DIM
QK_DIM = NOPE_DIM + ROPE_DIM
ATTN_SCALE = 1.0 / math.sqrt(QK_DIM)
LOG2_E = math.log2(math.e)
D_C = 1024
N_GROUPS_C = 8
G_C = D_C // N_GROUPS_C

LANES = 128
HEAD_PAD = 128
VMEM_LIMIT = 56 * 1024 * 1024

_C_UA, _C_GA, _C_CQ, _C_CKV, _C_GB, _C_KRA, _C_KRB, _C_END = (
    0, 1536, 2048, 2304, 2560, 3072, 3200, 3328)


def _params(sem):
    return pltpu.CompilerParams(dimension_semantics=sem, vmem_limit_bytes=VMEM_LIMIT)


def _rms(x):
    return x * lax.rsqrt(jnp.mean(x * x, axis=-1, keepdims=True) + EPS)


def _silu(g):
    return g * jax.nn.sigmoid(g)


def _dot(a, b):
    return jnp.dot(a, b, preferred_element_type=F32)


def _smem():
    return pl.BlockSpec(memory_space=pltpu.SMEM)


def _ada_kernel(cond_ref, w_ref, b_ref, o_ref):
    c = _silu(cond_ref[...])
    o_ref[...] = jnp.dot(c, w_ref[...], preferred_element_type=F32,
                         precision=lax.Precision.HIGHEST) + b_ref[...]


def _ada_params(cond, w_mod, b_mod):
    depth = w_mod.shape[0]
    n = cond.shape[0]
    tn = 1024
    return pl.pallas_call(
        _ada_kernel,
        grid=(depth, 3 * D_MODEL // tn),
        in_specs=[pl.BlockSpec((n, D_MODEL), lambda i, j: (0, 0)),
                  pl.BlockSpec((None, D_MODEL, tn), lambda i, j: (i, 0, j)),
                  pl.BlockSpec((None, 1, tn), lambda i, j: (i, 0, j))],
        out_specs=pl.BlockSpec((None, n, tn), lambda i, j: (i, 0, j)),
        out_shape=jax.ShapeDtypeStruct((depth, n, 3 * D_MODEL), F32),
        compiler_params=_params(("arbitrary", "arbitrary")),
        name="ada_params",
    )(cond, w_mod, b_mod.reshape(depth, 1, 3 * D_MODEL))


def _inproj_even_kernel(x_ref, sc_ref, sh_ref, w_ref, qg_ref, kvg_ref,
                        ua_ref, ga_ref, cq_ref, ckv_ref, gb_ref, kra_ref, krb_ref):
    h = (_rms(x_ref[...]) * sc_ref[...] + sh_ref[...]).astype(BF16)

    def proj(a, b):
        return _dot(h, w_ref[:, a:b])

    ua_ref[...] = proj(_C_UA, _C_GA)
    ga_ref[...] = proj(_C_GA, _C_CQ)
    cq_ref[...] = (_rms(proj(_C_CQ, _C_CKV)) * qg_ref[...]).astype(BF16)
    ckv_ref[...] = _rms(proj(_C_CKV, _C_GB)) * kvg_ref[...]
    gb_ref[...] = proj(_C_GB, _C_KRA)
    kra_ref[...] = proj(_C_KRA, _C_KRB)
    krb_ref[...] = proj(_C_KRB, _C_END)


def _inproj_even(x2d, sc, sh, w, qg, kvg, tiles_per_mod, tm):
    t = x2d.shape[0]
    row = lambda i: (i, 0)
    fixed = lambda i: (0, 0)
    mod = lambda i: (i // tiles_per_mod, 0, 0)
    widths = (3 * D_A, D_A, Q_RANK, KV_RANK, D_B, LANES, LANES)
    dtypes = (F32, F32, BF16, F32, F32, F32, F32)
    return pl.pallas_call(
        _inproj_even_kernel,
        grid=(t // tm,),
        in_specs=[pl.BlockSpec((tm, D_MODEL), row),
                  pl.BlockSpec((None, 1, D_MODEL), mod),
                  pl.BlockSpec((None, 1, D_MODEL), mod),
                  pl.BlockSpec((D_MODEL, _C_END), fixed),
                  pl.BlockSpec((1, Q_RANK), fixed),
                  pl.BlockSpec((1, KV_RANK), fixed)],
        out_specs=[pl.BlockSpec((tm, wd), row) for wd in widths],
        out_shape=[jax.ShapeDtypeStruct((t, wd), dt) for wd, dt in zip(widths, dtypes)],
        compiler_params=_params(("parallel",)),
        name="inproj_even",
    )(x2d, sc, sh, w, qg, kvg)


def _stage_a(ca_ref, sa_ref, x_ref, k1, nslab):
    ar = ca_ref[k1, 0] * x_ref[0]
    ai = sa_ref[k1, 0] * x_ref[0]
    for n1 in range(1, nslab):
        x = x_ref[n1]
        ar = ar + ca_ref[k1, n1] * x
        ai = ai + sa_ref[k1, n1] * x
    return ar, ai


def _acc(acc, coef, x):
    if x is None or abs(coef) < 1e-12:
        return acc
    if abs(abs(coef) - 1.0) < 1e-12:
        if acc is None:
            return x if coef > 0 else -x
        return acc + x if coef > 0 else acc - x
    return coef * x if acc is None else acc + coef * x


def _add(a, b):
    if a is None:
        return b
    return a if b is None else a + b


def _sub(a, b):
    if b is None:
        return a
    return -b if a is None else a - b


def _cmul(ar, ai, br, bi):
    def mul(x, y):
        return None if x is None or y is None else x * y
    return _sub(mul(ar, br), mul(ai, bi)), _add(mul(ar, bi), mul(ai, br))


def _radix_parts(h):
    singles = [0] + ([h // 2] if h >= 2 and h % 2 == 0 else [])
    pairs = [(n, h - n) for n in range(1, (h + 1) // 2)]
    return singles, pairs


def _radix_fwd(k1, h, single, pair_sum, pair_diff):
    singles, pairs = _radix_parts(h)
    ar = ai = None
    for n in singles:
        th = 2.0 * math.pi * n * k1 / (2 * h)
        x = single(n)
        ar = _acc(ar, math.cos(th), x)
        ai = _acc(ai, -math.sin(th), x)
    for i, (n, _) in enumerate(pairs):
        th = 2.0 * math.pi * n * k1 / (2 * h)
        re_src, im_src = (pair_sum, pair_diff) if k1 % 2 == 0 else (pair_diff, pair_sum)
        ar = _acc(ar, math.cos(th), re_src(i))
        ai = _acc(ai, -math.sin(th), im_src(i))
    return ar, ai


def _radix_inv(h, n_total, bpr, bpi, emit):
    singles, pairs = _radix_parts(h)

    def coefs(n, k1):
        th = 2.0 * math.pi * n * k1 / (2 * h)
        w = (1.0 if k1 in (0, h) else 2.0) / n_total
        return w * math.cos(th), w * math.sin(th)

    for n in singles:
        y = None
        for k1 in range(h + 1):
            c, s = coefs(n, k1)
            y = _acc(y, c, bpr(k1) if abs(c) > 1e-12 else None)
            y = _acc(y, -s, bpi(k1) if abs(s) > 1e-12 else None)
        emit(n, y)
    for n, m in pairs:
        acc = {(0, "c"): None, (0, "s"): None, (1, "c"): None, (1, "s"): None}
        for k1 in range(h + 1):
            c, s = coefs(n, k1)
            par = k1 % 2
            acc[(par, "c")] = _acc(acc[(par, "c")], c, bpr(k1) if abs(c) > 1e-12 else None)
            acc[(par, "s")] = _acc(acc[(par, "s")], s, bpi(k1) if abs(s) > 1e-12 else None)
        even_m, even_p = _sub(acc[(0, "c")], acc[(0, "s")]), _add(acc[(0, "c")], acc[(0, "s")])
        odd_m, odd_p = _sub(acc[(1, "c")], acc[(1, "s")]), _add(acc[(1, "c")], acc[(1, "s")])
        emit(n, _add(even_m, odd_m))
        emit(m, _sub(even_p, odd_p))


def _next_twiddle(tw, tw1_ref):
    if tw is None:
        return tw1_ref[0], tw1_ref[1]
    return _cmul(tw[0], tw[1], tw1_ref[0], tw1_ref[1])


def _spectrum_kernel(hf_ref, hb_ref, inv_ref, tw1_ref, fh_ref, fl_ref, kf_ref, sd_scr,
                     *, h, n2, nk_pad):
    ct = hf_ref.shape[-1]
    row = lax.broadcasted_iota(jnp.int32, (n2, ct), 0)
    _, pairs = _radix_parts(h)

    def hb(n):
        return jnp.where(row == 0, 0.0, hb_ref[0]) if n == 0 else hb_ref[n]

    for i, (n, m) in enumerate(pairs):
        sd_scr[0, 0, i] = hf_ref[n] + hf_ref[m]
        sd_scr[0, 1, i] = hf_ref[n] - hf_ref[m]
        sd_scr[1, 0, i] = hb(n) + hb(m)
        sd_scr[1, 1, i] = hb(n) - hb(m)

    inv = inv_ref[...]
    tw = None
    for k1 in range(h + 1):
        far, fai = _radix_fwd(k1, h, lambda n: hf_ref[n],
                              lambda i: sd_scr[0, 0, i], lambda i: sd_scr[0, 1, i])
        bar, bai = _radix_fwd(k1, h, hb, lambda i: sd_scr[1, 0, i], lambda i: sd_scr[1, 1, i])
        if k1 > 0:
            tw = _next_twiddle(tw, tw1_ref)
            far, fai = _cmul(far, fai, tw[0], tw[1])
            bar, bai = _cmul(bar, bai, tw[0], tw[1])
        zero = jnp.zeros((n2, ct), F32)
        af = jnp.concatenate([zero if far is None else far, zero if fai is None else fai], axis=0)
        ab = jnp.concatenate([zero if bar is None else bar, zero if bai is None else bai], axis=0)
        a = jnp.concatenate([af, ab], axis=1)
        a_hi = a.astype(BF16)
        a_lo = (a - a_hi.astype(F32)).astype(BF16)
        x = _dot(fh_ref[...], a_hi) + (_dot(fh_ref[...], a_lo) + _dot(fl_ref[...], a_hi))
        lanes = slice(k1 * ct, (k1 + 1) * ct)
        kf_ref[0, :, lanes] = (x[:n2, :ct] + x[:n2, ct:]) * inv
        kf_ref[1, :, lanes] = (x[n2:, :ct] - x[n2:, ct:]) * inv
    if nk_pad > h + 1:
        pad = slice((h + 1) * ct, nk_pad * ct)
        kf_ref[:, :, pad] = jnp.zeros((2, n2, (nk_pad - h - 1) * ct), F32)


def _hyena_kernel(x0_ref, x1_ref, v_ref, ga_ref,
                  w0_ref, w1_ref, wv_ref, b0_ref, b1_ref, bv_ref, skip_ref,
                  kf_ref, tw1_ref, f_ref,
                  o_ref, v_scr, sd_scr, a_scr, b_scr, *, seq, nslab, n2, nk_pad):
    ct = o_ref.shape[-1]
    h = nslab
    chunk = 2 * LANES
    kpc = chunk // ct
    row = lax.broadcasted_iota(jnp.int32, (n2, ct), 0)

    def short_conv(ref, w_ref, b_ref, s, start):
        u = ref[pl.ds(start, n2), :]
        prev = ref[pl.ds(jnp.maximum(start - 1, 0), 1), :]
        prev = jnp.where(s == 0, 0.0, prev)
        nxt = ref[pl.ds(jnp.minimum(start + n2, seq - 1), 1), :]
        nxt = jnp.where(s == nslab - 1, 0.0, nxt)
        up = jnp.where(row == 0, prev, pltpu.roll(u, 1, 0))
        un = jnp.where(row == n2 - 1, nxt, pltpu.roll(u, n2 - 1, 0))
        return b_ref[...] + up * w_ref[0:1, :] + u * w_ref[1:2, :] + un * w_ref[2:3, :]

    def prep(s, carry):
        start = pl.multiple_of(s * n2, n2)
        x0 = short_conv(x0_ref, w0_ref, b0_ref, s, start)
        x1 = short_conv(x1_ref, w1_ref, b1_ref, s, start)
        v_scr[s] = short_conv(v_ref, wv_ref, bv_ref, s, start) * x1
        o_ref[pl.ds(start, n2), :] = x0 * _silu(ga_ref[pl.ds(start, n2), :])
        return carry

    lax.fori_loop(0, nslab, prep, 0)

    _, pairs = _radix_parts(h)
    for i, (n, m) in enumerate(pairs):
        sd_scr[0, i] = v_scr[n] + v_scr[m]
        sd_scr[1, i] = v_scr[n] - v_scr[m]
    zero = jnp.zeros((n2, ct), BF16)
    tw = None
    for k1 in range(h + 1):
        ar, ai = _radix_fwd(k1, h, lambda n: v_scr[n], lambda i: sd_scr[0, i],
                            lambda i: sd_scr[1, i])
        if k1 > 0:
            tw = _next_twiddle(tw, tw1_ref)
            ar, ai = _cmul(ar, ai, tw[0], tw[1])
        lanes = slice(k1 * ct, (k1 + 1) * ct)
        a_scr[0:n2, lanes] = zero if ar is None else ar.astype(BF16)
        a_scr[n2:2 * n2, lanes] = zero if ai is None else ai.astype(BF16)
    for k1 in range(h + 1, nk_pad):
        lanes = slice(k1 * ct, (k1 + 1) * ct)
        a_scr[0:n2, lanes] = zero
        a_scr[n2:2 * n2, lanes] = zero

    tw = None
    for j in range(nk_pad // kpc):
        cols = slice(j * chunk, (j + 1) * chunk)
        x = _dot(f_ref[...], a_scr[:, cols])
        xr, xi = x[:n2], x[n2:]
        kr = kf_ref[0, :, cols]
        ki = kf_ref[1, :, cols]
        z = jnp.concatenate([xr * kr - xi * ki, -(xr * ki + xi * kr)], axis=0)
        bt = _dot(f_ref[...], z.astype(BF16))
        for kk in range(kpc):
            k1 = j * kpc + kk
            if k1 > h:
                continue
            lanes = slice(kk * ct, (kk + 1) * ct)
            br = bt[:n2, lanes]
            bi = -bt[n2:, lanes]
            if k1 > 0:
                tw = _next_twiddle(tw, tw1_ref)
                br, bi = _cmul(br, bi, tw[0], -tw[1])
            b_scr[0, k1] = br
            if k1 not in (0, h):
                b_scr[1, k1] = bi

    def emit(n, y):
        rows = pl.ds(n * n2, n2)
        o_ref[rows, :] = (y + v_scr[n] * skip_ref[...]) * o_ref[rows, :]

    _radix_inv(h, 2 * seq, lambda k1: b_scr[0, k1], lambda k1: b_scr[1, k1], emit)


def _dft_tables(n1, n2, nslab_fwd):
    n = n1 * n2
    h = n1 // 2
    k1 = np.arange(h + 1)[:, None]
    ang_f = 2.0 * np.pi * k1 * np.arange(nslab_fwd)[None, :] / n1
    ang_i = 2.0 * np.pi * k1 * np.arange(h)[None, :] / n1
    wgt = np.where((k1 == 0) | (k1 == h), 1.0, 2.0) / n
    ca = np.cos(ang_f).astype(np.float32)
    sa = (-np.sin(ang_f)).astype(np.float32)
    cr = (wgt * np.cos(ang_i)).astype(np.float32)
    ci = (wgt * np.sin(ang_i)).astype(np.float32)
    return jnp.asarray(ca), jnp.asarray(sa), jnp.asarray(cr), jnp.asarray(ci)


def _twiddles(n1, n2, ct):
    n = n1 * n2
    ang = 2.0 * np.pi * (np.arange(n1 // 2 + 1)[:, None] * np.arange(n2)[None, :]) / n
    tw = np.stack([np.cos(ang), -np.sin(ang)]).astype(np.float32)
    return jnp.broadcast_to(jnp.asarray(tw)[..., None], tw.shape + (ct,))


def _dft_cos_sin(n2, shift=0):
    idx = np.arange(n2)
    ang = 2.0 * np.pi * (((idx[:, None] + shift) * idx[None, :]) % n2) / n2
    return np.cos(ang), np.sin(ang)


def _bf16_split(x):
    x32 = jnp.asarray(x.astype(np.float32))
    hi = x32.astype(BF16)
    lo = (x32 - hi.astype(F32)).astype(BF16)
    return hi, lo


def _dft_block(n2):
    c, s = _dft_cos_sin(n2)
    return np.block([[c, s], [-s, c]])


def _hyena_plan(seq):
    n2 = min(seq, 512)
    n1 = 2 * seq // n2
    ct = LANES if seq > 512 else 2 * LANES
    kpc = 2 * LANES // ct
    nk_pad = -(-(n1 // 2 + 1) // kpc) * kpc
    return n1, n2, ct, nk_pad


def _twiddle1(n1, n2, ct):
    ang = 2.0 * np.pi * np.arange(n2) / (n1 * n2)
    tw = np.stack([np.cos(ang), -np.sin(ang)]).astype(np.float32)
    return jnp.broadcast_to(jnp.asarray(tw)[..., None], tw.shape + (ct,))


def _spectrum(h, inv_norm, seq):
    n1, n2, ct, nk_pad = _hyena_plan(seq)
    nslab = n1 // 2
    nct = D_A // ct
    npair = len(_radix_parts(nslab)[1])
    ffwd_hi, ffwd_lo = _bf16_split(_dft_block(n2))
    h3 = h.reshape(nslab, n2, 2 * D_A)
    fixed2 = lambda c: (0, 0)
    return pl.pallas_call(
        functools.partial(_spectrum_kernel, h=nslab, n2=n2, nk_pad=nk_pad),
        grid=(nct,),
        in_specs=[pl.BlockSpec((nslab, n2, ct), lambda c: (0, 0, c)),
                  pl.BlockSpec((nslab, n2, ct), lambda c: (0, 0, nct + c)),
                  pl.BlockSpec((1, ct), lambda c: (0, c)),
                  pl.BlockSpec((2, n2, ct), lambda c: (0, 0, 0)),
                  pl.BlockSpec((2 * n2, 2 * n2), fixed2),
                  pl.BlockSpec((2 * n2, 2 * n2), fixed2)],
        out_specs=pl.BlockSpec((None, 2, n2, nk_pad * ct), lambda c: (c, 0, 0, 0)),
        out_shape=jax.ShapeDtypeStruct((nct, 2, n2, nk_pad * ct), F32),
        scratch_shapes=[pltpu.VMEM((2, 2, max(npair, 1), n2, ct), F32)],
        compiler_params=_params(("parallel",)),
        name="filter_spectrum",
    )(h3, h3, inv_norm, _twiddle1(n1, n2, ct), ffwd_hi, ffwd_lo)


def _hyena(ua, ga, conv_w, conv_b, skip, kf):
    bsz, seq, _ = ua.shape
    n1, n2, ct, nk_pad = _hyena_plan(seq)
    nslab = n1 // 2
    nct = D_A // ct
    npair = len(_radix_parts(nslab)[1])
    cb = conv_b.reshape(1, 3 * D_A)
    sk = skip.reshape(1, D_A)
    part = lambda p: pl.BlockSpec((None, seq, ct), lambda c, b: (b, 0, p * nct + c))
    wpart = lambda p: pl.BlockSpec((3, ct), lambda c, b: (0, p * nct + c))
    bpart = lambda p: pl.BlockSpec((1, ct), lambda c, b: (0, p * nct + c))
    return pl.pallas_call(
        functools.partial(_hyena_kernel, seq=seq, nslab=nslab, n2=n2, nk_pad=nk_pad),
        grid=(nct, bsz),
        in_specs=[part(0), part(1), part(2),
                  pl.BlockSpec((None, seq, ct), lambda c, b: (b, 0, c)),
                  wpart(0), wpart(1), wpart(2), bpart(0), bpart(1), bpart(2),
                  pl.BlockSpec((1, ct), lambda c, b: (0, c)),
                  pl.BlockSpec((None, 2, n2, nk_pad * ct), lambda c, b: (c, 0, 0, 0)),
                  pl.BlockSpec((2, n2, ct), lambda c, b: (0, 0, 0)),
                  pl.BlockSpec((2 * n2, 2 * n2), lambda c, b: (0, 0))],
        out_specs=pl.BlockSpec((None, seq, ct), lambda c, b: (b, 0, c)),
        out_shape=jax.ShapeDtypeStruct((bsz, seq, D_A), F32),
        scratch_shapes=[pltpu.VMEM((nslab, n2, ct), F32),
                        pltpu.VMEM((2, max(npair, 1), n2, ct), F32),
                        pltpu.VMEM((2 * n2, nk_pad * ct), BF16),
                        pltpu.VMEM((2, nk_pad, n2, ct), F32)],
        compiler_params=_params(("parallel", "parallel")),
        name="hyena",
    )(ua, ua, ua, ga, conv_w, conv_w, conv_w, cb, cb, cb, sk, kf, _twiddle1(n1, n2, ct),
      _bf16_split(_dft_block(n2))[0])


def _hyena_filter_kernel(fv_ref, w1_ref, b1_ref, w2_ref, b2_ref, w3_ref, del_ref,
                         o_ref, nrm_ref, *, seq):
    hp = lax.Precision.HIGHEST
    tm = o_ref.shape[0]
    i = pl.program_id(0)
    bands = (EMB_DIM - 1) // 2
    pos = (i * tm + lax.broadcasted_iota(jnp.int32, (tm, LANES), 0)).astype(F32)
    lane = lax.broadcasted_iota(jnp.int32, (tm, LANES), 1)
    t = pos * (1.0 / (seq - 1))
    ang = (pos * (2.0 * math.pi / seq)) * fv_ref[...]
    z = jnp.where(lane == 0, t,
                  jnp.where(lane <= bands, jnp.cos(ang),
                            jnp.where(lane < EMB_DIM, -jnp.sin(ang), 0.0)))
    h = jnp.sin(jnp.dot(z, w1_ref[...], precision=hp, preferred_element_type=F32) + b1_ref[...])
    h = jnp.sin(jnp.dot(h, w2_ref[...], precision=hp, preferred_element_type=F32) + b2_ref[...])
    tcol = (i * tm + lax.broadcasted_iota(jnp.int32, (tm, 1), 0)).astype(F32) * (1.0 / (seq - 1))
    h = jnp.dot(h, w3_ref[...], precision=hp, preferred_element_type=F32) * jnp.exp(
        -tcol * del_ref[...])
    o_ref[...] = h

    @pl.when(i == 0)
    def _():
        nrm_ref[...] = jnp.zeros_like(nrm_ref)

    nrm_ref[...] += jnp.sum(jnp.abs(h), axis=0, keepdims=True)


def _hyena_filter(seq, w1, b1, w2, b2, w3):
    bands = (EMB_DIM - 1) // 2
    f = np.linspace(1e-4, bands - 1, bands)
    fv = np.zeros((1, LANES), np.float32)
    fv[0, 1:1 + bands] = f
    fv[0, 1 + bands:EMB_DIM] = f
    deltas = np.abs(np.linspace(MIN_DECAY, MAX_DECAY, D_A)).astype(np.float32)
    deltas2 = np.concatenate([deltas, deltas])[None, :]
    w1p = jnp.pad(w1, ((0, LANES - EMB_DIM), (0, 0)))
    tm = min(seq, 512)
    fo = w1.shape[1]
    fixed = lambda i: (0, 0)
    h, nrm = pl.pallas_call(
        functools.partial(_hyena_filter_kernel, seq=seq),
        grid=(seq // tm,),
        in_specs=[pl.BlockSpec((1, LANES), fixed),
                  pl.BlockSpec((LANES, fo), fixed),
                  pl.BlockSpec((1, fo), fixed),
                  pl.BlockSpec((fo, fo), fixed),
                  pl.BlockSpec((1, fo), fixed),
                  pl.BlockSpec((fo, 2 * D_A), fixed),
                  pl.BlockSpec((1, 2 * D_A), fixed)],
        out_specs=[pl.BlockSpec((tm, 2 * D_A), lambda i: (i, 0)),
                   pl.BlockSpec((1, 2 * D_A), fixed)],
        out_shape=[jax.ShapeDtypeStruct((seq, 2 * D_A), F32),
                   jax.ShapeDtypeStruct((1, 2 * D_A), F32)],
        compiler_params=_params(("arbitrary",)),
        name="hyena_filter",
    )(jnp.asarray(fv), w1p, b1.reshape(1, fo), w2, b2.reshape(1, fo), w3, jnp.asarray(deltas2))
    return h, 1.0 / (nrm[:, :D_A] + nrm[:, D_A:])


def _kv_new_rows(ckv_ref, kra_ref, krb_ref, cos_ref, sin_ref, wk_ref, wv_ref, k_ref, v_ref):
    c = ckv_ref[...].astype(BF16)
    kr = (kra_ref[...] * cos_ref[...] + krb_ref[...] * sin_ref[...]).astype(BF16)
    kin = jnp.concatenate([c, kr], axis=-1)
    k_ref[...] = _dot(kin, wk_ref[...]).astype(BF16)
    v_ref[...] = _dot(c, wv_ref[...]).astype(BF16)


def _kvup_kernel(ckv_ref, kra_ref, krb_ref, cos_ref, sin_ref, wk_ref, wv_ref, k_ref, v_ref):
    _kv_new_rows(ckv_ref, kra_ref, krb_ref, cos_ref, sin_ref, wk_ref, wv_ref, k_ref, v_ref)


def _kvup(ckv, kra, krb, cos, sin, wk, wv, tm):
    t = ckv.shape[0]
    tiles_per_seq = cos.shape[0] // tm
    fixed = lambda i: (0, 0)
    row = lambda i: (i, 0)
    pos = lambda i: (i % tiles_per_seq, 0)
    return pl.pallas_call(
        _kvup_kernel,
        grid=(t // tm,),
        in_specs=[pl.BlockSpec((tm, KV_RANK), row),
                  pl.BlockSpec((tm, LANES), row),
                  pl.BlockSpec((tm, LANES), row),
                  pl.BlockSpec((tm, LANES), pos),
                  pl.BlockSpec((tm, LANES), pos),
                  pl.BlockSpec((KV_RANK + LANES, N_HEADS * HEAD_PAD), fixed),
                  pl.BlockSpec((KV_RANK, D_B), fixed)],
        out_specs=[pl.BlockSpec((tm, N_HEADS * HEAD_PAD), row),
                   pl.BlockSpec((tm, D_B), row)],
        out_shape=[jax.ShapeDtypeStruct((t, N_HEADS * HEAD_PAD), BF16),
                   jax.ShapeDtypeStruct((t, D_B), BF16)],
        compiler_params=_params(("parallel",)),
        name="kv_up",
    )(ckv, kra, krb, cos, sin, wk, wv)


def _kvup_cached_kernel(cckv_ref, ckr_ref, ckv_ref, kra_ref, krb_ref, cos_ref, sin_ref,
                        wk_ref, wv_ref, k_ref, v_ref, *, nctx):
    t = pl.program_id(1)

    @pl.when(t < nctx)
    def _():
        c = cckv_ref[...].astype(BF16)
        kr = ckr_ref[...].astype(BF16)
        k_ref[...] = (_dot(c, wk_ref[0:KV_RANK, :])
                      + _dot(kr, wk_ref[KV_RANK:KV_RANK + ROPE_DIM, :])).astype(BF16)
        v_ref[...] = _dot(c, wv_ref[...]).astype(BF16)

    @pl.when(t >= nctx)
    def _():
        _kv_new_rows(ckv_ref, kra_ref, krb_ref, cos_ref, sin_ref, wk_ref, wv_ref, k_ref, v_ref)


def _kvup_cached(cache_ckv, cache_krope, layer, ckv, kra, krb, cos, sin, wk, wv, tm):
    bsz, seq, _ = ckv.shape
    past = cache_ckv.shape[2]
    nctx = past // tm
    ntile = nctx + seq // tm
    fixed = lambda b, t: (0, 0)
    ctx = lambda b, t: (b, layer, jnp.minimum(t, nctx - 1), 0)
    new = lambda b, t: (b, jnp.maximum(t - nctx, 0), 0)
    pos = lambda b, t: (jnp.maximum(t - nctx, 0), 0)
    out = lambda b, t: (b, t, 0)
    return pl.pallas_call(
        functools.partial(_kvup_cached_kernel, nctx=nctx),
        grid=(bsz, ntile),
        in_specs=[pl.BlockSpec((None, None, tm, KV_RANK), ctx),
                  pl.BlockSpec((None, None, tm, ROPE_DIM), ctx),
                  pl.BlockSpec((None, tm, KV_RANK), new),
                  pl.BlockSpec((None, tm, LANES), new),
                  pl.BlockSpec((None, tm, LANES), new),
                  pl.BlockSpec((tm, LANES), pos),
                  pl.BlockSpec((tm, LANES), pos),
                  pl.BlockSpec((KV_RANK + LANES, N_HEADS * HEAD_PAD), fixed),
                  pl.BlockSpec((KV_RANK, D_B), fixed)],
        out_specs=[pl.BlockSpec((None, tm, N_HEADS * HEAD_PAD), out),
                   pl.BlockSpec((None, tm, D_B), out)],
        out_shape=[jax.ShapeDtypeStruct((bsz, past + seq, N_HEADS * HEAD_PAD), BF16),
                   jax.ShapeDtypeStruct((bsz, past + seq, D_B), BF16)],
        compiler_params=_params(("parallel", "arbitrary")),
        name="kv_up_cached",
    )(cache_ckv, cache_krope, ckv, kra, krb, cos, sin, wk, wv)


def _attn_kernel(cq_ref, cos_ref, sin_ref, wa_ref, wb_ref, k_ref, v_ref, gb_ref, o_ref):
    cq = cq_ref[...]
    cos = jnp.concatenate([cos_ref[...]] * N_HEADS, axis=-1)
    sin = jnp.concatenate([sin_ref[...]] * N_HEADS, axis=-1)
    q = (_dot(cq, wa_ref[...]) * cos + _dot(cq, wb_ref[...]) * sin) * (ATTN_SCALE * LOG2_E)
    q = q.astype(BF16)
    lane = lax.broadcasted_iota(jnp.int32, (q.shape[0], 2 * V_DIM), 1)
    for pair in range(N_HEADS // 2):
        vsl = slice(pair * 2 * V_DIM, (pair + 1) * 2 * V_DIM)
        outs = []
        for hh in range(2):
            hsl = slice((2 * pair + hh) * HEAD_PAD, (2 * pair + hh + 1) * HEAD_PAD)
            s = lax.dot_general(q[:, hsl], k_ref[:, hsl], (((1,), (1,)), ((), ())),
                                preferred_element_type=F32)
            p = jnp.exp2(s - jnp.max(s, axis=-1, keepdims=True))
            denom = jnp.sum(p, axis=-1, keepdims=True)
            outs.append(_dot(p.astype(BF16), v_ref[:, vsl]) / denom)
        o = jnp.where(lane < V_DIM, outs[0], outs[1])
        o_ref[:, vsl] = (o * _silu(gb_ref[:, vsl])).astype(BF16)


def _attention(cq, cos, sin, wa, wb, k, v, gb, tq):
    bsz, lq, _ = cq.shape
    lk = k.shape[1]
    fixed = lambda b, i: (0, 0)
    qrow = lambda b, i: (b, i, 0)
    kv = lambda b, i: (b, 0, 0)
    pos = lambda b, i: (i, 0)
    return pl.pallas_call(
        _attn_kernel,
        grid=(bsz, lq // tq),
        in_specs=[pl.BlockSpec((None, tq, Q_RANK), qrow),
                  pl.BlockSpec((tq, LANES), pos),
                  pl.BlockSpec((tq, LANES), pos),
                  pl.BlockSpec((Q_RANK, N_HEADS * HEAD_PAD), fixed),
                  pl.BlockSpec((Q_RANK, N_HEADS * HEAD_PAD), fixed),
                  pl.BlockSpec((None, lk, N_HEADS * HEAD_PAD), kv),
                  pl.BlockSpec((None, lk, D_B), kv),
                  pl.BlockSpec((None, tq, D_B), qrow)],
        out_specs=pl.BlockSpec((None, tq, D_B), qrow),
        out_shape=jax.ShapeDtypeStruct((bsz, lq, D_B), BF16),
        compiler_params=_params(("parallel", "arbitrary")),
        name="mla_attention",
    )(cq, cos, sin, wa, wb, k, v, gb)


def _mid_kernel(x_ref, ya_ref, ob_ref, gt_ref, wo_ref, sc_ref, sh_ref, wi_ref,
                x1_ref, u_ref, g_ref):
    y = _dot(ya_ref[...].astype(BF16), wo_ref[0:D_A, :]) + _dot(ob_ref[...], wo_ref[D_A:, :])
    x1 = x_ref[...] + gt_ref[...] * y
    x1_ref[...] = x1
    h = (_rms(x1) * sc_ref[...] + sh_ref[...]).astype(BF16)
    u_ref[...] = _dot(h, wi_ref[:, :D_C])
    g_ref[...] = _dot(h, wi_ref[:, D_C:])


def _mid(x2d, ya, ob, gt, wo, sc, sh, wi, tiles_per_mod, tm):
    t = x2d.shape[0]
    row = lambda i: (i, 0)
    fixed = lambda i: (0, 0)
    mod = lambda i: (i // tiles_per_mod, 0, 0)
    return pl.pallas_call(
        _mid_kernel,
        grid=(t // tm,),
        in_specs=[pl.BlockSpec((tm, D_MODEL), row),
                  pl.BlockSpec((tm, D_A), row),
                  pl.BlockSpec((tm, D_B), row),
                  pl.BlockSpec((None, 1, D_MODEL), mod),
                  pl.BlockSpec((D_A + D_B, D_MODEL), fixed),
                  pl.BlockSpec((None, 1, D_MODEL), mod),
                  pl.BlockSpec((None, 1, D_MODEL), mod),
                  pl.BlockSpec((D_MODEL, 2 * D_C), fixed)],
        out_specs=[pl.BlockSpec((tm, D_MODEL), row),
                   pl.BlockSpec((tm, D_C), row),
                   pl.BlockSpec((tm, D_C), row)],
        out_shape=[jax.ShapeDtypeStruct((t, D_MODEL), F32),
                   jax.ShapeDtypeStruct((t, D_C), F32),
                   jax.ShapeDtypeStruct((t, D_C), F32)],
        compiler_params=_params(("parallel",)),
        name="outproj_even_inproj_odd",
    )(x2d, ya, ob, gt, wo, sc, sh, wi)


def _fnet_weights_kernel(w_ref, cs_ref, o_ref):
    hp = lax.Precision.HIGHEST
    o_ref[...] = jnp.dot(cs_ref[...], w_ref[...], precision=hp,
                         preferred_element_type=F32).astype(BF16)


def _fnet_weights(fnet_w, seq):
    c, s = _dft_cos_sin(G_C)
    cs = jnp.asarray((np.concatenate([c, s], axis=0) / math.sqrt(seq * G_C)).astype(np.float32))
    return pl.pallas_call(
        _fnet_weights_kernel,
        grid=(N_GROUPS_C,),
        in_specs=[pl.BlockSpec((None, G_C, G_C), lambda g: (g, 0, 0)),
                  pl.BlockSpec((2 * G_C, G_C), lambda g: (0, 0))],
        out_specs=pl.BlockSpec((None, 2 * G_C, G_C), lambda g: (g, 0, 0)),
        out_shape=jax.ShapeDtypeStruct((N_GROUPS_C, 2 * G_C, G_C), BF16),
        compiler_params=_params(("parallel",)),
        name="fnet_weights",
    )(fnet_w, cs)


def _fnet_long_kernel(ca_ref, sa_ref, u_ref, tw_ref, m_ref, ab_ref, o_ref, *, n1, n2):
    h = n1 // 2

    def group_map(xr, xi):
        return _dot(jnp.concatenate([xr, xi], axis=1).astype(BF16), ab_ref[...])

    def body(k1, carry):
        ar, ai = _stage_a(ca_ref, sa_ref, u_ref, k1, n1)
        twr = tw_ref[0, k1]
        twi = tw_ref[1, k1]
        a = jnp.concatenate([ar * twr - ai * twi, ar * twi + ai * twr], axis=0)
        x = _dot(m_ref[...], a.astype(BF16))
        o_ref[k1] = group_map(x[:n2], x[n2:2 * n2])

        @pl.when(jnp.logical_and(k1 > 0, k1 < h))
        def _():
            o_ref[n1 - k1] = group_map(x[2 * n2:3 * n2], x[3 * n2:])

        return carry

    lax.fori_loop(0, h + 1, body, 0)


def _fnet_long(u, ab, n1, n2):
    bsz, seq, _ = u.shape
    h = n1 // 2
    ca, sa, _, _ = _dft_tables(n1, n2, n1)
    tw = _twiddles(n1, n2, G_C)
    c, s = _dft_cos_sin(n2)
    ce, se = _dft_cos_sin(n2, shift=1)
    m = _bf16_split(np.block([[c, s], [-s, c], [ce, -se], [-se, -ce]]))[0]
    out = pl.pallas_call(
        functools.partial(_fnet_long_kernel, n1=n1, n2=n2),
        grid=(bsz, N_GROUPS_C),
        in_specs=[_smem(), _smem(),
                  pl.BlockSpec((None, n1, n2, G_C), lambda b, g: (b, 0, 0, g)),
                  pl.BlockSpec((2, h + 1, n2, G_C), lambda b, g: (0, 0, 0, 0)),
                  pl.BlockSpec((4 * n2, 2 * n2), lambda b, g: (0, 0)),
                  pl.BlockSpec((None, 2 * G_C, G_C), lambda b, g: (g, 0, 0))],
        out_specs=pl.BlockSpec((None, n1, n2, G_C), lambda b, g: (b, 0, 0, g)),
        out_shape=jax.ShapeDtypeStruct((bsz, n1, n2, D_C), F32),
        compiler_params=_params(("parallel", "parallel")),
        name="fnet_long",
    )(ca, sa, u.reshape(bsz, n1, n2, D_C), tw, m, ab)
    return out.transpose(0, 2, 1, 3).reshape(bsz, seq, D_C)


def _fnet_short_kernel(u_ref, f_ref, ab_ref, o_ref, *, seq):
    x = _dot(f_ref[...], u_ref[...].astype(BF16))
    xr, xi = x[:seq], x[seq:]
    for g in range(N_GROUPS_C):
        sl = slice(g * G_C, (g + 1) * G_C)
        xin = jnp.concatenate([xr[:, sl], xi[:, sl]], axis=1).astype(BF16)
        o_ref[:, sl] = _dot(xin, ab_ref[g])


def _fnet_short(u, ab):
    bsz, seq, _ = u.shape
    c, s = _dft_cos_sin(seq)
    f = _bf16_split(np.concatenate([c, -s], axis=0))[0]
    return pl.pallas_call(
        functools.partial(_fnet_short_kernel, seq=seq),
        grid=(bsz,),
        in_specs=[pl.BlockSpec((None, seq, D_C), lambda b: (b, 0, 0)),
                  pl.BlockSpec((2 * seq, seq), lambda b: (0, 0)),
                  pl.BlockSpec((N_GROUPS_C, 2 * G_C, G_C), lambda b: (0, 0, 0))],
        out_specs=pl.BlockSpec((None, seq, D_C), lambda b: (b, 0, 0)),
        out_shape=jax.ShapeDtypeStruct((bsz, seq, D_C), F32),
        compiler_params=_params(("parallel",)),
        name="fnet_short",
    )(u, f, ab)


def _final_kernel(x1_ref, y_ref, g_ref, gt_ref, wo_ref, fg_ref, o_ref):
    z = (y_ref[...] * _silu(g_ref[...])).astype(BF16)
    x2 = x1_ref[...] + gt_ref[...] * _dot(z, wo_ref[...])
    o_ref[...] = _rms(x2) * fg_ref[...]


def _final(x1, y, g, gt, wo, fg, tiles_per_mod, tm):
    t = x1.shape[0]
    row = lambda i: (i, 0)
    fixed = lambda i: (0, 0)
    mod = lambda i: (i // tiles_per_mod, 0, 0)
    return pl.pallas_call(
        _final_kernel,
        grid=(t // tm,),
        in_specs=[pl.BlockSpec((tm, D_MODEL), row),
                  pl.BlockSpec((tm, D_C), row),
                  pl.BlockSpec((tm, D_C), row),
                  pl.BlockSpec((None, 1, D_MODEL), mod),
                  pl.BlockSpec((D_C, D_MODEL), fixed),
                  pl.BlockSpec((1, D_MODEL), fixed)],
        out_specs=pl.BlockSpec((tm, D_MODEL), row),
        out_shape=jax.ShapeDtypeStruct((t, D_MODEL), F32),
        compiler_params=_params(("parallel",)),
        name="outproj_odd_final",
    )(x1, y, g, gt, wo, fg)


def _rope_swap(w):
    nf = ROPE_DIM // 4
    w4 = w.reshape(w.shape[:-1] + (2, 2, nf))
    return jnp.stack([-w4[..., 1, :], w4[..., 0, :]], axis=-2).reshape(w.shape)


def _pack_w_in_even(w):
    ua_ga_cq_ckv = w[:, :4 * D_A + Q_RANK + KV_RANK]
    o = 4 * D_A + Q_RANK + KV_RANK
    krope = w[:, o:o + ROPE_DIM]
    gb = w[:, o + ROPE_DIM:]
    zpad = jnp.zeros((w.shape[0], LANES - ROPE_DIM), w.dtype)
    return jnp.concatenate([ua_ga_cq_ckv, gb, krope, zpad, _rope_swap(krope), zpad],
                           axis=1).astype(BF16)


def _pack_w_uq(w):
    w3 = w.reshape(Q_RANK, N_HEADS, QK_DIM)
    zq = jnp.zeros((Q_RANK, N_HEADS, HEAD_PAD - QK_DIM), w.dtype)
    wa = jnp.concatenate([w3, zq], axis=-1)
    zn = jnp.zeros((Q_RANK, N_HEADS, NOPE_DIM), w.dtype)
    wb = jnp.concatenate([zn, _rope_swap(w3[..., NOPE_DIM:]), zq], axis=-1)
    shape = (Q_RANK, N_HEADS * HEAD_PAD)
    return wa.reshape(shape).astype(BF16), wb.reshape(shape).astype(BF16)


def _pack_w_ukv(w):
    w3 = w.reshape(KV_RANK, N_HEADS, NOPE_DIM + V_DIM)
    zk = jnp.zeros((KV_RANK, N_HEADS, HEAD_PAD - NOPE_DIM), w.dtype)
    wk_c = jnp.concatenate([w3[..., :NOPE_DIM], zk], axis=-1).reshape(KV_RANK, -1)
    eye = jnp.eye(LANES, ROPE_DIM, dtype=w.dtype)
    rope_rows = jnp.concatenate([jnp.zeros((LANES, NOPE_DIM), w.dtype), eye,
                                 jnp.zeros((LANES, HEAD_PAD - QK_DIM), w.dtype)], axis=-1)
    wk_r = jnp.tile(rope_rows, (1, N_HEADS))
    wk = jnp.concatenate([wk_c, wk_r], axis=0)
    wv = w3[..., NOPE_DIM:].reshape(KV_RANK, D_B)
    return wk.astype(BF16), wv.astype(BF16)


def _rope_tables(seq, rotate):
    pos = np.arange(seq)
    half = ROPE_DIM // 2
    inv = ROPE_BASE ** (-np.arange(0, half, 2, dtype=np.float64) / half)
    r = (pos // GRID_W)[:, None] * inv
    c = (pos % GRID_W)[:, None] * inv
    ang = np.concatenate([r, r, c, c], axis=-1) * (1.0 if rotate else 0.0)
    cos_q = np.zeros((seq, LANES))
    sin_q = np.zeros((seq, LANES))
    cos_k = np.zeros((seq, LANES))
    sin_k = np.zeros((seq, LANES))
    cos_q[:, :NOPE_DIM] = 1.0
    cos_q[:, NOPE_DIM:QK_DIM] = np.cos(ang)
    sin_q[:, NOPE_DIM:QK_DIM] = np.sin(ang)
    cos_k[:, :ROPE_DIM] = np.cos(ang)
    sin_k[:, :ROPE_DIM] = np.sin(ang)
    return tuple(jnp.asarray(t.astype(np.float32)) for t in (cos_q, sin_q, cos_k, sin_k))


def kernel(x_prompt, x_sample, c, cache_ckv, cache_krope, c_ctx, norm_g, w_mod, b_mod,
           w_in_e, conv_w, conv_b, filt_w1, filt_b1, filt_w2, filt_b2, filt_w3, hyena_skip,
           q_norm_g, kv_norm_g, w_uq, w_ukv, w_out_e, w_in_o, fnet_w, w_out_o, final_g):
    nb_p, seq_p, _ = x_prompt.shape
    nb_s, seq_s, _ = x_sample.shape
    tp = nb_p * seq_p
    ts = nb_s * seq_s
    tm = 512

    cond = jnp.concatenate([c_ctx[None, :], c, jnp.zeros((8 - 1 - nb_s, D_MODEL), F32)], axis=0)
    mod = _ada_params(cond, w_mod, b_mod)
    shift = mod[:, :, None, :D_MODEL]
    scale = (1.0 + mod[:, :, None, D_MODEL:2 * D_MODEL]) * norm_g[:, None, None, :]
    gate = mod[:, :, None, 2 * D_MODEL:]

    xp = x_prompt.reshape(tp, D_MODEL)
    xs = x_sample.reshape(ts, D_MODEL)

    w_in = _pack_w_in_even(w_in_e[0])
    qg = q_norm_g[0].reshape(1, Q_RANK)
    kvg = kv_norm_g[0].reshape(1, KV_RANK)
    pr_p = _inproj_even(xp, scale[0, :1], shift[0, :1], w_in, qg, kvg, tp // tm, tm)
    pr_s = _inproj_even(xs, scale[0, 1:1 + nb_s], shift[0, 1:1 + nb_s], w_in, qg, kvg,
                        seq_s // tm, tm)
    ua_p, ga_p, cq_p, ckv_p, gb_p, kra_p, krb_p = pr_p
    ua_s, ga_s, cq_s, ckv_s, gb_s, kra_s, krb_s = pr_s

    filt = (filt_w1[0], filt_b1[0], filt_w2[0], filt_b2[0], filt_w3[0])
    ya = []
    for ua, ga, nb, seq in ((ua_p, ga_p, nb_p, seq_p), (ua_s, ga_s, nb_s, seq_s)):
        h, inv_norm = _hyena_filter(seq, *filt)
        kf = _spectrum(h, inv_norm, seq)
        ya.append(_hyena(ua.reshape(nb, seq, 3 * D_A), ga.reshape(nb, seq, D_A),
                         conv_w[0], conv_b[0], hyena_skip[0], kf))
    ya_p = ya[0].reshape(tp, D_A)
    ya_s = ya[1].reshape(ts, D_A)

    wq_a, wq_b = _pack_w_uq(w_uq[0])
    wk, wv = _pack_w_ukv(w_ukv[0])
    cos_q, sin_q, cos_k, sin_k = _rope_tables(seq_p, rotate=False)
    k_p, v_p = _kvup(ckv_p, kra_p, krb_p, cos_k, sin_k, wk, wv, seq_p)
    ob_p = _attention(cq_p.reshape(nb_p, seq_p, Q_RANK), cos_q, sin_q, wq_a, wq_b,
                      k_p.reshape(nb_p, seq_p, -1), v_p.reshape(nb_p, seq_p, -1),
                      gb_p.reshape(nb_p, seq_p, D_B), seq_p)

    cos_q, sin_q, cos_k, sin_k = _rope_tables(seq_s, rotate=True)
    k_s, v_s = _kvup_cached(cache_ckv, cache_krope, 0, ckv_s.reshape(nb_s, seq_s, KV_RANK),
                            kra_s.reshape(nb_s, seq_s, LANES), krb_s.reshape(nb_s, seq_s, LANES),
                            cos_k, sin_k, wk, wv, tm)
    ob_s = _attention(cq_s.reshape(nb_s, seq_s, Q_RANK), cos_q, sin_q, wq_a, wq_b, k_s, v_s,
                      gb_s.reshape(nb_s, seq_s, D_B), 256)

    wo_e = w_out_e[0].astype(BF16)
    wi_o = w_in_o[0].astype(BF16)
    x1_p, u_p, g_p = _mid(xp, ya_p, ob_p.reshape(tp, D_B), gate[0, :1], wo_e,
                          scale[1, :1], shift[1, :1], wi_o, tp // tm, tm)
    x1_s, u_s, g_s = _mid(xs, ya_s, ob_s.reshape(ts, D_B), gate[0, 1:1 + nb_s], wo_e,
                          scale[1, 1:1 + nb_s], shift[1, 1:1 + nb_s], wi_o, seq_s // tm, tm)

    y_p = _fnet_short(u_p.reshape(nb_p, seq_p, D_C), _fnet_weights(fnet_w[0], seq_p))
    y_s = _fnet_long(u_s.reshape(nb_s, seq_s, D_C), _fnet_weights(fnet_w[0], seq_s), 16,
                     seq_s // 16)
    wo_o = w_out_o[0].astype(BF16)
    fg = final_g.reshape(1, D_MODEL)
    out_p = _final(x1_p, y_p.reshape(tp, D_C), g_p, gate[1, :1], wo_o, fg, tp // tm, tm)
    out_s = _final(x1_s, y_s.reshape(ts, D_C), g_s, gate[1, 1:1 + nb_s], wo_o, fg,
                   seq_s // tm, tm)

    state_ckv = ckv_p.reshape(nb_p, 1, seq_p, KV_RANK)
    state_krope = kra_p[:, :ROPE_DIM].reshape(nb_p, 1, seq_p, ROPE_DIM)
    return (out_p.reshape(nb_p, seq_p, D_MODEL), out_s.reshape(nb_s, seq_s, D_MODEL),
            state_ckv, state_krope)
```

```python
import functools
import math

import numpy as np
import jax
import jax.numpy as jnp
from jax import lax
from jax.experimental import pallas as pl
from jax.experimental.pallas import tpu as pltpu

F32 = jnp.float32
BF16 = jnp.bfloat16

D_MODEL = 1024
GRID_W = 64
EPS = 1e-6
D_A = 512
EMB_DIM = 33
DECAY_TARGET = 1e-2
MIN_DECAY = math.log(DECAY_TARGET) / 1.5
MAX_DECAY = math.log(DECAY_TARGET) / 0.3
N_HEADS = 8
Q_RANK = 256
KV_RANK = 256
NOPE_DIM = 64
ROPE_DIM = 32
V_DIM = 64
ROPE_BASE = 10000.0
D_B = N_HEADS * V_DIM
QK_DIM = NOPE_DIM + ROPE_DIM
ATTN_SCALE = 1.0 / math.sqrt(QK_DIM)
LOG2_E = math.log2(math.e)
D_C = 1024
N_GROUPS_C = 8
G_C = D_C // N_GROUPS_C

LANES = 128
HEAD_PAD = 128
VMEM_LIMIT = 56 * 1024 * 1024

_C_UA, _C_GA, _C_CQ, _C_CKV, _C_GB, _C_KRA, _C_KRB, _C_END = (
    0, 1536, 2048, 2304, 2560, 3072, 3200, 3328)


def _params(sem):
    return pltpu.CompilerParams(dimension_semantics=sem, vmem_limit_bytes=VMEM_LIMIT)


def _rms(x):
    return x * lax.rsqrt(jnp.mean(x * x, axis=-1, keepdims=True) + EPS)


def _silu(g):
    return g * jax.nn.sigmoid(g)


def _dot(a, b):
    return jnp.dot(a, b, preferred_element_type=F32)


def _ada_kernel(cond_ref, w_ref, b_ref, o_ref):
    c = _silu(cond_ref[...])
    o_ref[...] = jnp.dot(c, w_ref[...], preferred_element_type=F32,
                         precision=lax.Precision.HIGHEST) + b_ref[...]


def _ada_params(cond, w_mod, b_mod):
    depth = w_mod.shape[0]
    n = cond.shape[0]
    tn = 1024
    return pl.pallas_call(
        _ada_kernel,
        grid=(depth, 3 * D_MODEL // tn),
        in_specs=[pl.BlockSpec((n, D_MODEL), lambda i, j: (0, 0)),
                  pl.BlockSpec((None, D_MODEL, tn), lambda i, j: (i, 0, j)),
                  pl.BlockSpec((None, 1, tn), lambda i, j: (i, 0, j))],
        out_specs=pl.BlockSpec((None, n, tn), lambda i, j: (i, 0, j)),
        out_shape=jax.ShapeDtypeStruct((depth, n, 3 * D_MODEL), F32),
        compiler_params=_params(("arbitrary", "arbitrary")),
        name="ada_params",
    )(cond, w_mod, b_mod.reshape(depth, 1, 3 * D_MODEL))


def _inproj_even_kernel(x_ref, sc_ref, sh_ref, w_ref, qg_ref, kvg_ref,
                        ua_ref, ga_ref, cq_ref, ckv_ref, gb_ref, kra_ref, krb_ref):
    h = (_rms(x_ref[...]) * sc_ref[...] + sh_ref[...]).astype(BF16)

    def proj(a, b):
        return _dot(h, w_ref[:, a:b])

    ua_ref[...] = proj(_C_UA, _C_GA)
    ga_ref[...] = proj(_C_GA, _C_CQ)
    cq_ref[...] = (_rms(proj(_C_CQ, _C_CKV)) * qg_ref[...]).astype(BF16)
    ckv_ref[...] = _rms(proj(_C_CKV, _C_GB)) * kvg_ref[...]
    gb_ref[...] = proj(_C_GB, _C_KRA)
    kra_ref[...] = proj(_C_KRA, _C_KRB)
    krb_ref[...] = proj(_C_KRB, _C_END)


def _inproj_even(x2d, sc, sh, w, qg, kvg, tiles_per_mod, tm):
    t = x2d.shape[0]
    row = lambda i: (i, 0)
    fixed = lambda i: (0, 0)
    mod = lambda i: (i // tiles_per_mod, 0, 0)
    widths = (3 * D_A, D_A, Q_RANK, KV_RANK, D_B, LANES, LANES)
    dtypes = (F32, F32, BF16, F32, F32, F32, F32)
    return pl.pallas_call(
        _inproj_even_kernel,
        grid=(t // tm,),
        in_specs=[pl.BlockSpec((tm, D_MODEL), row),
                  pl.BlockSpec((None, 1, D_MODEL), mod),
                  pl.BlockSpec((None, 1, D_MODEL), mod),
                  pl.BlockSpec((D_MODEL, _C_END), fixed),
                  pl.BlockSpec((1, Q_RANK), fixed),
                  pl.BlockSpec((1, KV_RANK), fixed)],
        out_specs=[pl.BlockSpec((tm, wd), row) for wd in widths],
        out_shape=[jax.ShapeDtypeStruct((t, wd), dt) for wd, dt in zip(widths, dtypes)],
        compiler_params=_params(("parallel",)),
        name="inproj_even",
    )(x2d, sc, sh, w, qg, kvg)


def _acc(acc, coef, x):
    if x is None or abs(coef) < 1e-12:
        return acc
    if abs(abs(coef) - 1.0) < 1e-12:
        if acc is None:
            return x if coef > 0 else -x
        return acc + x if coef > 0 else acc - x
    return coef * x if acc is None else acc + coef * x


def _add(a, b):
    if a is None:
        return b
    return a if b is None else a + b


def _sub(a, b):
    if b is None:
        return a
    return -b if a is None else a - b


def _cmul(ar, ai, br, bi):
    def mul(x, y):
        return None if x is None or y is None else x * y
    return _sub(mul(ar, br), mul(ai, bi)), _add(mul(ar, bi), mul(ai, br))


def _radix_parts(h):
    singles = [0] + ([h // 2] if h >= 2 and h % 2 == 0 else [])
    pairs = [(n, h - n) for n in range(1, (h + 1) // 2)]
    return singles, pairs


def _radix_fwd(k1, h, single, pair_sum, pair_diff):
    singles, pairs = _radix_parts(h)
    ar = ai = None
    for n in singles:
        th = 2.0 * math.pi * n * k1 / (2 * h)
        x = single(n)
        ar = _acc(ar, math.cos(th), x)
        ai = _acc(ai, -math.sin(th), x)
    for i, (n, _) in enumerate(pairs):
        th = 2.0 * math.pi * n * k1 / (2 * h)
        re_src, im_src = (pair_sum, pair_diff) if k1 % 2 == 0 else (pair_diff, pair_sum)
        ar = _acc(ar, math.cos(th), re_src(i))
        ai = _acc(ai, -math.sin(th), im_src(i))
    return ar, ai


def _radix_inv(h, n_total, bpr, bpi, emit):
    singles, pairs = _radix_parts(h)

    def coefs(n, k1):
        th = 2.0 * math.pi * n * k1 / (2 * h)
        w = (1.0 if k1 in (0, h) else 2.0) / n_total
        return w * math.cos(th), w * math.sin(th)

    for n in singles:
        y = None
        for k1 in range(h + 1):
            c, s = coefs(n, k1)
            y = _acc(y, c, bpr(k1) if abs(c) > 1e-12 else None)
            y = _acc(y, -s, bpi(k1) if abs(s) > 1e-12 else None)
        emit(n, y)
    for n, m in pairs:
        acc = {(0, "c"): None, (0, "s"): None, (1, "c"): None, (1, "s"): None}
        for k1 in range(h + 1):
            c, s = coefs(n, k1)
            par = k1 % 2
            acc[(par, "c")] = _acc(acc[(par, "c")], c, bpr(k1) if abs(c) > 1e-12 else None)
            acc[(par, "s")] = _acc(acc[(par, "s")], s, bpi(k1) if abs(s) > 1e-12 else None)
        even_m, even_p = _sub(acc[(0, "c")], acc[(0, "s")]), _add(acc[(0, "c")], acc[(0, "s")])
        odd_m, odd_p = _sub(acc[(1, "c")], acc[(1, "s")]), _add(acc[(1, "c")], acc[(1, "s")])
        emit(n, _add(even_m, odd_m))
        emit(m, _sub(even_p, odd_p))


def _next_twiddle(tw, tw1_ref):
    if tw is None:
        return tw1_ref[0], tw1_ref[1]
    return _cmul(tw[0], tw[1], tw1_ref[0], tw1_ref[1])


def _spectrum_kernel(hf_ref, hb_ref, inv_ref, tw1_ref, fh_ref, fl_ref, kf_ref, sd_scr,
                     *, h, n2, nk_pad):
    ct = hf_ref.shape[-1]
    row = lax.broadcasted_iota(jnp.int32, (n2, ct), 0)
    _, pairs = _radix_parts(h)

    def hb(n):
        return jnp.where(row == 0, 0.0, hb_ref[0]) if n == 0 else hb_ref[n]

    for i, (n, m) in enumerate(pairs):
        sd_scr[0, 0, i] = hf_ref[n] + hf_ref[m]
        sd_scr[0, 1, i] = hf_ref[n] - hf_ref[m]
        sd_scr[1, 0, i] = hb(n) + hb(m)
        sd_scr[1, 1, i] = hb(n) - hb(m)

    inv = inv_ref[...]
    tw = None
    for k1 in range(h + 1):
        far, fai = _radix_fwd(k1, h, lambda n: hf_ref[n],
                              lambda i: sd_scr[0, 0, i], lambda i: sd_scr[0, 1, i])
        bar, bai = _radix_fwd(k1, h, hb, lambda i: sd_scr[1, 0, i], lambda i: sd_scr[1, 1, i])
        if k1 > 0:
            tw = _next_twiddle(tw, tw1_ref)
            far, fai = _cmul(far, fai, tw[0], tw[1])
            bar, bai = _cmul(bar, bai, tw[0], tw[1])
        zero = jnp.zeros((n2, ct), F32)
        af = jnp.concatenate([zero if far is None else far, zero if fai is None else fai], axis=0)
        ab = jnp.concatenate([zero if bar is None else bar, zero if bai is None else bai], axis=0)
        a = jnp.concatenate([af, ab], axis=1)
        a_hi = a.astype(BF16)
        a_lo = (a - a_hi.astype(F32)).astype(BF16)
        x = _dot(fh_ref[...], a_hi) + (_dot(fh_ref[...], a_lo) + _dot(fl_ref[...], a_hi))
        lanes = slice(k1 * ct, (k1 + 1) * ct)
        kf_ref[0, :, lanes] = (x[:n2, :ct] + x[:n2, ct:]) * inv
        kf_ref[1, :, lanes] = (x[n2:, :ct] - x[n2:, ct:]) * inv
    if nk_pad > h + 1:
        pad = slice((h + 1) * ct, nk_pad * ct)
        kf_ref[:, :, pad] = jnp.zeros((2, n2, (nk_pad - h - 1) * ct), F32)


def _hyena_kernel(x0_ref, x1_ref, v_ref, ga_ref,
                  w0_ref, w1_ref, wv_ref, b0_ref, b1_ref, bv_ref, skip_ref,
                  kf_ref, tw1_ref, f_ref,
                  o_ref, v_scr, sd_scr, a_scr, b_scr, *, seq, nslab, n2, nk_pad):
    ct = o_ref.shape[-1]
    h = nslab
    chunk = 2 * LANES
    kpc = chunk // ct
    row = lax.broadcasted_iota(jnp.int32, (n2, ct), 0)

    def short_conv(ref, w_ref, b_ref, s, start):
        u = ref[pl.ds(start, n2), :]
        prev = ref[pl.ds(jnp.maximum(start - 1, 0), 1), :]
        prev = jnp.where(s == 0, 0.0, prev)
        nxt = ref[pl.ds(jnp.minimum(start + n2, seq - 1), 1), :]
        nxt = jnp.where(s == nslab - 1, 0.0, nxt)
        up = jnp.where(row == 0, prev, pltpu.roll(u, 1, 0))
        un = jnp.where(row == n2 - 1, nxt, pltpu.roll(u, n2 - 1, 0))
        return b_ref[...] + up * w_ref[0:1, :] + u * w_ref[1:2, :] + un * w_ref[2:3, :]

    def prep(s, carry):
        start = pl.multiple_of(s * n2, n2)
        x0 = short_conv(x0_ref, w0_ref, b0_ref, s, start)
        x1 = short_conv(x1_ref, w1_ref, b1_ref, s, start)
        v_scr[s] = short_conv(v_ref, wv_ref, bv_ref, s, start) * x1
        o_ref[pl.ds(start, n2), :] = x0 * _silu(ga_ref[pl.ds(start, n2), :])
        return carry

    lax.fori_loop(0, nslab, prep, 0)

    _, pairs = _radix_parts(h)
    for i, (n, m) in enumerate(pairs):
        sd_scr[0, i] = v_scr[n] + v_scr[m]
        sd_scr[1, i] = v_scr[n] - v_scr[m]
    zero = jnp.zeros((n2, ct), BF16)
    tw = None
    for k1 in range(h + 1):
        ar, ai = _radix_fwd(k1, h, lambda n: v_scr[n], lambda i: sd_scr[0, i],
                            lambda i: sd_scr[1, i])
        if k1 > 0:
            tw = _next_twiddle(tw, tw1_ref)
            ar, ai = _cmul(ar, ai, tw[0], tw[1])
        lanes = slice(k1 * ct, (k1 + 1) * ct)
        a_scr[0:n2, lanes] = zero if ar is None else ar.astype(BF16)
        a_scr[n2:2 * n2, lanes] = zero if ai is None else ai.astype(BF16)
    for k1 in range(h + 1, nk_pad):
        lanes = slice(k1 * ct, (k1 + 1) * ct)
        a_scr[0:n2, lanes] = zero
        a_scr[n2:2 * n2, lanes] = zero

    tw = None
    for j in range(nk_pad // kpc):
        cols = slice(j * chunk, (j + 1) * chunk)
        x = _dot(f_ref[...], a_scr[:, cols])
        xr, xi = x[:n2], x[n2:]
        kr = kf_ref[0, :, cols]
        ki = kf_ref[1, :, cols]
        z = jnp.concatenate([xr * kr - xi * ki, -(xr * ki + xi * kr)], axis=0)
        bt = _dot(f_ref[...], z.astype(BF16))
        for kk in range(kpc):
            k1 = j * kpc + kk
            if k1 > h:
                continue
            lanes = slice(kk * ct, (kk + 1) * ct)
            br = bt[:n2, lanes]
            bi = -bt[n2:, lanes]
            if k1 > 0:
                tw = _next_twiddle(tw, tw1_ref)
                br, bi = _cmul(br, bi, tw[0], -tw[1])
            b_scr[0, k1] = br
            if k1 not in (0, h):
                b_scr[1, k1] = bi

    def emit(n, y):
        rows = pl.ds(n * n2, n2)
        o_ref[rows, :] = (y + v_scr[n] * skip_ref[...]) * o_ref[rows, :]

    _radix_inv(h, 2 * seq, lambda k1: b_scr[0, k1], lambda k1: b_scr[1, k1], emit)


def _dft_cos_sin(n2, shift=0):
    idx = np.arange(n2)
    ang = 2.0 * np.pi * (((idx[:, None] + shift) * idx[None, :]) % n2) / n2
    return np.cos(ang), np.sin(ang)


def _bf16_split(x):
    x32 = jnp.asarray(x.astype(np.float32))
    hi = x32.astype(BF16)
    lo = (x32 - hi.astype(F32)).astype(BF16)
    return hi, lo


def _dft_block(n2):
    c, s = _dft_cos_sin(n2)
    return np.block([[c, s], [-s, c]])


def _hyena_plan(seq):
    n2 = min(seq, 512)
    n1 = 2 * seq // n2
    ct = LANES if seq > 512 else 2 * LANES
    kpc = 2 * LANES // ct
    nk_pad = -(-(n1 // 2 + 1) // kpc) * kpc
    return n1, n2, ct, nk_pad


def _twiddle1(n1, n2, ct):
    ang = 2.0 * np.pi * np.arange(n2) / (n1 * n2)
    tw = np.stack([np.cos(ang), -np.sin(ang)]).astype(np.float32)
    return jnp.broadcast_to(jnp.asarray(tw)[..., None], tw.shape + (ct,))


def _spectrum(h, inv_norm, seq):
    n1, n2, ct, nk_pad = _hyena_plan(seq)
    nslab = n1 // 2
    nct = D_A // ct
    npair = len(_radix_parts(nslab)[1])
    ffwd_hi, ffwd_lo = _bf16_split(_dft_block(n2))
    h3 = h.reshape(nslab, n2, 2 * D_A)
    fixed2 = lambda c: (0, 0)
    return pl.pallas_call(
        functools.partial(_spectrum_kernel, h=nslab, n2=n2, nk_pad=nk_pad),
        grid=(nct,),
        in_specs=[pl.BlockSpec((nslab, n2, ct), lambda c: (0, 0, c)),
                  pl.BlockSpec((nslab, n2, ct), lambda c: (0, 0, nct + c)),
                  pl.BlockSpec((1, ct), lambda c: (0, c)),
                  pl.BlockSpec((2, n2, ct), lambda c: (0, 0, 0)),
                  pl.BlockSpec((2 * n2, 2 * n2), fixed2),
                  pl.BlockSpec((2 * n2, 2 * n2), fixed2)],
        out_specs=pl.BlockSpec((None, 2, n2, nk_pad * ct), lambda c: (c, 0, 0, 0)),
        out_shape=jax.ShapeDtypeStruct((nct, 2, n2, nk_pad * ct), F32),
        scratch_shapes=[pltpu.VMEM((2, 2, max(npair, 1), n2, ct), F32)],
        compiler_params=_params(("parallel",)),
        name="filter_spectrum",
    )(h3, h3, inv_norm, _twiddle1(n1, n2, ct), ffwd_hi, ffwd_lo)


def _hyena(ua, ga, conv_w, conv_b, skip, kf):
    bsz, seq, _ = ua.shape
    n1, n2, ct, nk_pad = _hyena_plan(seq)
    nslab = n1 // 2
    nct = D_A // ct
    npair = len(_radix_parts(nslab)[1])
    cb = conv_b.reshape(1, 3 * D_A)
    sk = skip.reshape(1, D_A)
    part = lambda p: pl.BlockSpec((None, seq, ct), lambda c, b: (b, 0, p * nct + c))
    wpart = lambda p: pl.BlockSpec((3, ct), lambda c, b: (0, p * nct + c))
    bpart = lambda p: pl.BlockSpec((1, ct), lambda c, b: (0, p * nct + c))
    return pl.pallas_call(
        functools.partial(_hyena_kernel, seq=seq, nslab=nslab, n2=n2, nk_pad=nk_pad),
        grid=(nct, bsz),
        in_specs=[part(0), part(1), part(2),
                  pl.BlockSpec((None, seq, ct), lambda c, b: (b, 0, c)),
                  wpart(0), wpart(1), wpart(2), bpart(0), bpart(1), bpart(2),
                  pl.BlockSpec((1, ct), lambda c, b: (0, c)),
                  pl.BlockSpec((None, 2, n2, nk_pad * ct), lambda c, b: (c, 0, 0, 0)),
                  pl.BlockSpec((2, n2, ct), lambda c, b: (0, 0, 0)),
                  pl.BlockSpec((2 * n2, 2 * n2), lambda c, b: (0, 0))],
        out_specs=pl.BlockSpec((None, seq, ct), lambda c, b: (b, 0, c)),
        out_shape=jax.ShapeDtypeStruct((bsz, seq, D_A), F32),
        scratch_shapes=[pltpu.VMEM((nslab, n2, ct), F32),
                        pltpu.VMEM((2, max(npair, 1), n2, ct), F32),
                        pltpu.VMEM((2 * n2, nk_pad * ct), BF16),
                        pltpu.VMEM((2, nk_pad, n2, ct), F32)],
        compiler_params=_params(("parallel", "parallel")),
        name="hyena",
    )(ua, ua, ua, ga, conv_w, conv_w, conv_w, cb, cb, cb, sk, kf, _twiddle1(n1, n2, ct),
      _bf16_split(_dft_block(n2))[0])


def _hyena_filter_kernel(fv_ref, w1_ref, b1_ref, w2_ref, b2_ref, w3_ref, del_ref,
                         o_ref, nrm_ref, *, seq):
    hp = lax.Precision.HIGHEST
    tm = o_ref.shape[0]
    i = pl.program_id(0)
    bands = (EMB_DIM - 1) // 2
    pos = (i * tm + lax.broadcasted_iota(jnp.int32, (tm, LANES), 0)).astype(F32)
    lane = lax.broadcasted_iota(jnp.int32, (tm, LANES), 1)
    t = pos * (1.0 / (seq - 1))
    ang = (pos * (2.0 * math.pi / seq)) * fv_ref[...]
    z = jnp.where(lane == 0, t,
                  jnp.where(lane <= bands, jnp.cos(ang),
                            jnp.where(lane < EMB_DIM, -jnp.sin(ang), 0.0)))
    h = jnp.sin(jnp.dot(z, w1_ref[...], precision=hp, preferred_element_type=F32) + b1_ref[...])
    h = jnp.sin(jnp.dot(h, w2_ref[...], precision=hp, preferred_element_type=F32) + b2_ref[...])
    tcol = (i * tm + lax.broadcasted_iota(jnp.int32, (tm, 1), 0)).astype(F32) * (1.0 / (seq - 1))
    h = jnp.dot(h, w3_ref[...], precision=hp, preferred_element_type=F32) * jnp.exp(
        -tcol * del_ref[...])
    o_ref[...] = h

    @pl.when(i == 0)
    def _():
        nrm_ref[...] = jnp.zeros_like(nrm_ref)

    nrm_ref[...] += jnp.sum(jnp.abs(h), axis=0, keepdims=True)


def _hyena_filter(seq, w1, b1, w2, b2, w3):
    bands = (EMB_DIM - 1) // 2
    f = np.linspace(1e-4, bands - 1, bands)
    fv = np.zeros((1, LANES), np.float32)
    fv[0, 1:1 + bands] = f
    fv[0, 1 + bands:EMB_DIM] = f
    deltas = np.abs(np.linspace(MIN_DECAY, MAX_DECAY, D_A)).astype(np.float32)
    deltas2 = np.concatenate([deltas, deltas])[None, :]
    w1p = jnp.pad(w1, ((0, LANES - EMB_DIM), (0, 0)))
    tm = min(seq, 512)
    fo = w1.shape[1]
    fixed = lambda i: (0, 0)
    h, nrm = pl.pallas_call(
        functools.partial(_hyena_filter_kernel, seq=seq),
        grid=(seq // tm,),
        in_specs=[pl.BlockSpec((1, LANES), fixed),
                  pl.BlockSpec((LANES, fo), fixed),
                  pl.BlockSpec((1, fo), fixed),
                  pl.BlockSpec((fo, fo), fixed),
                  pl.BlockSpec((1, fo), fixed),
                  pl.BlockSpec((fo, 2 * D_A), fixed),
                  pl.BlockSpec((1, 2 * D_A), fixed)],
        out_specs=[pl.BlockSpec((tm, 2 * D_A), lambda i: (i, 0)),
                   pl.BlockSpec((1, 2 * D_A), fixed)],
        out_shape=[jax.ShapeDtypeStruct((seq, 2 * D_A), F32),
                   jax.ShapeDtypeStruct((1, 2 * D_A), F32)],
        compiler_params=_params(("arbitrary",)),
        name="hyena_filter",
    )(jnp.asarray(fv), w1p, b1.reshape(1, fo), w2, b2.reshape(1, fo), w3, jnp.asarray(deltas2))
    return h, 1.0 / (nrm[:, :D_A] + nrm[:, D_A:])


def _kv_new_rows(ckv_ref, kra_ref, krb_ref, cos_ref, sin_ref, wk_ref, wv_ref, k_ref, v_ref):
    c = ckv_ref[...].astype(BF16)
    kr = (kra_ref[...] * cos_ref[...] + krb_ref[...] * sin_ref[...]).astype(BF16)
    kin = jnp.concatenate([c, kr], axis=-1)
    k_ref[...] = _dot(kin, wk_ref[...]).astype(BF16)
    v_ref[...] = _dot(c, wv_ref[...]).astype(BF16)


def _kvup_kernel(ckv_ref, kra_ref, krb_ref, cos_ref, sin_ref, wk_ref, wv_ref, k_ref, v_ref):
    _kv_new_rows(ckv_ref, kra_ref, krb_ref, cos_ref, sin_ref, wk_ref, wv_ref, k_ref, v_ref)


def _kvup(ckv, kra, krb, cos, sin, wk, wv, tm):
    t = ckv.shape[0]
    tiles_per_seq = cos.shape[0] // tm
    fixed = lambda i: (0, 0)
    row = lambda i: (i, 0)
    pos = lambda i: (i % tiles_per_seq, 0)
    return pl.pallas_call(
        _kvup_kernel,
        grid=(t // tm,),
        in_specs=[pl.BlockSpec((tm, KV_RANK), row),
                  pl.BlockSpec((tm, LANES), row),
                  pl.BlockSpec((tm, LANES), row),
                  pl.BlockSpec((tm, LANES), pos),
                  pl.BlockSpec((tm, LANES), pos),
                  pl.BlockSpec((KV_RANK + LANES, N_HEADS * HEAD_PAD), fixed),
                  pl.BlockSpec((KV_RANK, D_B), fixed)],
        out_specs=[pl.BlockSpec((tm, N_HEADS * HEAD_PAD), row),
                   pl.BlockSpec((tm, D_B), row)],
        out_shape=[jax.ShapeDtypeStruct((t, N_HEADS * HEAD_PAD), BF16),
                   jax.ShapeDtypeStruct((t, D_B), BF16)],
        compiler_params=_params(("parallel",)),
        name="kv_up",
    )(ckv, kra, krb, cos, sin, wk, wv)


def _kvup_cached_kernel(cckv_ref, ckr_ref, ckv_ref, kra_ref, krb_ref, cos_ref, sin_ref,
                        wk_ref, wv_ref, k_ref, v_ref, *, nctx):
    t = pl.program_id(1)

    @pl.when(t < nctx)
    def _():
        c = cckv_ref[...].astype(BF16)
        kr = ckr_ref[...].astype(BF16)
        k_ref[...] = (_dot(c, wk_ref[0:KV_RANK, :])
                      + _dot(kr, wk_ref[KV_RANK:KV_RANK + ROPE_DIM, :])).astype(BF16)
        v_ref[...] = _dot(c, wv_ref[...]).astype(BF16)

    @pl.when(t >= nctx)
    def _():
        _kv_new_rows(ckv_ref, kra_ref, krb_ref, cos_ref, sin_ref, wk_ref, wv_ref, k_ref, v_ref)


def _kvup_cached(cache_ckv, cache_krope, layer, ckv, kra, krb, cos, sin, wk, wv, tm):
    bsz, seq, _ = ckv.shape
    past = cache_ckv.shape[2]
    nctx = past // tm
    ntile = nctx + seq // tm
    fixed = lambda b, t: (0, 0)
    ctx = lambda b, t: (b, layer, jnp.minimum(t, nctx - 1), 0)
    new = lambda b, t: (b, jnp.maximum(t - nctx, 0), 0)
    pos = lambda b, t: (jnp.maximum(t - nctx, 0), 0)
    out = lambda b, t: (b, t, 0)
    return pl.pallas_call(
        functools.partial(_kvup_cached_kernel, nctx=nctx),
        grid=(bsz, ntile),
        in_specs=[pl.BlockSpec((None, None, tm, KV_RANK), ctx),
                  pl.BlockSpec((None, None, tm, ROPE_DIM), ctx),
                  pl.BlockSpec((None, tm, KV_RANK), new),
                  pl.BlockSpec((None, tm, LANES), new),
                  pl.BlockSpec((None, tm, LANES), new),
                  pl.BlockSpec((tm, LANES), pos),
                  pl.BlockSpec((tm, LANES), pos),
                  pl.BlockSpec((KV_RANK + LANES, N_HEADS * HEAD_PAD), fixed),
                  pl.BlockSpec((KV_RANK, D_B), fixed)],
        out_specs=[pl.BlockSpec((None, tm, N_HEADS * HEAD_PAD), out),
                   pl.BlockSpec((None, tm, D_B), out)],
        out_shape=[jax.ShapeDtypeStruct((bsz, past + seq, N_HEADS * HEAD_PAD), BF16),
                   jax.ShapeDtypeStruct((bsz, past + seq, D_B), BF16)],
        compiler_params=_params(("parallel", "arbitrary")),
        name="kv_up_cached",
    )(cache_ckv, cache_krope, ckv, kra, krb, cos, sin, wk, wv)


def _attn_kernel(cq_ref, cos_ref, sin_ref, wa_ref, wb_ref, k_ref, v_ref, gb_ref, o_ref):
    cq = cq_ref[...]
    cos = jnp.concatenate([cos_ref[...]] * N_HEADS, axis=-1)
    sin = jnp.concatenate([sin_ref[...]] * N_HEADS, axis=-1)
    q = (_dot(cq, wa_ref[...]) * cos + _dot(cq, wb_ref[...]) * sin) * (ATTN_SCALE * LOG2_E)
    q = q.astype(BF16)
    lane = lax.broadcasted_iota(jnp.int32, (q.shape[0], 2 * V_DIM), 1)
    for pair in range(N_HEADS // 2):
        vsl = slice(pair * 2 * V_DIM, (pair + 1) * 2 * V_DIM)
        outs = []
        for hh in range(2):
            hsl = slice((2 * pair + hh) * HEAD_PAD, (2 * pair + hh + 1) * HEAD_PAD)
            s = lax.dot_general(q[:, hsl], k_ref[:, hsl], (((1,), (1,)), ((), ())),
                                preferred_element_type=F32)
            p = jnp.exp2(s - jnp.max(s, axis=-1, keepdims=True))
            denom = jnp.sum(p, axis=-1, keepdims=True)
            outs.append(_dot(p.astype(BF16), v_ref[:, vsl]) / denom)
        o = jnp.where(lane < V_DIM, outs[0], outs[1])
        o_ref[:, vsl] = (o * _silu(gb_ref[:, vsl])).astype(BF16)


def _attention(cq, cos, sin, wa, wb, k, v, gb, tq):
    bsz, lq, _ = cq.shape
    lk = k.shape[1]
    fixed = lambda b, i: (0, 0)
    qrow = lambda b, i: (b, i, 0)
    kv = lambda b, i: (b, 0, 0)
    pos = lambda b, i: (i, 0)
    return pl.pallas_call(
        _attn_kernel,
        grid=(bsz, lq // tq),
        in_specs=[pl.BlockSpec((None, tq, Q_RANK), qrow),
                  pl.BlockSpec((tq, LANES), pos),
                  pl.BlockSpec((tq, LANES), pos),
                  pl.BlockSpec((Q_RANK, N_HEADS * HEAD_PAD), fixed),
                  pl.BlockSpec((Q_RANK, N_HEADS * HEAD_PAD), fixed),
                  pl.BlockSpec((None, lk, N_HEADS * HEAD_PAD), kv),
                  pl.BlockSpec((None, lk, D_B), kv),
                  pl.BlockSpec((None, tq, D_B), qrow)],
        out_specs=pl.BlockSpec((None, tq, D_B), qrow),
        out_shape=jax.ShapeDtypeStruct((bsz, lq, D_B), BF16),
        compiler_params=_params(("parallel", "arbitrary")),
        name="mla_attention",
    )(cq, cos, sin, wa, wb, k, v, gb)


def _mid_kernel(x_ref, ya_ref, ob_ref, gt_ref, wo_ref, sc_ref, sh_ref, wi_ref,
                x1_ref, u_ref, g_ref):
    y = _dot(ya_ref[...].astype(BF16), wo_ref[0:D_A, :]) + _dot(ob_ref[...], wo_ref[D_A:, :])
    x1 = x_ref[...] + gt_ref[...] * y
    x1_ref[...] = x1
    h = (_rms(x1) * sc_ref[...] + sh_ref[...]).astype(BF16)
    u_ref[...] = _dot(h, wi_ref[:, :D_C]).astype(BF16)
    g_ref[...] = _dot(h, wi_ref[:, D_C:]).astype(BF16)


def _mid(x2d, ya, ob, gt, wo, sc, sh, wi, tiles_per_mod, tm):
    t = x2d.shape[0]
    row = lambda i: (i, 0)
    fixed = lambda i: (0, 0)
    mod = lambda i: (i // tiles_per_mod, 0, 0)
    return pl.pallas_call(
        _mid_kernel,
        grid=(t // tm,),
        in_specs=[pl.BlockSpec((tm, D_MODEL), row),
                  pl.BlockSpec((tm, D_A), row),
                  pl.BlockSpec((tm, D_B), row),
                  pl.BlockSpec((None, 1, D_MODEL), mod),
                  pl.BlockSpec((D_A + D_B, D_MODEL), fixed),
                  pl.BlockSpec((None, 1, D_MODEL), mod),
                  pl.BlockSpec((None, 1, D_MODEL), mod),
                  pl.BlockSpec((D_MODEL, 2 * D_C), fixed)],
        out_specs=[pl.BlockSpec((tm, D_MODEL), row),
                   pl.BlockSpec((tm, D_C), row),
                   pl.BlockSpec((tm, D_C), row)],
        out_shape=[jax.ShapeDtypeStruct((t, D_MODEL), F32),
                   jax.ShapeDtypeStruct((t, D_C), BF16),
                   jax.ShapeDtypeStruct((t, D_C), BF16)],
        compiler_params=_params(("parallel",)),
        name="outproj_even_inproj_odd",
    )(x2d, ya, ob, gt, wo, sc, sh, wi)


def _fnet_weights_kernel(w_ref, cs_ref, o_ref):
    hp = lax.Precision.HIGHEST
    o_ref[...] = jnp.dot(cs_ref[...], w_ref[...], precision=hp,
                         preferred_element_type=F32).astype(BF16)


def _fnet_weights(fnet_w, seq):
    c, s = _dft_cos_sin(G_C)
    cs = jnp.asarray((np.concatenate([c, s], axis=0) / math.sqrt(seq * G_C)).astype(np.float32))
    return pl.pallas_call(
        _fnet_weights_kernel,
        grid=(N_GROUPS_C,),
        in_specs=[pl.BlockSpec((None, G_C, G_C), lambda g: (g, 0, 0)),
                  pl.BlockSpec((2 * G_C, G_C), lambda g: (0, 0))],
        out_specs=pl.BlockSpec((None, 2 * G_C, G_C), lambda g: (g, 0, 0)),
        out_shape=jax.ShapeDtypeStruct((N_GROUPS_C, 2 * G_C, G_C), BF16),
        compiler_params=_params(("parallel",)),
        name="fnet_weights",
    )(fnet_w, cs)


def _fnet_long_kernel(u_ref, tw1_ref, m_ref, ab_ref, o_ref, sd_scr, a_scr, *, n1, n2, nk_pad):
    h = n1 // 2
    ct = o_ref.shape[-1]
    chunk = 2 * LANES
    kpc = chunk // ct
    for i, n in enumerate(range(1, h)):
        a = u_ref[n].astype(F32)
        b = u_ref[n1 - n].astype(F32)
        sd_scr[0, i] = a + b
        sd_scr[1, i] = a - b
    x0 = u_ref[0].astype(F32)
    xh = u_ref[h].astype(F32)
    zero = jnp.zeros((n2, ct), BF16)
    tw = None
    for k1 in range(h + 1):
        ar = x0 + xh if k1 % 2 == 0 else x0 - xh
        ai = None
        for i, n in enumerate(range(1, h)):
            th = 2.0 * math.pi * n * k1 / n1
            ar = _acc(ar, math.cos(th), sd_scr[0, i])
            ai = _acc(ai, -math.sin(th), sd_scr[1, i])
        if k1 > 0:
            tw = _next_twiddle(tw, tw1_ref)
            ar, ai = _cmul(ar, ai, tw[0], tw[1])
        lanes = slice(k1 * ct, (k1 + 1) * ct)
        a_scr[0:n2, lanes] = ar.astype(BF16)
        a_scr[n2:2 * n2, lanes] = zero if ai is None else ai.astype(BF16)
    for k1 in range(h + 1, nk_pad):
        lanes = slice(k1 * ct, (k1 + 1) * ct)
        a_scr[0:n2, lanes] = zero
        a_scr[n2:2 * n2, lanes] = zero

    def group_map(xr, xi):
        y = _dot(jnp.concatenate([xr, xi], axis=1).astype(BF16), ab_ref[...])
        return y.astype(o_ref.dtype)

    for j in range(nk_pad // kpc):
        x = _dot(m_ref[...], a_scr[:, j * chunk:(j + 1) * chunk])
        for kk in range(kpc):
            k1 = j * kpc + kk
            if k1 > h:
                continue
            lanes = slice(kk * ct, (kk + 1) * ct)
            o_ref[k1] = group_map(x[:n2, lanes], x[n2:2 * n2, lanes])
            if 0 < k1 < h:
                o_ref[n1 - k1] = group_map(x[2 * n2:3 * n2, lanes], x[3 * n2:, lanes])


def _fnet_long(u, ab, n1, n2):
    bsz, seq, _ = u.shape
    h = n1 // 2
    kpc = 2 * LANES // G_C
    nk_pad = -(-(h + 1) // kpc) * kpc
    c, s = _dft_cos_sin(n2)
    ce, se = _dft_cos_sin(n2, shift=1)
    m = _bf16_split(np.block([[c, s], [-s, c], [ce, -se], [-se, -ce]]))[0]
    out = pl.pallas_call(
        functools.partial(_fnet_long_kernel, n1=n1, n2=n2, nk_pad=nk_pad),
        grid=(bsz, N_GROUPS_C),
        in_specs=[pl.BlockSpec((None, n1, n2, G_C), lambda b, g: (b, 0, 0, g)),
                  pl.BlockSpec((2, n2, G_C), lambda b, g: (0, 0, 0)),
                  pl.BlockSpec((4 * n2, 2 * n2), lambda b, g: (0, 0)),
                  pl.BlockSpec((None, 2 * G_C, G_C), lambda b, g: (g, 0, 0))],
        out_specs=pl.BlockSpec((None, n1, n2, G_C), lambda b, g: (b, 0, 0, g)),
        out_shape=jax.ShapeDtypeStruct((bsz, n1, n2, D_C), BF16),
        scratch_shapes=[pltpu.VMEM((2, h - 1, n2, G_C), F32),
                        pltpu.VMEM((2 * n2, nk_pad * G_C), BF16)],
        compiler_params=_params(("parallel", "parallel")),
        name="fnet_long",
    )(u.reshape(bsz, n1, n2, D_C), _twiddle1(n1, n2, G_C), m, ab)
    return out.transpose(0, 2, 1, 3).reshape(bsz, seq, D_C)


def _fnet_short_kernel(u_ref, f_ref, ab_ref, o_ref, *, seq):
    x = _dot(f_ref[...], u_ref[...])
    xr, xi = x[:seq], x[seq:]
    for g in range(N_GROUPS_C):
        sl = slice(g * G_C, (g + 1) * G_C)
        xin = jnp.concatenate([xr[:, sl], xi[:, sl]], axis=1).astype(BF16)
        o_ref[:, sl] = _dot(xin, ab_ref[g]).astype(o_ref.dtype)


def _fnet_short(u, ab):
    bsz, seq, _ = u.shape
    c, s = _dft_cos_sin(seq)
    f = _bf16_split(np.concatenate([c, -s], axis=0))[0]
    return pl.pallas_call(
        functools.partial(_fnet_short_kernel, seq=seq),
        grid=(bsz,),
        in_specs=[pl.BlockSpec((None, seq, D_C), lambda b: (b, 0, 0)),
                  pl.BlockSpec((2 * seq, seq), lambda b: (0, 0)),
                  pl.BlockSpec((N_GROUPS_C, 2 * G_C, G_C), lambda b: (0, 0, 0))],
        out_specs=pl.BlockSpec((None, seq, D_C), lambda b: (b, 0, 0)),
        out_shape=jax.ShapeDtypeStruct((bsz, seq, D_C), BF16),
        compiler_params=_params(("parallel",)),
        name="fnet_short",
    )(u, f, ab)


def _final_kernel(x1_ref, y_ref, g_ref, gt_ref, wo_ref, fg_ref, o_ref):
    z = (y_ref[...].astype(F32) * _silu(g_ref[...].astype(F32))).astype(BF16)
    x2 = x1_ref[...] + gt_ref[...] * _dot(z, wo_ref[...])
    o_ref[...] = _rms(x2) * fg_ref[...]


def _final(x1, y, g, gt, wo, fg, tiles_per_mod, tm):
    t = x1.shape[0]
    row = lambda i: (i, 0)
    fixed = lambda i: (0, 0)
    mod = lambda i: (i // tiles_per_mod, 0, 0)
    return pl.pallas_call(
        _final_kernel,
        grid=(t // tm,),
        in_specs=[pl.BlockSpec((tm, D_MODEL), row),
                  pl.BlockSpec((tm, D_C), row),
                  pl.BlockSpec((tm, D_C), row),
                  pl.BlockSpec((None, 1, D_MODEL), mod),
                  pl.BlockSpec((D_C, D_MODEL), fixed),
                  pl.BlockSpec((1, D_MODEL), fixed)],
        out_specs=pl.BlockSpec((tm, D_MODEL), row),
        out_shape=jax.ShapeDtypeStruct((t, D_MODEL), F32),
        compiler_params=_params(("parallel",)),
        name="outproj_odd_final",
    )(x1, y, g, gt, wo, fg)


def _rope_swap(w):
    nf = ROPE_DIM // 4
    w4 = w.reshape(w.shape[:-1] + (2, 2, nf))
    return jnp.stack([-w4[..., 1, :], w4[..., 0, :]], axis=-2).reshape(w.shape)


def _pack_w_in_even(w):
    ua_ga_cq_ckv = w[:, :4 * D_A + Q_RANK + KV_RANK]
    o = 4 * D_A + Q_RANK + KV_RANK
    krope = w[:, o:o + ROPE_DIM]
    gb = w[:, o + ROPE_DIM:]
    zpad = jnp.zeros((w.shape[0], LANES - ROPE_DIM), w.dtype)
    return jnp.concatenate([ua_ga_cq_ckv, gb, krope, zpad, _rope_swap(krope), zpad],
                           axis=1).astype(BF16)


def _pack_w_uq(w):
    w3 = w.reshape(Q_RANK, N_HEADS, QK_DIM)
    zq = jnp.zeros((Q_RANK, N_HEADS, HEAD_PAD - QK_DIM), w.dtype)
    wa = jnp.concatenate([w3, zq], axis=-1)
    zn = jnp.zeros((Q_RANK, N_HEADS, NOPE_DIM), w.dtype)
    wb = jnp.concatenate([zn, _rope_swap(w3[..., NOPE_DIM:]), zq], axis=-1)
    shape = (Q_RANK, N_HEADS * HEAD_PAD)
    return wa.reshape(shape).astype(BF16), wb.reshape(shape).astype(BF16)


def _pack_w_ukv(w):
    w3 = w.reshape(KV_RANK, N_HEADS, NOPE_DIM + V_DIM)
    zk = jnp.zeros((KV_RANK, N_HEADS, HEAD_PAD - NOPE_DIM), w.dtype)
    wk_c = jnp.concatenate([w3[..., :NOPE_DIM], zk], axis=-1).reshape(KV_RANK, -1)
    eye = jnp.eye(LANES, ROPE_DIM, dtype=w.dtype)
    rope_rows = jnp.concatenate([jnp.zeros((LANES, NOPE_DIM), w.dtype), eye,
                                 jnp.zeros((LANES, HEAD_PAD - QK_DIM), w.dtype)], axis=-1)
    wk_r = jnp.tile(rope_rows, (1, N_HEADS))
    wk = jnp.concatenate([wk_c, wk_r], axis=0)
    wv = w3[..., NOPE_DIM:].reshape(KV_RANK, D_B)
    return wk.astype(BF16), wv.astype(BF16)


def _rope_tables(seq, rotate):
    pos = np.arange(seq)
    half = ROPE_DIM // 2
    inv = ROPE_BASE ** (-np.arange(0, half, 2, dtype=np.float64) / half)
    r = (pos // GRID_W)[:, None] * inv
    c = (pos % GRID_W)[:, None] * inv
    ang = np.concatenate([r, r, c, c], axis=-1) * (1.0 if rotate else 0.0)
    cos_q = np.zeros((seq, LANES))
    sin_q = np.zeros((seq, LANES))
    cos_k = np.zeros((seq, LANES))
    sin_k = np.zeros((seq, LANES))
    cos_q[:, :NOPE_DIM] = 1.0
    cos_q[:, NOPE_DIM:QK_DIM] = np.cos(ang)
    sin_q[:, NOPE_DIM:QK_DIM] = np.sin(ang)
    cos_k[:, :ROPE_DIM] = np.cos(ang)
    sin_k[:, :ROPE_DIM] = np.sin(ang)
    return tuple(jnp.asarray(t.astype(np.float32)) for t in (cos_q, sin_q, cos_k, sin_k))


def kernel(x_prompt, x_sample, c, cache_ckv, cache_krope, c_ctx, norm_g, w_mod, b_mod,
           w_in_e, conv_w, conv_b, filt_w1, filt_b1, filt_w2, filt_b2, filt_w3, hyena_skip,
           q_norm_g, kv_norm_g, w_uq, w_ukv, w_out_e, w_in_o, fnet_w, w_out_o, final_g):
    nb_p, seq_p, _ = x_prompt.shape
    nb_s, seq_s, _ = x_sample.shape
    tp = nb_p * seq_p
    ts = nb_s * seq_s
    tm = 512

    cond = jnp.concatenate([c_ctx[None, :], c, jnp.zeros((8 - 1 - nb_s, D_MODEL), F32)], axis=0)
    mod = _ada_params(cond, w_mod, b_mod)
    shift = mod[:, :, None, :D_MODEL]
    scale = (1.0 + mod[:, :, None, D_MODEL:2 * D_MODEL]) * norm_g[:, None, None, :]
    gate = mod[:, :, None, 2 * D_MODEL:]

    xp = x_prompt.reshape(tp, D_MODEL)
    xs = x_sample.reshape(ts, D_MODEL)

    w_in = _pack_w_in_even(w_in_e[0])
    qg = q_norm_g[0].reshape(1, Q_RANK)
    kvg = kv_norm_g[0].reshape(1, KV_RANK)
    pr_p = _inproj_even(xp, scale[0, :1], shift[0, :1], w_in, qg, kvg, tp // tm, tm)
    pr_s = _inproj_even(xs, scale[0, 1:1 + nb_s], shift[0, 1:1 + nb_s], w_in, qg, kvg,
                        seq_s // tm, tm)
    ua_p, ga_p, cq_p, ckv_p, gb_p, kra_p, krb_p = pr_p
    ua_s, ga_s, cq_s, ckv_s, gb_s, kra_s, krb_s = pr_s

    filt = (filt_w1[0], filt_b1[0], filt_w2[0], filt_b2[0], filt_w3[0])
    ya = []
    for ua, ga, nb, seq in ((ua_p, ga_p, nb_p, seq_p), (ua_s, ga_s, nb_s, seq_s)):
        h, inv_norm = _hyena_filter(seq, *filt)
        kf = _spectrum(h, inv_norm, seq)
        ya.append(_hyena(ua.reshape(nb, seq, 3 * D_A), ga.reshape(nb, seq, D_A),
                         conv_w[0], conv_b[0], hyena_skip[0], kf))
    ya_p = ya[0].reshape(tp, D_A)
    ya_s = ya[1].reshape(ts, D_A)

    wq_a, wq_b = _pack_w_uq(w_uq[0])
    wk, wv = _pack_w_ukv(w_ukv[0])
    cos_q, sin_q, cos_k, sin_k = _rope_tables(seq_p, rotate=False)
    k_p, v_p = _kvup(ckv_p, kra_p, krb_p, cos_k, sin_k, wk, wv, seq_p)
    ob_p = _attention(cq_p.reshape(nb_p, seq_p, Q_RANK), cos_q, sin_q, wq_a, wq_b,
                      k_p.reshape(nb_p, seq_p, -1), v_p.reshape(nb_p, seq_p, -1),
                      gb_p.reshape(nb_p, seq_p, D_B), seq_p)

    cos_q, sin_q, cos_k, sin_k = _rope_tables(seq_s, rotate=True)
    k_s, v_s = _kvup_cached(cache_ckv, cache_krope, 0, ckv_s.reshape(nb_s, seq_s, KV_RANK),
                            kra_s.reshape(nb_s, seq_s, LANES), krb_s.reshape(nb_s, seq_s, LANES),
                            cos_k, sin_k, wk, wv, tm)
    ob_s = _attention(cq_s.reshape(nb_s, seq_s, Q_RANK), cos_q, sin_q, wq_a, wq_b, k_s, v_s,
                      gb_s.reshape(nb_s, seq_s, D_B), 256)

    wo_e = w_out_e[0].astype(BF16)
    wi_o = w_in_o[0].astype(BF16)
    x1_p, u_p, g_p = _mid(xp, ya_p, ob_p.reshape(tp, D_B), gate[0, :1], wo_e,
                          scale[1, :1], shift[1, :1], wi_o, tp // tm, tm)
    x1_s, u_s, g_s = _mid(xs, ya_s, ob_s.reshape(ts, D_B), gate[0, 1:1 + nb_s], wo_e,
                          scale[1, 1:1 + nb_s], shift[1, 1:1 + nb_s], wi_o, seq_s // tm, tm)

    y_p = _fnet_short(u_p.reshape(nb_p, seq_p, D_C), _fnet_weights(fnet_w[0], seq_p))
    y_s = _fnet_long(u_s.reshape(nb_s, seq_s, D_C), _fnet_weights(fnet_w[0], seq_s), 16,
                     seq_s // 16)
    wo_o = w_out_o[0].astype(BF16)
    fg = final_g.reshape(1, D_MODEL)
    out_p = _final(x1_p, y_p.reshape(tp, D_C), g_p, gate[1, :1], wo_o, fg, tp // tm, tm)
    out_s = _final(x1_s, y_s.reshape(ts, D_C), g_s, gate[1, 1:1 + nb_s], wo_o, fg,
                   seq_s // tm, tm)

    state_ckv = ckv_p.reshape(nb_p, 1, seq_p, KV_RANK)
    state_krope = kra_p[:, :ROPE_DIM].reshape(nb_p, 1, seq_p, ROPE_DIM)
    return (out_p.reshape(nb_p, seq_p, D_MODEL), out_s.reshape(nb_s, seq_s, D_MODEL),
            state_ckv, state_krope)
```

```python
import functools
import math

import numpy as np
import jax
import jax.numpy as jnp
from jax import lax
from jax.experimental import pallas as pl
from jax.experimental.pallas import tpu as pltpu

F32 = jnp.float32
BF16 = jnp.bfloat16

D_MODEL = 1024
GRID_W = 64
EPS = 1e-6
D_A = 512
EMB_DIM = 33
DECAY_TARGET = 1e-2
MIN_DECAY = math.log(DECAY_TARGET) / 1.5
MAX_DECAY = math.log(DECAY_TARGET) / 0.3
N_HEADS = 8
Q_RANK = 256
KV_RANK = 256
NOPE_DIM = 64
ROPE_DIM = 32
V_DIM = 64
ROPE_BASE = 10000.0
D_B = N_HEADS * V_DIM
QK_DIM = NOPE_DIM + ROPE_DIM
ATTN_SCALE = 1.0 / math.sqrt(QK_DIM)
LOG2_E = math.log2(math.e)
D_C = 1024
N_GROUPS_C = 8
G_C = D_C // N_GROUPS_C

LANES = 128
HEAD_PAD = 128
VMEM_LIMIT = 56 * 1024 * 1024

_C_UA, _C_GA, _C_CQ, _C_CKV, _C_GB, _C_KRA, _C_KRB, _C_END = (
    0, 1536, 2048, 2304, 2560, 3072, 3200, 3328)


def _params(sem):
    return pltpu.CompilerParams(dimension_semantics=sem, vmem_limit_bytes=VMEM_LIMIT)


def _rms(x):
    return x * lax.rsqrt(jnp.mean(x * x, axis=-1, keepdims=True) + EPS)


def _silu(g):
    return g * jax.nn.sigmoid(g)


def _dot(a, b):
    return jnp.dot(a, b, preferred_element_type=F32)


def _ada_kernel(cond_ref, w_ref, b_ref, o_ref):
    c = _silu(cond_ref[...])
    o_ref[...] = jnp.dot(c, w_ref[...], preferred_element_type=F32,
                         precision=lax.Precision.HIGHEST) + b_ref[...]


def _ada_params(cond, w_mod, b_mod):
    depth = w_mod.shape[0]
    n = cond.shape[0]
    tn = 1024
    return pl.pallas_call(
        _ada_kernel,
        grid=(depth, 3 * D_MODEL // tn),
        in_specs=[pl.BlockSpec((n, D_MODEL), lambda i, j: (0, 0)),
                  pl.BlockSpec((None, D_MODEL, tn), lambda i, j: (i, 0, j)),
                  pl.BlockSpec((None, 1, tn), lambda i, j: (i, 0, j))],
        out_specs=pl.BlockSpec((None, n, tn), lambda i, j: (i, 0, j)),
        out_shape=jax.ShapeDtypeStruct((depth, n, 3 * D_MODEL), F32),
        compiler_params=_params(("arbitrary", "arbitrary")),
        name="ada_params",
    )(cond, w_mod, b_mod.reshape(depth, 1, 3 * D_MODEL))


def _inproj_even_kernel(x_ref, sc_ref, sh_ref, w_ref, qg_ref, kvg_ref, cos_ref, sin_ref,
                        wk_ref, wv_ref, *refs, nctx, emit_state):
    if nctx:
        cckv_ref, ckr_ref = refs[:2]
        refs = refs[2:]
    ua_ref, ga_ref, cq_ref, gb_ref, k_ref, v_ref = refs[:6]
    state_refs = refs[6:]

    def tokens():
        h = (_rms(x_ref[...]) * sc_ref[...] + sh_ref[...]).astype(BF16)

        def proj(a, b):
            return _dot(h, w_ref[:, a:b])

        ua_ref[...] = proj(_C_UA, _C_GA)
        ga_ref[...] = proj(_C_GA, _C_CQ)
        cq_ref[...] = (_rms(proj(_C_CQ, _C_CKV)) * qg_ref[...]).astype(BF16)
        gb_ref[...] = proj(_C_GB, _C_KRA)
        ckv = _rms(proj(_C_CKV, _C_GB)) * kvg_ref[...]
        kra = proj(_C_KRA, _C_KRB)
        krb = proj(_C_KRB, _C_END)
        c = ckv.astype(BF16)
        kr = (kra * cos_ref[...] + krb * sin_ref[...]).astype(BF16)
        k_ref[...] = _dot(jnp.concatenate([c, kr], axis=-1), wk_ref[...]).astype(BF16)
        v_ref[...] = _dot(c, wv_ref[...]).astype(BF16)
        if emit_state:
            state_refs[0][...] = ckv
            state_refs[1][...] = kra

    if not nctx:
        tokens()
        return

    t = pl.program_id(1)

    @pl.when(t < nctx)
    def _():
        c = cckv_ref[...].astype(BF16)
        kr = ckr_ref[...].astype(BF16)
        k_ref[...] = (_dot(c, wk_ref[0:KV_RANK, :])
                      + _dot(kr, wk_ref[KV_RANK:KV_RANK + ROPE_DIM, :])).astype(BF16)
        v_ref[...] = _dot(c, wv_ref[...]).astype(BF16)

    pl.when(t >= nctx)(tokens)


def _inproj_even(x2d, sc, sh, w, qg, kvg, cos_k, sin_k, wk, wv, tm, emit_state, cache=None):
    t = x2d.shape[0]
    nb = sc.shape[0]
    tiles = t // nb // tm
    tiles_per_seq = cos_k.shape[0] // tm
    nctx = 0 if cache is None else cache[0].shape[2] // tm
    tok = lambda b, s: b * tiles + jnp.maximum(s - nctx, 0)
    row = lambda b, s: (tok(b, s), 0)
    fixed = lambda b, s: (0, 0)
    mod = lambda b, s: (b, 0, 0)
    pos = lambda b, s: (jnp.maximum(s - nctx, 0) % tiles_per_seq, 0)
    kvo = lambda b, s: (b, s, 0)
    widths = (3 * D_A, D_A, Q_RANK, D_B)
    dtypes = (F32, F32, BF16, F32)
    in_specs = [pl.BlockSpec((tm, D_MODEL), row),
                pl.BlockSpec((None, 1, D_MODEL), mod),
                pl.BlockSpec((None, 1, D_MODEL), mod),
                pl.BlockSpec((D_MODEL, _C_END), fixed),
                pl.BlockSpec((1, Q_RANK), fixed),
                pl.BlockSpec((1, KV_RANK), fixed),
                pl.BlockSpec((tm, LANES), pos),
                pl.BlockSpec((tm, LANES), pos),
                pl.BlockSpec((KV_RANK + LANES, N_HEADS * HEAD_PAD), fixed),
                pl.BlockSpec((KV_RANK, D_B), fixed)]
    args = [x2d, sc, sh, w, qg, kvg, cos_k, sin_k, wk, wv]
    if nctx:
        layer = cache[2]
        ctx = lambda b, s: (b, layer, jnp.minimum(s, nctx - 1), 0)
        in_specs += [pl.BlockSpec((None, None, tm, KV_RANK), ctx),
                     pl.BlockSpec((None, None, tm, ROPE_DIM), ctx)]
        args += [cache[0], cache[1]]
    out_specs = [pl.BlockSpec((tm, wd), row) for wd in widths]
    out_shape = [jax.ShapeDtypeStruct((t, wd), dt) for wd, dt in zip(widths, dtypes)]
    for wd in (N_HEADS * HEAD_PAD, D_B):
        out_specs.append(pl.BlockSpec((None, tm, wd), kvo))
        out_shape.append(jax.ShapeDtypeStruct((nb, (nctx + tiles) * tm, wd), BF16))
    if emit_state:
        for wd in (KV_RANK, LANES):
            out_specs.append(pl.BlockSpec((tm, wd), row))
            out_shape.append(jax.ShapeDtypeStruct((t, wd), F32))
    return pl.pallas_call(
        functools.partial(_inproj_even_kernel, nctx=nctx, emit_state=emit_state),
        grid=(nb, nctx + tiles),
        in_specs=in_specs,
        out_specs=out_specs,
        out_shape=out_shape,
        compiler_params=_params(("parallel", "arbitrary")),
        name="inproj_even",
    )(*args)


def _acc(acc, coef, x):
    if x is None or abs(coef) < 1e-12:
        return acc
    if abs(abs(coef) - 1.0) < 1e-12:
        if acc is None:
            return x if coef > 0 else -x
        return acc + x if coef > 0 else acc - x
    return coef * x if acc is None else acc + coef * x


def _add(a, b):
    if a is None:
        return b
    return a if b is None else a + b


def _sub(a, b):
    if b is None:
        return a
    return -b if a is None else a - b


def _cmul(ar, ai, br, bi):
    def mul(x, y):
        return None if x is None or y is None else x * y
    return _sub(mul(ar, br), mul(ai, bi)), _add(mul(ar, bi), mul(ai, br))


def _radix_parts(h):
    singles = [0] + ([h // 2] if h >= 2 and h % 2 == 0 else [])
    pairs = [(n, h - n) for n in range(1, (h + 1) // 2)]
    return singles, pairs


def _radix_fwd(k1, h, single, pair_sum, pair_diff):
    singles, pairs = _radix_parts(h)
    ar = ai = None
    for n in singles:
        th = 2.0 * math.pi * n * k1 / (2 * h)
        x = single(n)
        ar = _acc(ar, math.cos(th), x)
        ai = _acc(ai, -math.sin(th), x)
    for i, (n, _) in enumerate(pairs):
        th = 2.0 * math.pi * n * k1 / (2 * h)
        re_src, im_src = (pair_sum, pair_diff) if k1 % 2 == 0 else (pair_diff, pair_sum)
        ar = _acc(ar, math.cos(th), re_src(i))
        ai = _acc(ai, -math.sin(th), im_src(i))
    return ar, ai


def _radix_inv(h, n_total, bpr, bpi, emit):
    singles, pairs = _radix_parts(h)

    def coefs(n, k1):
        th = 2.0 * math.pi * n * k1 / (2 * h)
        w = (1.0 if k1 in (0, h) else 2.0) / n_total
        return w * math.cos(th), w * math.sin(th)

    for n in singles:
        y = None
        for k1 in range(h + 1):
            c, s = coefs(n, k1)
            y = _acc(y, c, bpr(k1) if abs(c) > 1e-12 else None)
            y = _acc(y, -s, bpi(k1) if abs(s) > 1e-12 else None)
        emit(n, y)
    for n, m in pairs:
        acc = {(0, "c"): None, (0, "s"): None, (1, "c"): None, (1, "s"): None}
        for k1 in range(h + 1):
            c, s = coefs(n, k1)
            par = k1 % 2
            acc[(par, "c")] = _acc(acc[(par, "c")], c, bpr(k1) if abs(c) > 1e-12 else None)
            acc[(par, "s")] = _acc(acc[(par, "s")], s, bpi(k1) if abs(s) > 1e-12 else None)
        even_m, even_p = _sub(acc[(0, "c")], acc[(0, "s")]), _add(acc[(0, "c")], acc[(0, "s")])
        odd_m, odd_p = _sub(acc[(1, "c")], acc[(1, "s")]), _add(acc[(1, "c")], acc[(1, "s")])
        emit(n, _add(even_m, odd_m))
        emit(m, _sub(even_p, odd_p))


def _next_twiddle(tw, tw1_ref):
    if tw is None:
        return tw1_ref[0], tw1_ref[1]
    return _cmul(tw[0], tw[1], tw1_ref[0], tw1_ref[1])


def _spectrum_kernel(hf_ref, hb_ref, inv_ref, tw1_ref, fh_ref, kf_ref, sd_scr,
                     *, h, n2, nk_pad):
    ct = hf_ref.shape[-1]
    row = lax.broadcasted_iota(jnp.int32, (n2, ct), 0)
    _, pairs = _radix_parts(h)

    def hb(n):
        return jnp.where(row == 0, 0.0, hb_ref[0]) if n == 0 else hb_ref[n]

    for i, (n, m) in enumerate(pairs):
        sd_scr[0, 0, i] = hf_ref[n] + hf_ref[m]
        sd_scr[0, 1, i] = hf_ref[n] - hf_ref[m]
        sd_scr[1, 0, i] = hb(n) + hb(m)
        sd_scr[1, 1, i] = hb(n) - hb(m)

    inv = inv_ref[...]
    tw = None
    for k1 in range(h + 1):
        far, fai = _radix_fwd(k1, h, lambda n: hf_ref[n],
                              lambda i: sd_scr[0, 0, i], lambda i: sd_scr[0, 1, i])
        bar, bai = _radix_fwd(k1, h, hb, lambda i: sd_scr[1, 0, i], lambda i: sd_scr[1, 1, i])
        if k1 > 0:
            tw = _next_twiddle(tw, tw1_ref)
            far, fai = _cmul(far, fai, tw[0], tw[1])
            bar, bai = _cmul(bar, bai, tw[0], tw[1])
        zero = jnp.zeros((n2, ct), F32)
        af = jnp.concatenate([zero if far is None else far, zero if fai is None else fai], axis=0)
        ab = jnp.concatenate([zero if bar is None else bar, zero if bai is None else bai], axis=0)
        a = jnp.concatenate([af, ab], axis=1)
        x = _dot(fh_ref[...], a.astype(BF16))
        lanes = slice(k1 * ct, (k1 + 1) * ct)
        kf_ref[0, :, lanes] = (x[:n2, :ct] + x[:n2, ct:]) * inv
        kf_ref[1, :, lanes] = (x[n2:, :ct] - x[n2:, ct:]) * inv
    if nk_pad > h + 1:
        pad = slice((h + 1) * ct, nk_pad * ct)
        kf_ref[:, :, pad] = jnp.zeros((2, n2, (nk_pad - h - 1) * ct), F32)


def _hyena_kernel(x0_ref, x1_ref, v_ref, ga_ref,
                  w0_ref, w1_ref, wv_ref, b0_ref, b1_ref, bv_ref, skip_ref,
                  kf_ref, tw1_ref, f_ref,
                  o_ref, v_scr, sd_scr, a_scr, b_scr, *, seq, nslab, n2, nk_pad):
    ct = o_ref.shape[-1]
    h = nslab
    chunk = 2 * LANES
    kpc = chunk // ct
    row = lax.broadcasted_iota(jnp.int32, (n2, ct), 0)

    def short_conv(ref, w_ref, b_ref, s, start):
        u = ref[pl.ds(start, n2), :]
        prev = ref[pl.ds(jnp.maximum(start - 1, 0), 1), :]
        prev = jnp.where(s == 0, 0.0, prev)
        nxt = ref[pl.ds(jnp.minimum(start + n2, seq - 1), 1), :]
        nxt = jnp.where(s == nslab - 1, 0.0, nxt)
        up = jnp.where(row == 0, prev, pltpu.roll(u, 1, 0))
        un = jnp.where(row == n2 - 1, nxt, pltpu.roll(u, n2 - 1, 0))
        return b_ref[...] + up * w_ref[0:1, :] + u * w_ref[1:2, :] + un * w_ref[2:3, :]

    def prep(s, carry):
        start = pl.multiple_of(s * n2, n2)
        x0 = short_conv(x0_ref, w0_ref, b0_ref, s, start)
        x1 = short_conv(x1_ref, w1_ref, b1_ref, s, start)
        v_scr[s] = short_conv(v_ref, wv_ref, bv_ref, s, start) * x1
        o_ref[pl.ds(start, n2), :] = x0 * _silu(ga_ref[pl.ds(start, n2), :])
        return carry

    lax.fori_loop(0, nslab, prep, 0)

    _, pairs = _radix_parts(h)
    for i, (n, m) in enumerate(pairs):
        sd_scr[0, i] = v_scr[n] + v_scr[m]
        sd_scr[1, i] = v_scr[n] - v_scr[m]
    zero = jnp.zeros((n2, ct), BF16)
    tw = None
    for k1 in range(h + 1):
        ar, ai = _radix_fwd(k1, h, lambda n: v_scr[n], lambda i: sd_scr[0, i],
                            lambda i: sd_scr[1, i])
        if k1 > 0:
            tw = _next_twiddle(tw, tw1_ref)
            ar, ai = _cmul(ar, ai, tw[0], tw[1])
        lanes = slice(k1 * ct, (k1 + 1) * ct)
        a_scr[0:n2, lanes] = zero if ar is None else ar.astype(BF16)
        a_scr[n2:2 * n2, lanes] = zero if ai is None else ai.astype(BF16)
    for k1 in range(h + 1, nk_pad):
        lanes = slice(k1 * ct, (k1 + 1) * ct)
        a_scr[0:n2, lanes] = zero
        a_scr[n2:2 * n2, lanes] = zero

    tw = None
    for j in range(nk_pad // kpc):
        cols = slice(j * chunk, (j + 1) * chunk)
        x = _dot(f_ref[...], a_scr[:, cols])
        xr, xi = x[:n2], x[n2:]
        kr = kf_ref[0, :, cols]
        ki = kf_ref[1, :, cols]
        z = jnp.concatenate([xr * kr - xi * ki, -(xr * ki + xi * kr)], axis=0)
        bt = _dot(f_ref[...], z.astype(BF16))
        for kk in range(kpc):
            k1 = j * kpc + kk
            if k1 > h:
                continue
            lanes = slice(kk * ct, (kk + 1) * ct)
            br = bt[:n2, lanes]
            bi = -bt[n2:, lanes]
            if k1 > 0:
                tw = _next_twiddle(tw, tw1_ref)
                br, bi = _cmul(br, bi, tw[0], -tw[1])
            b_scr[0, k1] = br
            if k1 not in (0, h):
                b_scr[1, k1] = bi

    def emit(n, y):
        rows = pl.ds(n * n2, n2)
        o_ref[rows, :] = (y + v_scr[n] * skip_ref[...]) * o_ref[rows, :]

    _radix_inv(h, 2 * seq, lambda k1: b_scr[0, k1], lambda k1: b_scr[1, k1], emit)


def _dft_cos_sin(n2, shift=0):
    idx = np.arange(n2)
    ang = 2.0 * np.pi * (((idx[:, None] + shift) * idx[None, :]) % n2) / n2
    return np.cos(ang), np.sin(ang)


def _bf16_split(x):
    x32 = jnp.asarray(x.astype(np.float32))
    hi = x32.astype(BF16)
    lo = (x32 - hi.astype(F32)).astype(BF16)
    return hi, lo


def _dft_block(n2):
    c, s = _dft_cos_sin(n2)
    return np.block([[c, s], [-s, c]])


def _hyena_plan(seq):
    n2 = min(seq, 512)
    n1 = 2 * seq // n2
    ct = LANES if seq > 512 else 2 * LANES
    kpc = 2 * LANES // ct
    nk_pad = -(-(n1 // 2 + 1) // kpc) * kpc
    return n1, n2, ct, nk_pad


def _twiddle1(n1, n2, ct):
    ang = 2.0 * np.pi * np.arange(n2) / (n1 * n2)
    tw = np.stack([np.cos(ang), -np.sin(ang)]).astype(np.float32)
    return jnp.broadcast_to(jnp.asarray(tw)[..., None], tw.shape + (ct,))


def _spectrum(h, inv_norm, seq):
    n1, n2, ct, nk_pad = _hyena_plan(seq)
    nslab = n1 // 2
    nct = D_A // ct
    npair = len(_radix_parts(nslab)[1])
    ffwd = _bf16_split(_dft_block(n2))[0]
    h3 = h.reshape(nslab, n2, 2 * D_A)
    fixed2 = lambda c: (0, 0)
    return pl.pallas_call(
        functools.partial(_spectrum_kernel, h=nslab, n2=n2, nk_pad=nk_pad),
        grid=(nct,),
        in_specs=[pl.BlockSpec((nslab, n2, ct), lambda c: (0, 0, c)),
                  pl.BlockSpec((nslab, n2, ct), lambda c: (0, 0, nct + c)),
                  pl.BlockSpec((1, ct), lambda c: (0, c)),
                  pl.BlockSpec((2, n2, ct), lambda c: (0, 0, 0)),
                  pl.BlockSpec((2 * n2, 2 * n2), fixed2)],
        out_specs=pl.BlockSpec((None, 2, n2, nk_pad * ct), lambda c: (c, 0, 0, 0)),
        out_shape=jax.ShapeDtypeStruct((nct, 2, n2, nk_pad * ct), F32),
        scratch_shapes=[pltpu.VMEM((2, 2, max(npair, 1), n2, ct), F32)],
        compiler_params=_params(("parallel",)),
        name="filter_spectrum",
    )(h3, h3, inv_norm, _twiddle1(n1, n2, ct), ffwd)


def _hyena(ua, ga, conv_w, conv_b, skip, kf):
    bsz, seq, _ = ua.shape
    n1, n2, ct, nk_pad = _hyena_plan(seq)
    nslab = n1 // 2
    nct = D_A // ct
    npair = len(_radix_parts(nslab)[1])
    cb = conv_b.reshape(1, 3 * D_A)
    sk = skip.reshape(1, D_A)
    part = lambda p: pl.BlockSpec((None, seq, ct), lambda c, b: (b, 0, p * nct + c))
    wpart = lambda p: pl.BlockSpec((3, ct), lambda c, b: (0, p * nct + c))
    bpart = lambda p: pl.BlockSpec((1, ct), lambda c, b: (0, p * nct + c))
    return pl.pallas_call(
        functools.partial(_hyena_kernel, seq=seq, nslab=nslab, n2=n2, nk_pad=nk_pad),
        grid=(nct, bsz),
        in_specs=[part(0), part(1), part(2),
                  pl.BlockSpec((None, seq, ct), lambda c, b: (b, 0, c)),
                  wpart(0), wpart(1), wpart(2), bpart(0), bpart(1), bpart(2),
                  pl.BlockSpec((1, ct), lambda c, b: (0, c)),
                  pl.BlockSpec((None, 2, n2, nk_pad * ct), lambda c, b: (c, 0, 0, 0)),
                  pl.BlockSpec((2, n2, ct), lambda c, b: (0, 0, 0)),
                  pl.BlockSpec((2 * n2, 2 * n2), lambda c, b: (0, 0))],
        out_specs=pl.BlockSpec((None, seq, ct), lambda c, b: (b, 0, c)),
        out_shape=jax.ShapeDtypeStruct((bsz, seq, D_A), F32),
        scratch_shapes=[pltpu.VMEM((nslab, n2, ct), F32),
                        pltpu.VMEM((2, max(npair, 1), n2, ct), F32),
                        pltpu.VMEM((2 * n2, nk_pad * ct), BF16),
                        pltpu.VMEM((2, nk_pad, n2, ct), F32)],
        compiler_params=_params(("parallel", "parallel")),
        name="hyena",
    )(ua, ua, ua, ga, conv_w, conv_w, conv_w, cb, cb, cb, sk, kf, _twiddle1(n1, n2, ct),
      _bf16_split(_dft_block(n2))[0])


def _hyena_filter_kernel(z_ref, w1_ref, b1_ref, w2_ref, b2_ref, w3_ref, del_ref,
                         o_ref, nrm_ref, *, seq):
    hp = lax.Precision.HIGHEST
    tm = o_ref.shape[0]
    i = pl.program_id(0)
    h = jnp.sin(jnp.dot(z_ref[...], w1_ref[...], precision=hp,
                        preferred_element_type=F32) + b1_ref[...])
    h = jnp.sin(jnp.dot(h, w2_ref[...], precision=hp, preferred_element_type=F32) + b2_ref[...])
    tcol = (i * tm + lax.broadcasted_iota(jnp.int32, (tm, 1), 0)).astype(F32) * (1.0 / (seq - 1))
    h = jnp.dot(h, w3_ref[...], precision=hp, preferred_element_type=F32) * jnp.exp(
        -tcol * del_ref[...])
    o_ref[...] = h

    @pl.when(i == 0)
    def _():
        nrm_ref[...] = jnp.zeros_like(nrm_ref)

    nrm_ref[...] += jnp.sum(jnp.abs(h), axis=0, keepdims=True)


def _hyena_filter(seq, w1, b1, w2, b2, w3):
    bands = (EMB_DIM - 1) // 2
    ang = (2.0 * np.pi * np.arange(seq)[:, None] / seq) * np.linspace(1e-4, bands - 1, bands)
    z = np.zeros((seq, LANES), np.float32)
    z[:, 0] = np.linspace(0.0, 1.0, seq)
    z[:, 1:1 + bands] = np.cos(ang)
    z[:, 1 + bands:EMB_DIM] = -np.sin(ang)
    deltas = np.abs(np.linspace(MIN_DECAY, MAX_DECAY, D_A)).astype(np.float32)
    deltas2 = np.concatenate([deltas, deltas])[None, :]
    w1p = jnp.pad(w1, ((0, LANES - EMB_DIM), (0, 0)))
    tm = min(seq, 512)
    fo = w1.shape[1]
    fixed = lambda i: (0, 0)
    h, nrm = pl.pallas_call(
        functools.partial(_hyena_filter_kernel, seq=seq),
        grid=(seq // tm,),
        in_specs=[pl.BlockSpec((tm, LANES), lambda i: (i, 0)),
                  pl.BlockSpec((LANES, fo), fixed),
                  pl.BlockSpec((1, fo), fixed),
                  pl.BlockSpec((fo, fo), fixed),
                  pl.BlockSpec((1, fo), fixed),
                  pl.BlockSpec((fo, 2 * D_A), fixed),
                  pl.BlockSpec((1, 2 * D_A), fixed)],
        out_specs=[pl.BlockSpec((tm, 2 * D_A), lambda i: (i, 0)),
                   pl.BlockSpec((1, 2 * D_A), fixed)],
        out_shape=[jax.ShapeDtypeStruct((seq, 2 * D_A), F32),
                   jax.ShapeDtypeStruct((1, 2 * D_A), F32)],
        compiler_params=_params(("arbitrary",)),
        name="hyena_filter",
    )(jnp.asarray(z), w1p, b1.reshape(1, fo), w2, b2.reshape(1, fo), w3, jnp.asarray(deltas2))
    return h, 1.0 / (nrm[:, :D_A] + nrm[:, D_A:])


def _attn_kernel(cq_ref, cos_ref, sin_ref, wa_ref, wb_ref, k_ref, v_ref, gb_ref, o_ref):
    nt = (((1,), (1,)), ((), ()))
    cq = cq_ref[...]
    cos = jnp.concatenate([cos_ref[...]] * N_HEADS, axis=-1)
    sin = jnp.concatenate([sin_ref[...]] * N_HEADS, axis=-1)
    q = (_dot(cq, wa_ref[...]) * cos + _dot(cq, wb_ref[...]) * sin) * (ATTN_SCALE * LOG2_E)
    q = q.astype(BF16)
    lane = lax.broadcasted_iota(jnp.int32, (q.shape[0], 2 * V_DIM), 1)
    for pair in range(N_HEADS // 2):
        vsl = slice(pair * 2 * V_DIM, (pair + 1) * 2 * V_DIM)
        outs = []
        for hh in range(2):
            hsl = slice((2 * pair + hh) * HEAD_PAD, (2 * pair + hh + 1) * HEAD_PAD)
            s = lax.dot_general(q[:, hsl], k_ref[:, hsl], nt, preferred_element_type=F32)
            p = jnp.exp2(s - jnp.max(s, axis=-1, keepdims=True))
            denom = jnp.sum(p, axis=-1, keepdims=True)
            outs.append(_dot(p.astype(BF16), v_ref[:, vsl]) / denom)
        o = jnp.where(lane < V_DIM, outs[0], outs[1])
        o_ref[:, vsl] = (o * _silu(gb_ref[:, vsl])).astype(BF16)


def _attention(cq, cos, sin, wa, wb, k, v, gb, tq):
    bsz, lq, _ = cq.shape
    lk = k.shape[1]
    fixed = lambda b, i: (0, 0)
    qrow = lambda b, i: (b, i, 0)
    kv = lambda b, i: (b, 0, 0)
    pos = lambda b, i: (i, 0)
    return pl.pallas_call(
        _attn_kernel,
        grid=(bsz, lq // tq),
        in_specs=[pl.BlockSpec((None, tq, Q_RANK), qrow),
                  pl.BlockSpec((tq, LANES), pos),
                  pl.BlockSpec((tq, LANES), pos),
                  pl.BlockSpec((Q_RANK, N_HEADS * HEAD_PAD), fixed),
                  pl.BlockSpec((Q_RANK, N_HEADS * HEAD_PAD), fixed),
                  pl.BlockSpec((None, lk, N_HEADS * HEAD_PAD), kv),
                  pl.BlockSpec((None, lk, D_B), kv),
                  pl.BlockSpec((None, tq, D_B), qrow)],
        out_specs=pl.BlockSpec((None, tq, D_B), qrow),
        out_shape=jax.ShapeDtypeStruct((bsz, lq, D_B), BF16),
        compiler_params=_params(("parallel", "arbitrary")),
        name="mla_attention",
    )(cq, cos, sin, wa, wb, k, v, gb)


def _mid_kernel(x_ref, ya_ref, ob_ref, gt_ref, wo_ref, sc_ref, sh_ref, wi_ref,
                x1_ref, u_ref, g_ref):
    y = _dot(ya_ref[...].astype(BF16), wo_ref[0:D_A, :]) + _dot(ob_ref[...], wo_ref[D_A:, :])
    x1 = x_ref[...] + gt_ref[...] * y
    x1_ref[...] = x1
    h = (_rms(x1) * sc_ref[...] + sh_ref[...]).astype(BF16)
    u_ref[...] = _dot(h, wi_ref[:, :D_C]).astype(BF16)
    g_ref[...] = _dot(h, wi_ref[:, D_C:]).astype(BF16)


def _mid(x2d, ya, ob, gt, wo, sc, sh, wi, tiles_per_mod, tm):
    t = x2d.shape[0]
    row = lambda i: (i, 0)
    fixed = lambda i: (0, 0)
    mod = lambda i: (i // tiles_per_mod, 0, 0)
    return pl.pallas_call(
        _mid_kernel,
        grid=(t // tm,),
        in_specs=[pl.BlockSpec((tm, D_MODEL), row),
                  pl.BlockSpec((tm, D_A), row),
                  pl.BlockSpec((tm, D_B), row),
                  pl.BlockSpec((None, 1, D_MODEL), mod),
                  pl.BlockSpec((D_A + D_B, D_MODEL), fixed),
                  pl.BlockSpec((None, 1, D_MODEL), mod),
                  pl.BlockSpec((None, 1, D_MODEL), mod),
                  pl.BlockSpec((D_MODEL, 2 * D_C), fixed)],
        out_specs=[pl.BlockSpec((tm, D_MODEL), row),
                   pl.BlockSpec((tm, D_C), row),
                   pl.BlockSpec((tm, D_C), row)],
        out_shape=[jax.ShapeDtypeStruct((t, D_MODEL), F32),
                   jax.ShapeDtypeStruct((t, D_C), BF16),
                   jax.ShapeDtypeStruct((t, D_C), BF16)],
        compiler_params=_params(("parallel",)),
        name="outproj_even_inproj_odd",
    )(x2d, ya, ob, gt, wo, sc, sh, wi)


def _fnet_weights_kernel(w_ref, cs_ref, o_ref):
    hp = lax.Precision.HIGHEST
    o_ref[...] = jnp.dot(cs_ref[...], w_ref[...], precision=hp,
                         preferred_element_type=F32).astype(BF16)


def _fnet_weights(fnet_w, seq):
    c, s = _dft_cos_sin(G_C)
    cs = jnp.asarray((np.concatenate([c, s], axis=0) / math.sqrt(seq * G_C)).astype(np.float32))
    return pl.pallas_call(
        _fnet_weights_kernel,
        grid=(N_GROUPS_C,),
        in_specs=[pl.BlockSpec((None, G_C, G_C), lambda g: (g, 0, 0)),
                  pl.BlockSpec((2 * G_C, G_C), lambda g: (0, 0))],
        out_specs=pl.BlockSpec((None, 2 * G_C, G_C), lambda g: (g, 0, 0)),
        out_shape=jax.ShapeDtypeStruct((N_GROUPS_C, 2 * G_C, G_C), BF16),
        compiler_params=_params(("parallel",)),
        name="fnet_weights",
    )(fnet_w, cs)


def _fnet_long_kernel(u_ref, tw1_ref, m_ref, ab_ref, o_ref, sd_scr, a_scr, *, n1, n2, nk_pad):
    h = n1 // 2
    ct = o_ref.shape[-1]
    chunk = 2 * LANES
    kpc = chunk // ct
    for i, n in enumerate(range(1, h)):
        a = u_ref[n].astype(F32)
        b = u_ref[n1 - n].astype(F32)
        sd_scr[0, i] = a + b
        sd_scr[1, i] = a - b
    x0 = u_ref[0].astype(F32)
    xh = u_ref[h].astype(F32)
    zero = jnp.zeros((n2, ct), BF16)
    tw = None
    for k1 in range(h + 1):
        ar = x0 + xh if k1 % 2 == 0 else x0 - xh
        ai = None
        for i, n in enumerate(range(1, h)):
            th = 2.0 * math.pi * n * k1 / n1
            ar = _acc(ar, math.cos(th), sd_scr[0, i])
            ai = _acc(ai, -math.sin(th), sd_scr[1, i])
        if k1 > 0:
            tw = _next_twiddle(tw, tw1_ref)
            ar, ai = _cmul(ar, ai, tw[0], tw[1])
        lanes = slice(k1 * ct, (k1 + 1) * ct)
        a_scr[0:n2, lanes] = ar.astype(BF16)
        a_scr[n2:2 * n2, lanes] = zero if ai is None else ai.astype(BF16)
    for k1 in range(h + 1, nk_pad):
        lanes = slice(k1 * ct, (k1 + 1) * ct)
        a_scr[0:n2, lanes] = zero
        a_scr[n2:2 * n2, lanes] = zero

    def group_map(xr, xi):
        y = _dot(jnp.concatenate([xr, xi], axis=1).astype(BF16), ab_ref[...])
        return y.astype(o_ref.dtype)

    for j in range(nk_pad // kpc):
        x = _dot(m_ref[...], a_scr[:, j * chunk:(j + 1) * chunk])
        for kk in range(kpc):
            k1 = j * kpc + kk
            if k1 > h:
                continue
            lanes = slice(kk * ct, (kk + 1) * ct)
            o_ref[k1] = group_map(x[:n2, lanes], x[n2:2 * n2, lanes])
            if 0 < k1 < h:
                o_ref[n1 - k1] = group_map(x[2 * n2:3 * n2, lanes], x[3 * n2:, lanes])


def _fnet_long(u, ab, n1, n2):
    bsz, seq, _ = u.shape
    h = n1 // 2
    kpc = 2 * LANES // G_C
    nk_pad = -(-(h + 1) // kpc) * kpc
    c, s = _dft_cos_sin(n2)
    ce, se = _dft_cos_sin(n2, shift=1)
    m = _bf16_split(np.block([[c, s], [-s, c], [ce, -se], [-se, -ce]]))[0]
    out = pl.pallas_call(
        functools.partial(_fnet_long_kernel, n1=n1, n2=n2, nk_pad=nk_pad),
        grid=(bsz, N_GROUPS_C),
        in_specs=[pl.BlockSpec((None, n1, n2, G_C), lambda b, g: (b, 0, 0, g)),
                  pl.BlockSpec((2, n2, G_C), lambda b, g: (0, 0, 0)),
                  pl.BlockSpec((4 * n2, 2 * n2), lambda b, g: (0, 0)),
                  pl.BlockSpec((None, 2 * G_C, G_C), lambda b, g: (g, 0, 0))],
        out_specs=pl.BlockSpec((None, n1, n2, G_C), lambda b, g: (b, 0, 0, g)),
        out_shape=jax.ShapeDtypeStruct((bsz, n1, n2, D_C), BF16),
        scratch_shapes=[pltpu.VMEM((2, h - 1, n2, G_C), F32),
                        pltpu.VMEM((2 * n2, nk_pad * G_C), BF16)],
        compiler_params=_params(("parallel", "parallel")),
        name="fnet_long",
    )(u.reshape(bsz, n1, n2, D_C), _twiddle1(n1, n2, G_C), m, ab)
    return out.transpose(0, 2, 1, 3).reshape(bsz, seq, D_C)


def _fnet_short_kernel(u_ref, f_ref, ab_ref, o_ref, *, seq, scale):
    x = _dot(f_ref[...], u_ref[...])
    xr, xi = x[:seq], x[seq:]
    for g in range(N_GROUPS_C):
        sl = slice(g * G_C, (g + 1) * G_C)
        xin = jnp.concatenate([xr[:, sl], xi[:, sl]], axis=1).astype(BF16)
        o_ref[:, sl] = (_dot(xin, ab_ref[g]) * scale).astype(o_ref.dtype)


def _fnet_short(u, ab, scale):
    bsz, seq, _ = u.shape
    c, s = _dft_cos_sin(seq)
    f = _bf16_split(np.concatenate([c, -s], axis=0))[0]
    return pl.pallas_call(
        functools.partial(_fnet_short_kernel, seq=seq, scale=scale),
        grid=(bsz,),
        in_specs=[pl.BlockSpec((None, seq, D_C), lambda b: (b, 0, 0)),
                  pl.BlockSpec((2 * seq, seq), lambda b: (0, 0)),
                  pl.BlockSpec((N_GROUPS_C, 2 * G_C, G_C), lambda b: (0, 0, 0))],
        out_specs=pl.BlockSpec((None, seq, D_C), lambda b: (b, 0, 0)),
        out_shape=jax.ShapeDtypeStruct((bsz, seq, D_C), BF16),
        compiler_params=_params(("parallel",)),
        name="fnet_short",
    )(u, f, ab)


def _final_kernel(x1_ref, y_ref, g_ref, gt_ref, wo_ref, fg_ref, o_ref):
    z = (y_ref[...].astype(F32) * _silu(g_ref[...].astype(F32))).astype(BF16)
    x2 = x1_ref[...] + gt_ref[...] * _dot(z, wo_ref[...])
    o_ref[...] = _rms(x2) * fg_ref[...]


def _final(x1, y, g, gt, wo, fg, tiles_per_mod, tm):
    t = x1.shape[0]
    row = lambda i: (i, 0)
    fixed = lambda i: (0, 0)
    mod = lambda i: (i // tiles_per_mod, 0, 0)
    return pl.pallas_call(
        _final_kernel,
        grid=(t // tm,),
        in_specs=[pl.BlockSpec((tm, D_MODEL), row),
                  pl.BlockSpec((tm, D_C), row),
                  pl.BlockSpec((tm, D_C), row),
                  pl.BlockSpec((None, 1, D_MODEL), mod),
                  pl.BlockSpec((D_C, D_MODEL), fixed),
                  pl.BlockSpec((1, D_MODEL), fixed)],
        out_specs=pl.BlockSpec((tm, D_MODEL), row),
        out_shape=jax.ShapeDtypeStruct((t, D_MODEL), F32),
        compiler_params=_params(("parallel",)),
        name="outproj_odd_final",
    )(x1, y, g, gt, wo, fg)


def _rope_swap(w):
    nf = ROPE_DIM // 4
    w4 = w.reshape(w.shape[:-1] + (2, 2, nf))
    return jnp.stack([-w4[..., 1, :], w4[..., 0, :]], axis=-2).reshape(w.shape)


def _pack_w_in_even(w):
    ua_ga_cq_ckv = w[:, :4 * D_A + Q_RANK + KV_RANK]
    o = 4 * D_A + Q_RANK + KV_RANK
    krope = w[:, o:o + ROPE_DIM]
    gb = w[:, o + ROPE_DIM:]
    zpad = jnp.zeros((w.shape[0], LANES - ROPE_DIM), w.dtype)
    return jnp.concatenate([ua_ga_cq_ckv, gb, krope, zpad, _rope_swap(krope), zpad],
                           axis=1).astype(BF16)


def _pack_w_uq(w):
    w3 = w.reshape(Q_RANK, N_HEADS, QK_DIM)
    zq = jnp.zeros((Q_RANK, N_HEADS, HEAD_PAD - QK_DIM), w.dtype)
    wa = jnp.concatenate([w3, zq], axis=-1)
    zn = jnp.zeros((Q_RANK, N_HEADS, NOPE_DIM), w.dtype)
    wb = jnp.concatenate([zn, _rope_swap(w3[..., NOPE_DIM:]), zq], axis=-1)
    shape = (Q_RANK, N_HEADS * HEAD_PAD)
    return wa.reshape(shape).astype(BF16), wb.reshape(shape).astype(BF16)


def _pack_w_ukv(w):
    w3 = w.reshape(KV_RANK, N_HEADS, NOPE_DIM + V_DIM)
    zk = jnp.zeros((KV_RANK, N_HEADS, HEAD_PAD - NOPE_DIM), w.dtype)
    wk_c = jnp.concatenate([w3[..., :NOPE_DIM], zk], axis=-1).reshape(KV_RANK, -1)
    eye = jnp.eye(LANES, ROPE_DIM, dtype=w.dtype)
    rope_rows = jnp.concatenate([jnp.zeros((LANES, NOPE_DIM), w.dtype), eye,
                                 jnp.zeros((LANES, HEAD_PAD - QK_DIM), w.dtype)], axis=-1)
    wk_r = jnp.tile(rope_rows, (1, N_HEADS))
    wk = jnp.concatenate([wk_c, wk_r], axis=0)
    wv = w3[..., NOPE_DIM:].reshape(KV_RANK, D_B)
    return wk.astype(BF16), wv.astype(BF16)


def _rope_tables(seq, rotate):
    pos = np.arange(seq)
    half = ROPE_DIM // 2
    inv = ROPE_BASE ** (-np.arange(0, half, 2, dtype=np.float64) / half)
    r = (pos // GRID_W)[:, None] * inv
    c = (pos % GRID_W)[:, None] * inv
    ang = np.concatenate([r, r, c, c], axis=-1) * (1.0 if rotate else 0.0)
    cos_q = np.zeros((seq, LANES))
    sin_q = np.zeros((seq, LANES))
    cos_k = np.zeros((seq, LANES))
    sin_k = np.zeros((seq, LANES))
    cos_q[:, :NOPE_DIM] = 1.0
    cos_q[:, NOPE_DIM:QK_DIM] = np.cos(ang)
    sin_q[:, NOPE_DIM:QK_DIM] = np.sin(ang)
    cos_k[:, :ROPE_DIM] = np.cos(ang)
    sin_k[:, :ROPE_DIM] = np.sin(ang)
    return tuple(jnp.asarray(t.astype(np.float32)) for t in (cos_q, sin_q, cos_k, sin_k))


def kernel(x_prompt, x_sample, c, cache_ckv, cache_krope, c_ctx, norm_g, w_mod, b_mod,
           w_in_e, conv_w, conv_b, filt_w1, filt_b1, filt_w2, filt_b2, filt_w3, hyena_skip,
           q_norm_g, kv_norm_g, w_uq, w_ukv, w_out_e, w_in_o, fnet_w, w_out_o, final_g):
    nb_p, seq_p, _ = x_prompt.shape
    nb_s, seq_s, _ = x_sample.shape
    tp = nb_p * seq_p
    ts = nb_s * seq_s
    tm = 512

    cond = jnp.concatenate([c_ctx[None, :], c, jnp.zeros((8 - 1 - nb_s, D_MODEL), F32)], axis=0)
    mod = _ada_params(cond, w_mod, b_mod)
    shift = mod[:, :, None, :D_MODEL]
    scale = (1.0 + mod[:, :, None, D_MODEL:2 * D_MODEL]) * norm_g[:, None, None, :]
    gate = mod[:, :, None, 2 * D_MODEL:]

    xp = x_prompt.reshape(tp, D_MODEL)
    xs = x_sample.reshape(ts, D_MODEL)

    w_in = _pack_w_in_even(w_in_e[0])
    qg = q_norm_g[0].reshape(1, Q_RANK)
    kvg = kv_norm_g[0].reshape(1, KV_RANK)
    wq_a, wq_b = _pack_w_uq(w_uq[0])
    wk, wv = _pack_w_ukv(w_ukv[0])
    cos_qp, sin_qp, cos_kp, sin_kp = _rope_tables(tm, rotate=False)
    cos_qs, sin_qs, cos_ks, sin_ks = _rope_tables(seq_s, rotate=True)
    pr_p = _inproj_even(xp, scale[0, :1], shift[0, :1], w_in, qg, kvg, cos_kp, sin_kp, wk, wv,
                        tm, True)
    pr_s = _inproj_even(xs, scale[0, 1:1 + nb_s], shift[0, 1:1 + nb_s], w_in, qg, kvg,
                        cos_ks, sin_ks, wk, wv, tm, False, cache=(cache_ckv, cache_krope, 0))
    ua_p, ga_p, cq_p, gb_p, k_p, v_p, ckv_p, kra_p = pr_p
    ua_s, ga_s, cq_s, gb_s, k_s, v_s = pr_s

    filt = (filt_w1[0], filt_b1[0], filt_w2[0], filt_b2[0], filt_w3[0])
    ya = []
    for ua, ga, nb, seq in ((ua_p, ga_p, nb_p, seq_p), (ua_s, ga_s, nb_s, seq_s)):
        h, inv_norm = _hyena_filter(seq, *filt)
        kf = _spectrum(h, inv_norm, seq)
        ya.append(_hyena(ua.reshape(nb, seq, 3 * D_A), ga.reshape(nb, seq, D_A),
                         conv_w[0], conv_b[0], hyena_skip[0], kf))
    ya_p = ya[0].reshape(tp, D_A)
    ya_s = ya[1].reshape(ts, D_A)

    ob_p = _attention(cq_p.reshape(nb_p, seq_p, Q_RANK), cos_qp, sin_qp, wq_a, wq_b,
                      k_p.reshape(nb_p, seq_p, -1), v_p.reshape(nb_p, seq_p, -1),
                      gb_p.reshape(nb_p, seq_p, D_B), seq_p)
    ob_s = _attention(cq_s.reshape(nb_s, seq_s, Q_RANK), cos_qs, sin_qs, wq_a, wq_b, k_s, v_s,
                      gb_s.reshape(nb_s, seq_s, D_B), 256)

    wo_e = w_out_e[0].astype(BF16)
    wi_o = w_in_o[0].astype(BF16)
    x1_p, u_p, g_p = _mid(xp, ya_p, ob_p.reshape(tp, D_B), gate[0, :1], wo_e,
                          scale[1, :1], shift[1, :1], wi_o, tp // tm, tm)
    x1_s, u_s, g_s = _mid(xs, ya_s, ob_s.reshape(ts, D_B), gate[0, 1:1 + nb_s], wo_e,
                          scale[1, 1:1 + nb_s], shift[1, 1:1 + nb_s], wi_o, seq_s // tm, tm)

    ab = _fnet_weights(fnet_w[0], seq_s)
    y_p = _fnet_short(u_p.reshape(nb_p, seq_p, D_C), ab, math.sqrt(seq_s / seq_p))
    y_s = _fnet_long(u_s.reshape(nb_s, seq_s, D_C), ab, 16, seq_s // 16)
    wo_o = w_out_o[0].astype(BF16)
    fg = final_g.reshape(1, D_MODEL)
    out_p = _final(x1_p, y_p.reshape(tp, D_C), g_p, gate[1, :1], wo_o, fg, tp // tm, tm)
    out_s = _final(x1_s, y_s.reshape(ts, D_C), g_s, gate[1, 1:1 + nb_s], wo_o, fg,
                   seq_s // tm, tm)

    state_ckv = ckv_p.reshape(nb_p, 1, seq_p, KV_RANK)
    state_krope = kra_p[:, :ROPE_DIM].reshape(nb_p, 1, seq_p, ROPE_DIM)
    return (out_p.reshape(nb_p, seq_p, D_MODEL), out_s.reshape(nb_s, seq_s, D_MODEL),
            state_ckv, state_krope)
```

```python
import functools
import math

import numpy as np
import jax
import jax.numpy as jnp
from jax import lax
from jax.experimental import pallas as pl
from jax.experimental.pallas import tpu as pltpu

F32 = jnp.float32
BF16 = jnp.bfloat16

D_MODEL = 1024
GRID_W = 64
EPS = 1e-6
D_A = 512
EMB_DIM = 33
DECAY_TARGET = 1e-2
MIN_DECAY = math.log(DECAY_TARGET) / 1.5
MAX_DECAY = math.log(DECAY_TARGET) / 0.3
N_HEADS = 8
Q_RANK = 256
KV_RANK = 256
NOPE_DIM = 64
ROPE_DIM = 32
V_DIM = 64
ROPE_BASE = 10000.0
D_B = N_HEADS * V_DIM
QK_DIM = NOPE_DIM + ROPE_DIM
ATTN_SCALE = 1.0 / math.sqrt(QK_DIM)
LOG2_E = math.log2(math.e)
D_C = 1024
N_GROUPS_C = 8
G_C = D_C // N_GROUPS_C

LANES = 128
HEAD_PAD = 128
VMEM_LIMIT = 56 * 1024 * 1024

_C_UA, _C_GA, _C_CQ, _C_CKV, _C_GB, _C_KRA, _C_KRB, _C_END = (
    0, 1536, 2048, 2304, 2560, 3072, 3200, 3328)


def _params(sem):
    return pltpu.CompilerParams(dimension_semantics=sem, vmem_limit_bytes=VMEM_LIMIT)


def _rms(x):
    return x * lax.rsqrt(jnp.mean(x * x, axis=-1, keepdims=True) + EPS)


def _silu(g):
    return g * jax.nn.sigmoid(g)


def _dot(a, b):
    return jnp.dot(a, b, preferred_element_type=F32)


def _ada_kernel(cond_ref, w_ref, b_ref, o_ref):
    c = _silu(cond_ref[...])
    o_ref[...] = jnp.dot(c, w_ref[...], preferred_element_type=F32,
                         precision=lax.Precision.HIGHEST) + b_ref[...]


def _ada_params(cond, w_mod, b_mod):
    depth = w_mod.shape[0]
    n = cond.shape[0]
    tn = 1024
    return pl.pallas_call(
        _ada_kernel,
        grid=(depth, 3 * D_MODEL // tn),
        in_specs=[pl.BlockSpec((n, D_MODEL), lambda i, j: (0, 0)),
                  pl.BlockSpec((None, D_MODEL, tn), lambda i, j: (i, 0, j)),
                  pl.BlockSpec((None, 1, tn), lambda i, j: (i, 0, j))],
        out_specs=pl.BlockSpec((None, n, tn), lambda i, j: (i, 0, j)),
        out_shape=jax.ShapeDtypeStruct((depth, n, 3 * D_MODEL), F32),
        compiler_params=_params(("arbitrary", "arbitrary")),
        name="ada_params",
    )(cond, w_mod, b_mod.reshape(depth, 1, 3 * D_MODEL))


def _inproj_even_kernel(x_ref, sc_ref, sh_ref, w_ref, qg_ref, kvg_ref, cos_ref, sin_ref,
                        wk_ref, wv_ref, vone_ref, *refs, nctx, emit_state):
    if nctx:
        cckv_ref, ckr_ref = refs[:2]
        refs = refs[2:]
    ua_ref, ga_ref, cq_ref, gb_ref, k_ref, v_ref = refs[:6]
    state_refs = refs[6:]

    def tokens():
        h = (_rms(x_ref[...]) * sc_ref[...] + sh_ref[...]).astype(BF16)

        def proj(a, b):
            return _dot(h, w_ref[:, a:b])

        ua_ref[...] = proj(_C_UA, _C_GA)
        ga_ref[...] = proj(_C_GA, _C_CQ)
        cq_ref[...] = (_rms(proj(_C_CQ, _C_CKV)) * qg_ref[...]).astype(BF16)
        gb_ref[...] = proj(_C_GB, _C_KRA)
        ckv = _rms(proj(_C_CKV, _C_GB)) * kvg_ref[...]
        kra = proj(_C_KRA, _C_KRB)
        krb = proj(_C_KRB, _C_END)
        c = ckv.astype(BF16)
        kr = (kra * cos_ref[...] + krb * sin_ref[...]).astype(BF16)
        k_ref[...] = _dot(jnp.concatenate([c, kr], axis=-1), wk_ref[...]).astype(BF16)
        v_ref[...] = (_dot(c, wv_ref[...]) + vone_ref[...]).astype(BF16)
        if emit_state:
            state_refs[0][...] = ckv
            state_refs[1][...] = kra

    if not nctx:
        tokens()
        return

    t = pl.program_id(1)

    @pl.when(t < nctx)
    def _():
        c = cckv_ref[...].astype(BF16)
        kr = ckr_ref[...].astype(BF16)
        k_ref[...] = (_dot(c, wk_ref[0:KV_RANK, :])
                      + _dot(kr, wk_ref[KV_RANK:KV_RANK + ROPE_DIM, :])).astype(BF16)
        v_ref[...] = (_dot(c, wv_ref[...]) + vone_ref[...]).astype(BF16)

    pl.when(t >= nctx)(tokens)


def _inproj_even(x2d, sc, sh, w, qg, kvg, cos_k, sin_k, wk, wv, vone, tm, emit_state,
                 cache=None):
    t = x2d.shape[0]
    nb = sc.shape[0]
    tiles = t // nb // tm
    tiles_per_seq = cos_k.shape[0] // tm
    nctx = 0 if cache is None else cache[0].shape[2] // tm
    tok = lambda b, s: b * tiles + jnp.maximum(s - nctx, 0)
    row = lambda b, s: (tok(b, s), 0)
    fixed = lambda b, s: (0, 0)
    mod = lambda b, s: (b, 0, 0)
    pos = lambda b, s: (jnp.maximum(s - nctx, 0) % tiles_per_seq, 0)
    kvo = lambda b, s: (b, s, 0)
    widths = (3 * D_A, D_A, Q_RANK, D_B)
    dtypes = (F32, F32, BF16, F32)
    in_specs = [pl.BlockSpec((tm, D_MODEL), row),
                pl.BlockSpec((None, 1, D_MODEL), mod),
                pl.BlockSpec((None, 1, D_MODEL), mod),
                pl.BlockSpec((D_MODEL, _C_END), fixed),
                pl.BlockSpec((1, Q_RANK), fixed),
                pl.BlockSpec((1, KV_RANK), fixed),
                pl.BlockSpec((tm, LANES), pos),
                pl.BlockSpec((tm, LANES), pos),
                pl.BlockSpec((KV_RANK + LANES, N_HEADS * HEAD_PAD), fixed),
                pl.BlockSpec((KV_RANK, N_HEADS * HEAD_PAD), fixed),
                pl.BlockSpec((1, N_HEADS * HEAD_PAD), fixed)]
    args = [x2d, sc, sh, w, qg, kvg, cos_k, sin_k, wk, wv, vone]
    if nctx:
        layer = cache[2]
        ctx = lambda b, s: (b, layer, jnp.minimum(s, nctx - 1), 0)
        in_specs += [pl.BlockSpec((None, None, tm, KV_RANK), ctx),
                     pl.BlockSpec((None, None, tm, ROPE_DIM), ctx)]
        args += [cache[0], cache[1]]
    out_specs = [pl.BlockSpec((tm, wd), row) for wd in widths]
    out_shape = [jax.ShapeDtypeStruct((t, wd), dt) for wd, dt in zip(widths, dtypes)]
    for wd in (N_HEADS * HEAD_PAD, N_HEADS * HEAD_PAD):
        out_specs.append(pl.BlockSpec((None, tm, wd), kvo))
        out_shape.append(jax.ShapeDtypeStruct((nb, (nctx + tiles) * tm, wd), BF16))
    if emit_state:
        for wd in (KV_RANK, LANES):
            out_specs.append(pl.BlockSpec((tm, wd), row))
            out_shape.append(jax.ShapeDtypeStruct((t, wd), F32))
    return pl.pallas_call(
        functools.partial(_inproj_even_kernel, nctx=nctx, emit_state=emit_state),
        grid=(nb, nctx + tiles),
        in_specs=in_specs,
        out_specs=out_specs,
        out_shape=out_shape,
        compiler_params=_params(("parallel", "arbitrary")),
        name="inproj_even",
    )(*args)


def _acc(acc, coef, x):
    if x is None or abs(coef) < 1e-12:
        return acc
    if abs(abs(coef) - 1.0) < 1e-12:
        if acc is None:
            return x if coef > 0 else -x
        return acc + x if coef > 0 else acc - x
    return coef * x if acc is None else acc + coef * x


def _add(a, b):
    if a is None:
        return b
    return a if b is None else a + b


def _sub(a, b):
    if b is None:
        return a
    return -b if a is None else a - b


def _cmul(ar, ai, br, bi):
    def mul(x, y):
        return None if x is None or y is None else x * y
    return _sub(mul(ar, br), mul(ai, bi)), _add(mul(ar, bi), mul(ai, br))


def _radix_parts(h):
    singles = [0] + ([h // 2] if h >= 2 and h % 2 == 0 else [])
    pairs = [(n, h - n) for n in range(1, (h + 1) // 2)]
    return singles, pairs


def _radix_fwd(k1, h, single, pair_sum, pair_diff):
    singles, pairs = _radix_parts(h)
    ar = ai = None
    for n in singles:
        th = 2.0 * math.pi * n * k1 / (2 * h)
        x = single(n)
        ar = _acc(ar, math.cos(th), x)
        ai = _acc(ai, -math.sin(th), x)
    for i, (n, _) in enumerate(pairs):
        th = 2.0 * math.pi * n * k1 / (2 * h)
        re_src, im_src = (pair_sum, pair_diff) if k1 % 2 == 0 else (pair_diff, pair_sum)
        ar = _acc(ar, math.cos(th), re_src(i))
        ai = _acc(ai, -math.sin(th), im_src(i))
    return ar, ai


def _radix_inv(h, n_total, bpr, bpi, emit):
    singles, pairs = _radix_parts(h)

    def coefs(n, k1):
        th = 2.0 * math.pi * n * k1 / (2 * h)
        w = (1.0 if k1 in (0, h) else 2.0) / n_total
        return w * math.cos(th), w * math.sin(th)

    for n in singles:
        y = None
        for k1 in range(h + 1):
            c, s = coefs(n, k1)
            y = _acc(y, c, bpr(k1) if abs(c) > 1e-12 else None)
            y = _acc(y, -s, bpi(k1) if abs(s) > 1e-12 else None)
        emit(n, y)
    for n, m in pairs:
        acc = {(0, "c"): None, (0, "s"): None, (1, "c"): None, (1, "s"): None}
        for k1 in range(h + 1):
            c, s = coefs(n, k1)
            par = k1 % 2
            acc[(par, "c")] = _acc(acc[(par, "c")], c, bpr(k1) if abs(c) > 1e-12 else None)
            acc[(par, "s")] = _acc(acc[(par, "s")], s, bpi(k1) if abs(s) > 1e-12 else None)
        even_m, even_p = _sub(acc[(0, "c")], acc[(0, "s")]), _add(acc[(0, "c")], acc[(0, "s")])
        odd_m, odd_p = _sub(acc[(1, "c")], acc[(1, "s")]), _add(acc[(1, "c")], acc[(1, "s")])
        emit(n, _add(even_m, odd_m))
        emit(m, _sub(even_p, odd_p))


def _next_twiddle(tw, tw1_ref):
    if tw is None:
        return tw1_ref[0], tw1_ref[1]
    return _cmul(tw[0], tw[1], tw1_ref[0], tw1_ref[1])


def _spectrum_kernel(hf_ref, hb_ref, inv_ref, tw1_ref, fh_ref, kf_ref, sd_scr,
                     *, h, n2, nk_pad):
    ct = hf_ref.shape[-1]
    row = lax.broadcasted_iota(jnp.int32, (n2, ct), 0)
    _, pairs = _radix_parts(h)

    def hb(n):
        return jnp.where(row == 0, 0.0, hb_ref[0]) if n == 0 else hb_ref[n]

    for i, (n, m) in enumerate(pairs):
        sd_scr[0, 0, i] = hf_ref[n] + hf_ref[m]
        sd_scr[0, 1, i] = hf_ref[n] - hf_ref[m]
        sd_scr[1, 0, i] = hb(n) + hb(m)
        sd_scr[1, 1, i] = hb(n) - hb(m)

    inv = inv_ref[...]
    tw = None
    for k1 in range(h + 1):
        far, fai = _radix_fwd(k1, h, lambda n: hf_ref[n],
                              lambda i: sd_scr[0, 0, i], lambda i: sd_scr[0, 1, i])
        bar, bai = _radix_fwd(k1, h, hb, lambda i: sd_scr[1, 0, i], lambda i: sd_scr[1, 1, i])
        if k1 > 0:
            tw = _next_twiddle(tw, tw1_ref)
            far, fai = _cmul(far, fai, tw[0], tw[1])
            bar, bai = _cmul(bar, bai, tw[0], tw[1])
        zero = jnp.zeros((n2, ct), F32)
        af = jnp.concatenate([zero if far is None else far, zero if fai is None else fai], axis=0)
        ab = jnp.concatenate([zero if bar is None else bar, zero if bai is None else bai], axis=0)
        a = jnp.concatenate([af, ab], axis=1)
        x = _dot(fh_ref[...], a.astype(BF16))
        lanes = slice(k1 * ct, (k1 + 1) * ct)
        kf_ref[0, :, lanes] = (x[:n2, :ct] + x[:n2, ct:]) * inv
        kf_ref[1, :, lanes] = (x[n2:, :ct] - x[n2:, ct:]) * inv
    if nk_pad > h + 1:
        pad = slice((h + 1) * ct, nk_pad * ct)
        kf_ref[:, :, pad] = jnp.zeros((2, n2, (nk_pad - h - 1) * ct), F32)


def _hyena_kernel(x0_ref, x1_ref, v_ref, ga_ref, *refs, **plan):
    o_ref = refs[10]

    def body(sq, carry):
        _hyena_sequence(x0_ref.at[sq], x1_ref.at[sq], v_ref.at[sq], ga_ref.at[sq], *refs[:10],
                        o_ref.at[sq], *refs[11:], **plan)
        return carry

    if o_ref.shape[0] == 1:
        body(0, 0)
    else:
        lax.fori_loop(0, o_ref.shape[0], body, 0)


def _hyena_sequence(x0_ref, x1_ref, v_ref, ga_ref,
                    w0_ref, w1_ref, wv_ref, b0_ref, b1_ref, bv_ref, skip_ref,
                    kf_ref, tw1_ref, f_ref,
                    o_ref, v_scr, sd_scr, a_scr, b_scr, *, seq, nslab, n2, nk_pad):
    ct = o_ref.shape[-1]
    h = nslab
    chunk = 2 * LANES
    kpc = chunk // ct
    row = lax.broadcasted_iota(jnp.int32, (n2, ct), 0)

    def short_conv(ref, w_ref, b_ref, s, start):
        u = ref[pl.ds(start, n2), :]
        prev = ref[pl.ds(jnp.maximum(start - 1, 0), 1), :]
        prev = jnp.where(s == 0, 0.0, prev)
        nxt = ref[pl.ds(jnp.minimum(start + n2, seq - 1), 1), :]
        nxt = jnp.where(s == nslab - 1, 0.0, nxt)
        up = jnp.where(row == 0, prev, pltpu.roll(u, 1, 0))
        un = jnp.where(row == n2 - 1, nxt, pltpu.roll(u, n2 - 1, 0))
        return b_ref[...] + up * w_ref[0:1, :] + u * w_ref[1:2, :] + un * w_ref[2:3, :]

    def prep(s, carry):
        start = pl.multiple_of(s * n2, n2)
        x0 = short_conv(x0_ref, w0_ref, b0_ref, s, start)
        x1 = short_conv(x1_ref, w1_ref, b1_ref, s, start)
        v_scr[s] = short_conv(v_ref, wv_ref, bv_ref, s, start) * x1
        o_ref[pl.ds(start, n2), :] = x0 * _silu(ga_ref[pl.ds(start, n2), :])
        return carry

    lax.fori_loop(0, nslab, prep, 0)

    _, pairs = _radix_parts(h)
    for i, (n, m) in enumerate(pairs):
        sd_scr[0, i] = v_scr[n] + v_scr[m]
        sd_scr[1, i] = v_scr[n] - v_scr[m]
    zero = jnp.zeros((n2, ct), BF16)
    tw = None
    for k1 in range(h + 1):
        ar, ai = _radix_fwd(k1, h, lambda n: v_scr[n], lambda i: sd_scr[0, i],
                            lambda i: sd_scr[1, i])
        if k1 > 0:
            tw = _next_twiddle(tw, tw1_ref)
            ar, ai = _cmul(ar, ai, tw[0], tw[1])
        lanes = slice(k1 * ct, (k1 + 1) * ct)
        a_scr[0:n2, lanes] = zero if ar is None else ar.astype(BF16)
        a_scr[n2:2 * n2, lanes] = zero if ai is None else ai.astype(BF16)
    for k1 in range(h + 1, nk_pad):
        lanes = slice(k1 * ct, (k1 + 1) * ct)
        a_scr[0:n2, lanes] = zero
        a_scr[n2:2 * n2, lanes] = zero

    def dft(j):
        return _dot(f_ref[...], a_scr[:, j * chunk:(j + 1) * chunk])

    tw = None
    nchunk = nk_pad // kpc
    x_next = dft(0)
    for j in range(nchunk):
        cols = slice(j * chunk, (j + 1) * chunk)
        x = x_next
        if j + 1 < nchunk:
            x_next = dft(j + 1)
        xr, xi = x[:n2], x[n2:]
        kr = kf_ref[0, :, cols]
        ki = kf_ref[1, :, cols]
        z = jnp.concatenate([xr * kr - xi * ki, -(xr * ki + xi * kr)], axis=0)
        bt = _dot(f_ref[...], z.astype(BF16))
        for kk in range(kpc):
            k1 = j * kpc + kk
            if k1 > h:
                continue
            lanes = slice(kk * ct, (kk + 1) * ct)
            br = bt[:n2, lanes]
            bi = -bt[n2:, lanes]
            if k1 > 0:
                tw = _next_twiddle(tw, tw1_ref)
                br, bi = _cmul(br, bi, tw[0], -tw[1])
            b_scr[0, k1] = br
            if k1 not in (0, h):
                b_scr[1, k1] = bi

    def emit(n, y):
        rows = pl.ds(n * n2, n2)
        o_ref[rows, :] = (y + v_scr[n] * skip_ref[...]) * o_ref[rows, :]

    _radix_inv(h, 2 * seq, lambda k1: b_scr[0, k1], lambda k1: b_scr[1, k1], emit)


def _dft_cos_sin(n2, shift=0):
    idx = np.arange(n2)
    ang = 2.0 * np.pi * (((idx[:, None] + shift) * idx[None, :]) % n2) / n2
    return np.cos(ang), np.sin(ang)


def _bf16_split(x):
    x32 = jnp.asarray(x.astype(np.float32))
    hi = x32.astype(BF16)
    lo = (x32 - hi.astype(F32)).astype(BF16)
    return hi, lo


def _dft_block(n2):
    c, s = _dft_cos_sin(n2)
    return np.block([[c, s], [-s, c]])


def _hyena_plan(seq):
    n2 = min(seq, 512)
    n1 = 2 * seq // n2
    ct = LANES if seq > 512 else 2 * LANES
    kpc = 2 * LANES // ct
    nk_pad = -(-(n1 // 2 + 1) // kpc) * kpc
    return n1, n2, ct, nk_pad


def _twiddle1(n1, n2, ct):
    ang = 2.0 * np.pi * np.arange(n2) / (n1 * n2)
    tw = np.stack([np.cos(ang), -np.sin(ang)]).astype(np.float32)
    return jnp.broadcast_to(jnp.asarray(tw)[..., None], tw.shape + (ct,))


def _spectrum(h, inv_norm, seq):
    n1, n2, ct, nk_pad = _hyena_plan(seq)
    nslab = n1 // 2
    nct = D_A // ct
    npair = len(_radix_parts(nslab)[1])
    ffwd = _bf16_split(_dft_block(n2))[0]
    h3 = h.reshape(nslab, n2, 2 * D_A)
    fixed2 = lambda c: (0, 0)
    return pl.pallas_call(
        functools.partial(_spectrum_kernel, h=nslab, n2=n2, nk_pad=nk_pad),
        grid=(nct,),
        in_specs=[pl.BlockSpec((nslab, n2, ct), lambda c: (0, 0, c)),
                  pl.BlockSpec((nslab, n2, ct), lambda c: (0, 0, nct + c)),
                  pl.BlockSpec((1, ct), lambda c: (0, c)),
                  pl.BlockSpec((2, n2, ct), lambda c: (0, 0, 0)),
                  pl.BlockSpec((2 * n2, 2 * n2), fixed2)],
        out_specs=pl.BlockSpec((None, 2, n2, nk_pad * ct), lambda c: (c, 0, 0, 0)),
        out_shape=jax.ShapeDtypeStruct((nct, 2, n2, nk_pad * ct), F32),
        scratch_shapes=[pltpu.VMEM((2, 2, max(npair, 1), n2, ct), F32)],
        compiler_params=_params(("parallel",)),
        name="filter_spectrum",
    )(h3, h3, inv_norm, _twiddle1(n1, n2, ct), ffwd)


def _hyena(ua, ga, conv_w, conv_b, skip, kf):
    bsz, seq, _ = ua.shape
    n1, n2, ct, nk_pad = _hyena_plan(seq)
    nslab = n1 // 2
    nct = D_A // ct
    npair = len(_radix_parts(nslab)[1])
    cb = conv_b.reshape(1, 3 * D_A)
    sk = skip.reshape(1, D_A)
    nseq = max(1, 1024 // seq)
    part = lambda p: pl.BlockSpec((nseq, seq, ct), lambda c, b: (b, 0, p * nct + c))
    wpart = lambda p: pl.BlockSpec((3, ct), lambda c, b: (0, p * nct + c))
    bpart = lambda p: pl.BlockSpec((1, ct), lambda c, b: (0, p * nct + c))
    return pl.pallas_call(
        functools.partial(_hyena_kernel, seq=seq, nslab=nslab, n2=n2, nk_pad=nk_pad),
        grid=(nct, bsz // nseq),
        in_specs=[part(0), part(1), part(2),
                  pl.BlockSpec((nseq, seq, ct), lambda c, b: (b, 0, c)),
                  wpart(0), wpart(1), wpart(2), bpart(0), bpart(1), bpart(2),
                  pl.BlockSpec((1, ct), lambda c, b: (0, c)),
                  pl.BlockSpec((None, 2, n2, nk_pad * ct), lambda c, b: (c, 0, 0, 0)),
                  pl.BlockSpec((2, n2, ct), lambda c, b: (0, 0, 0)),
                  pl.BlockSpec((2 * n2, 2 * n2), lambda c, b: (0, 0))],
        out_specs=pl.BlockSpec((nseq, seq, ct), lambda c, b: (b, 0, c)),
        out_shape=jax.ShapeDtypeStruct((bsz, seq, D_A), F32),
        scratch_shapes=[pltpu.VMEM((nslab, n2, ct), F32),
                        pltpu.VMEM((2, max(npair, 1), n2, ct), F32),
                        pltpu.VMEM((2 * n2, nk_pad * ct), BF16),
                        pltpu.VMEM((2, nk_pad, n2, ct), F32)],
        compiler_params=_params(("parallel", "parallel")),
        name="hyena",
    )(ua, ua, ua, ga, conv_w, conv_w, conv_w, cb, cb, cb, sk, kf, _twiddle1(n1, n2, ct),
      _bf16_split(_dft_block(n2))[0])


def _hyena_filter_kernel(z_ref, w1_ref, b1_ref, w2_ref, b2_ref, w3_ref, del_ref,
                         o_ref, nrm_ref, *, seq):
    hp = lax.Precision.HIGHEST
    tm = o_ref.shape[0]
    i = pl.program_id(0)
    h = jnp.sin(jnp.dot(z_ref[...], w1_ref[...], precision=hp,
                        preferred_element_type=F32) + b1_ref[...])
    h = jnp.sin(jnp.dot(h, w2_ref[...], precision=hp, preferred_element_type=F32) + b2_ref[...])
    tcol = (i * tm + lax.broadcasted_iota(jnp.int32, (tm, 1), 0)).astype(F32) * (1.0 / (seq - 1))
    h = jnp.dot(h, w3_ref[...], precision=hp, preferred_element_type=F32) * jnp.exp(
        -tcol * del_ref[...])
    o_ref[...] = h

    @pl.when(i == 0)
    def _():
        nrm_ref[...] = jnp.zeros_like(nrm_ref)

    nrm_ref[...] += jnp.sum(jnp.abs(h), axis=0, keepdims=True)


def _hyena_filter(seq, w1, b1, w2, b2, w3):
    bands = (EMB_DIM - 1) // 2
    ang = (2.0 * np.pi * np.arange(seq)[:, None] / seq) * np.linspace(1e-4, bands - 1, bands)
    z = np.zeros((seq, LANES), np.float32)
    z[:, 0] = np.linspace(0.0, 1.0, seq)
    z[:, 1:1 + bands] = np.cos(ang)
    z[:, 1 + bands:EMB_DIM] = -np.sin(ang)
    deltas = np.abs(np.linspace(MIN_DECAY, MAX_DECAY, D_A)).astype(np.float32)
    deltas2 = np.concatenate([deltas, deltas])[None, :]
    w1p = jnp.pad(w1, ((0, LANES - EMB_DIM), (0, 0)))
    tm = min(seq, 512)
    fo = w1.shape[1]
    fixed = lambda i: (0, 0)
    h, nrm = pl.pallas_call(
        functools.partial(_hyena_filter_kernel, seq=seq),
        grid=(seq // tm,),
        in_specs=[pl.BlockSpec((tm, LANES), lambda i: (i, 0)),
                  pl.BlockSpec((LANES, fo), fixed),
                  pl.BlockSpec((1, fo), fixed),
                  pl.BlockSpec((fo, fo), fixed),
                  pl.BlockSpec((1, fo), fixed),
                  pl.BlockSpec((fo, 2 * D_A), fixed),
                  pl.BlockSpec((1, 2 * D_A), fixed)],
        out_specs=[pl.BlockSpec((tm, 2 * D_A), lambda i: (i, 0)),
                   pl.BlockSpec((1, 2 * D_A), fixed)],
        out_shape=[jax.ShapeDtypeStruct((seq, 2 * D_A), F32),
                   jax.ShapeDtypeStruct((1, 2 * D_A), F32)],
        compiler_params=_params(("arbitrary",)),
        name="hyena_filter",
    )(jnp.asarray(z), w1p, b1.reshape(1, fo), w2, b2.reshape(1, fo), w3, jnp.asarray(deltas2))
    return h, 1.0 / (nrm[:, :D_A] + nrm[:, D_A:])


def _attn_kernel(cq_ref, cos_ref, sin_ref, wa_ref, wb_ref, k_ref, v_ref, gb_ref, o_ref, *, nseq):
    nt = (((1,), (1,)), ((), ()))
    tq = cq_ref.shape[1]
    cos = jnp.concatenate([cos_ref[...]] * N_HEADS, axis=-1)
    sin = jnp.concatenate([sin_ref[...]] * N_HEADS, axis=-1)
    lane = lax.broadcasted_iota(jnp.int32, (tq, 2 * V_DIM), 1)

    def one_sequence(sq):
        cq = cq_ref[sq]
        q = (_dot(cq, wa_ref[...]) * cos + _dot(cq, wb_ref[...]) * sin) * (ATTN_SCALE * LOG2_E)
        q = q.astype(BF16)

        def scores(h):
            hsl = slice(h * HEAD_PAD, (h + 1) * HEAD_PAD)
            return lax.dot_general(q[:, hsl], k_ref[sq, :, hsl], nt, preferred_element_type=F32)

        look = 1 if k_ref.shape[1] > 2 * tq else N_HEADS - 1
        pending = [scores(h) for h in range(look)]
        outs = []
        for h in range(N_HEADS):
            if h + look < N_HEADS:
                pending.append(scores(h + look))
            sb = pending.pop(0).astype(BF16)
            p = jnp.exp2(sb - jnp.max(sb, axis=-1, keepdims=True))
            of = _dot(p, v_ref[sq, :, h * HEAD_PAD:(h + 1) * HEAD_PAD])
            outs.append(of / of[:, V_DIM:V_DIM + 1])
            if h % 2 == 1:
                vsl = slice((h // 2) * 2 * V_DIM, (h // 2 + 1) * 2 * V_DIM)
                o = jnp.where(lane < V_DIM, outs[h - 1], pltpu.roll(outs[h], V_DIM, 1))
                o_ref[sq, :, vsl] = (o * _silu(gb_ref[sq, :, vsl])).astype(BF16)

    if nseq == 1:
        one_sequence(0)
    else:
        def body(sq, carry):
            one_sequence(sq)
            return carry

        lax.fori_loop(0, nseq, body, 0)


def _attention(cq, cos, sin, wa, wb, k, v, gb, tq, nseq=1):
    bsz, lq, _ = cq.shape
    lk = k.shape[1]
    fixed = lambda b, i: (0, 0)
    qrow = lambda b, i: (b, i, 0)
    kv = lambda b, i: (b, 0, 0)
    pos = lambda b, i: (i, 0)
    return pl.pallas_call(
        functools.partial(_attn_kernel, nseq=nseq),
        grid=(bsz // nseq, lq // tq),
        in_specs=[pl.BlockSpec((nseq, tq, Q_RANK), qrow),
                  pl.BlockSpec((tq, LANES), pos),
                  pl.BlockSpec((tq, LANES), pos),
                  pl.BlockSpec((Q_RANK, N_HEADS * HEAD_PAD), fixed),
                  pl.BlockSpec((Q_RANK, N_HEADS * HEAD_PAD), fixed),
                  pl.BlockSpec((nseq, lk, N_HEADS * HEAD_PAD), kv),
                  pl.BlockSpec((nseq, lk, N_HEADS * HEAD_PAD), kv),
                  pl.BlockSpec((nseq, tq, D_B), qrow)],
        out_specs=pl.BlockSpec((nseq, tq, D_B), qrow),
        out_shape=jax.ShapeDtypeStruct((bsz, lq, D_B), BF16),
        compiler_params=_params(("parallel", "arbitrary")),
        name="mla_attention",
    )(cq, cos, sin, wa, wb, k, v, gb)


def _mid_kernel(x_ref, ya_ref, ob_ref, gt_ref, wo_ref, sc_ref, sh_ref, wi_ref,
                x1_ref, u_ref, g_ref):
    y = _dot(ya_ref[...].astype(BF16), wo_ref[0:D_A, :]) + _dot(ob_ref[...], wo_ref[D_A:, :])
    x1 = x_ref[...] + gt_ref[...] * y
    x1_ref[...] = x1
    h = (_rms(x1) * sc_ref[...] + sh_ref[...]).astype(BF16)
    u_ref[...] = _dot(h, wi_ref[:, :D_C]).astype(BF16)
    g_ref[...] = _dot(h, wi_ref[:, D_C:]).astype(BF16)


def _mid(x2d, ya, ob, gt, wo, sc, sh, wi, tiles_per_mod, tm):
    t = x2d.shape[0]
    row = lambda i: (i, 0)
    fixed = lambda i: (0, 0)
    mod = lambda i: (i // tiles_per_mod, 0, 0)
    return pl.pallas_call(
        _mid_kernel,
        grid=(t // tm,),
        in_specs=[pl.BlockSpec((tm, D_MODEL), row),
                  pl.BlockSpec((tm, D_A), row),
                  pl.BlockSpec((tm, D_B), row),
                  pl.BlockSpec((None, 1, D_MODEL), mod),
                  pl.BlockSpec((D_A + D_B, D_MODEL), fixed),
                  pl.BlockSpec((None, 1, D_MODEL), mod),
                  pl.BlockSpec((None, 1, D_MODEL), mod),
                  pl.BlockSpec((D_MODEL, 2 * D_C), fixed)],
        out_specs=[pl.BlockSpec((tm, D_MODEL), row),
                   pl.BlockSpec((tm, D_C), row),
                   pl.BlockSpec((tm, D_C), row)],
        out_shape=[jax.ShapeDtypeStruct((t, D_MODEL), F32),
                   jax.ShapeDtypeStruct((t, D_C), BF16),
                   jax.ShapeDtypeStruct((t, D_C), BF16)],
        compiler_params=_params(("parallel",)),
        name="outproj_even_inproj_odd",
    )(x2d, ya, ob, gt, wo, sc, sh, wi)


def _fnet_weights_kernel(w_ref, cs_ref, o_ref):
    hp = lax.Precision.HIGHEST
    o_ref[...] = jnp.dot(cs_ref[...], w_ref[...], precision=hp,
                         preferred_element_type=F32).astype(BF16)


def _fnet_weights(fnet_w, seq):
    c, s = _dft_cos_sin(G_C)
    cs = jnp.asarray((np.concatenate([c, s], axis=0) / math.sqrt(seq * G_C)).astype(np.float32))
    return pl.pallas_call(
        _fnet_weights_kernel,
        grid=(N_GROUPS_C,),
        in_specs=[pl.BlockSpec((None, G_C, G_C), lambda g: (g, 0, 0)),
                  pl.BlockSpec((2 * G_C, G_C), lambda g: (0, 0))],
        out_specs=pl.BlockSpec((None, 2 * G_C, G_C), lambda g: (g, 0, 0)),
        out_shape=jax.ShapeDtypeStruct((N_GROUPS_C, 2 * G_C, G_C), BF16),
        compiler_params=_params(("parallel",)),
        name="fnet_weights",
    )(fnet_w, cs)


def _fnet_long_kernel(u_ref, tw1_ref, m_ref, ab_ref, o_ref, sd_scr, a_scr, *, n1, n2, nk_pad):
    h = n1 // 2
    ct = o_ref.shape[-1]
    chunk = 2 * LANES
    kpc = chunk // ct
    for i, n in enumerate(range(1, h)):
        a = u_ref[n].astype(F32)
        b = u_ref[n1 - n].astype(F32)
        sd_scr[0, i] = a + b
        sd_scr[1, i] = a - b
    x0 = u_ref[0].astype(F32)
    xh = u_ref[h].astype(F32)
    zero = jnp.zeros((n2, ct), BF16)
    tw = None
    for k1 in range(h + 1):
        ar = x0 + xh if k1 % 2 == 0 else x0 - xh
        ai = None
        for i, n in enumerate(range(1, h)):
            th = 2.0 * math.pi * n * k1 / n1
            ar = _acc(ar, math.cos(th), sd_scr[0, i])
            ai = _acc(ai, -math.sin(th), sd_scr[1, i])
        if k1 > 0:
            tw = _next_twiddle(tw, tw1_ref)
            ar, ai = _cmul(ar, ai, tw[0], tw[1])
        lanes = slice(k1 * ct, (k1 + 1) * ct)
        a_scr[0:n2, lanes] = ar.astype(BF16)
        a_scr[n2:2 * n2, lanes] = zero if ai is None else ai.astype(BF16)
    for k1 in range(h + 1, nk_pad):
        lanes = slice(k1 * ct, (k1 + 1) * ct)
        a_scr[0:n2, lanes] = zero
        a_scr[n2:2 * n2, lanes] = zero

    def group_map(xr, xi):
        y = _dot(jnp.concatenate([xr, xi], axis=1).astype(BF16), ab_ref[...])
        return y.astype(o_ref.dtype)

    def dft(j):
        return _dot(m_ref[...], a_scr[:, j * chunk:(j + 1) * chunk])

    nchunk = nk_pad // kpc
    x_next = dft(0)
    for j in range(nchunk):
        x = x_next
        if j + 1 < nchunk:
            x_next = dft(j + 1)
        for kk in range(kpc):
            k1 = j * kpc + kk
            if k1 > h:
                continue
            lanes = slice(kk * ct, (kk + 1) * ct)
            o_ref[k1] = group_map(x[:n2, lanes], x[n2:2 * n2, lanes])
            if 0 < k1 < h:
                o_ref[n1 - k1] = group_map(x[2 * n2:3 * n2, lanes], x[3 * n2:, lanes])


def _fnet_long(u, ab, n1, n2):
    bsz, seq, _ = u.shape
    h = n1 // 2
    kpc = 2 * LANES // G_C
    nk_pad = -(-(h + 1) // kpc) * kpc
    c, s = _dft_cos_sin(n2)
    ce, se = _dft_cos_sin(n2, shift=1)
    m = _bf16_split(np.block([[c, s], [-s, c], [ce, -se], [-se, -ce]]))[0]
    out = pl.pallas_call(
        functools.partial(_fnet_long_kernel, n1=n1, n2=n2, nk_pad=nk_pad),
        grid=(bsz, N_GROUPS_C),
        in_specs=[pl.BlockSpec((None, n1, n2, G_C), lambda b, g: (b, 0, 0, g)),
                  pl.BlockSpec((2, n2, G_C), lambda b, g: (0, 0, 0)),
                  pl.BlockSpec((4 * n2, 2 * n2), lambda b, g: (0, 0)),
                  pl.BlockSpec((None, 2 * G_C, G_C), lambda b, g: (g, 0, 0))],
        out_specs=pl.BlockSpec((None, n1, n2, G_C), lambda b, g: (b, 0, 0, g)),
        out_shape=jax.ShapeDtypeStruct((bsz, n1, n2, D_C), BF16),
        scratch_shapes=[pltpu.VMEM((2, h - 1, n2, G_C), F32),
                        pltpu.VMEM((2 * n2, nk_pad * G_C), BF16)],
        compiler_params=_params(("parallel", "parallel")),
        name="fnet_long",
    )(u.reshape(bsz, n1, n2, D_C), _twiddle1(n1, n2, G_C), m, ab)
    return out.transpose(0, 2, 1, 3).reshape(bsz, seq, D_C)


def _fnet_short_kernel(u_ref, f_ref, ab_ref, o_ref, *, seq, scale):
    def body(sq, carry):
        x = _dot(f_ref[...], u_ref[sq])
        xr, xi = x[:seq], x[seq:]
        for g in range(N_GROUPS_C):
            sl = slice(g * G_C, (g + 1) * G_C)
            xin = jnp.concatenate([xr[:, sl], xi[:, sl]], axis=1).astype(BF16)
            o_ref[sq, :, sl] = (_dot(xin, ab_ref[g]) * scale).astype(o_ref.dtype)
        return carry

    lax.fori_loop(0, u_ref.shape[0], body, 0)


def _fnet_short(u, ab, scale, nseq=4):
    bsz, seq, _ = u.shape
    c, s = _dft_cos_sin(seq)
    f = _bf16_split(np.concatenate([c, -s], axis=0))[0]
    return pl.pallas_call(
        functools.partial(_fnet_short_kernel, seq=seq, scale=scale),
        grid=(bsz // nseq,),
        in_specs=[pl.BlockSpec((nseq, seq, D_C), lambda b: (b, 0, 0)),
                  pl.BlockSpec((2 * seq, seq), lambda b: (0, 0)),
                  pl.BlockSpec((N_GROUPS_C, 2 * G_C, G_C), lambda b: (0, 0, 0))],
        out_specs=pl.BlockSpec((nseq, seq, D_C), lambda b: (b, 0, 0)),
        out_shape=jax.ShapeDtypeStruct((bsz, seq, D_C), BF16),
        compiler_params=_params(("parallel",)),
        name="fnet_short",
    )(u, f, ab)


def _final_kernel(x1_ref, y_ref, g_ref, gt_ref, wo_ref, fg_ref, o_ref):
    z = (y_ref[...].astype(F32) * _silu(g_ref[...].astype(F32))).astype(BF16)
    x2 = x1_ref[...] + gt_ref[...] * _dot(z, wo_ref[...])
    o_ref[...] = _rms(x2) * fg_ref[...]


def _final(x1, y, g, gt, wo, fg, tiles_per_mod, tm):
    t = x1.shape[0]
    row = lambda i: (i, 0)
    fixed = lambda i: (0, 0)
    mod = lambda i: (i // tiles_per_mod, 0, 0)
    return pl.pallas_call(
        _final_kernel,
        grid=(t // tm,),
        in_specs=[pl.BlockSpec((tm, D_MODEL), row),
                  pl.BlockSpec((tm, D_C), row),
                  pl.BlockSpec((tm, D_C), row),
                  pl.BlockSpec((None, 1, D_MODEL), mod),
                  pl.BlockSpec((D_C, D_MODEL), fixed),
                  pl.BlockSpec((1, D_MODEL), fixed)],
        out_specs=pl.BlockSpec((tm, D_MODEL), row),
        out_shape=jax.ShapeDtypeStruct((t, D_MODEL), F32),
        compiler_params=_params(("parallel",)),
        name="outproj_odd_final",
    )(x1, y, g, gt, wo, fg)


def _rope_swap(w):
    nf = ROPE_DIM // 4
    w4 = w.reshape(w.shape[:-1] + (2, 2, nf))
    return jnp.stack([-w4[..., 1, :], w4[..., 0, :]], axis=-2).reshape(w.shape)


def _pack_w_in_even(w):
    ua_ga_cq_ckv = w[:, :4 * D_A + Q_RANK + KV_RANK]
    o = 4 * D_A + Q_RANK + KV_RANK
    krope = w[:, o:o + ROPE_DIM]
    gb = w[:, o + ROPE_DIM:]
    zpad = jnp.zeros((w.shape[0], LANES - ROPE_DIM), w.dtype)
    return jnp.concatenate([ua_ga_cq_ckv, gb, krope, zpad, _rope_swap(krope), zpad],
                           axis=1).astype(BF16)


def _pack_w_uq(w):
    w3 = w.reshape(Q_RANK, N_HEADS, QK_DIM)
    zq = jnp.zeros((Q_RANK, N_HEADS, HEAD_PAD - QK_DIM), w.dtype)
    wa = jnp.concatenate([w3, zq], axis=-1)
    zn = jnp.zeros((Q_RANK, N_HEADS, NOPE_DIM), w.dtype)
    wb = jnp.concatenate([zn, _rope_swap(w3[..., NOPE_DIM:]), zq], axis=-1)
    shape = (Q_RANK, N_HEADS * HEAD_PAD)
    return wa.reshape(shape).astype(BF16), wb.reshape(shape).astype(BF16)


def _pack_w_ukv(w):
    w3 = w.reshape(KV_RANK, N_HEADS, NOPE_DIM + V_DIM)
    zk = jnp.zeros((KV_RANK, N_HEADS, HEAD_PAD - NOPE_DIM), w.dtype)
    wk_c = jnp.concatenate([w3[..., :NOPE_DIM], zk], axis=-1).reshape(KV_RANK, -1)
    eye = jnp.eye(LANES, ROPE_DIM, dtype=w.dtype)
    rope_rows = jnp.concatenate([jnp.zeros((LANES, NOPE_DIM), w.dtype), eye,
                                 jnp.zeros((LANES, HEAD_PAD - QK_DIM), w.dtype)], axis=-1)
    wk_r = jnp.tile(rope_rows, (1, N_HEADS))
    wk = jnp.concatenate([wk_c, wk_r], axis=0)
    zv = jnp.zeros((KV_RANK, N_HEADS, HEAD_PAD - V_DIM), w.dtype)
    wv = jnp.concatenate([w3[..., NOPE_DIM:], zv], axis=-1).reshape(KV_RANK, -1)
    ones = np.zeros((1, N_HEADS, HEAD_PAD), np.float32)
    ones[:, :, V_DIM] = 1.0
    return wk.astype(BF16), wv.astype(BF16), jnp.asarray(ones.reshape(1, -1))


def _rope_tables(seq, rotate):
    pos = np.arange(seq)
    half = ROPE_DIM // 2
    inv = ROPE_BASE ** (-np.arange(0, half, 2, dtype=np.float64) / half)
    r = (pos // GRID_W)[:, None] * inv
    c = (pos % GRID_W)[:, None] * inv
    ang = np.concatenate([r, r, c, c], axis=-1) * (1.0 if rotate else 0.0)
    cos_q = np.zeros((seq, LANES))
    sin_q = np.zeros((seq, LANES))
    cos_k = np.zeros((seq, LANES))
    sin_k = np.zeros((seq, LANES))
    cos_q[:, :NOPE_DIM] = 1.0
    cos_q[:, NOPE_DIM:QK_DIM] = np.cos(ang)
    sin_q[:, NOPE_DIM:QK_DIM] = np.sin(ang)
    cos_k[:, :ROPE_DIM] = np.cos(ang)
    sin_k[:, :ROPE_DIM] = np.sin(ang)
    return tuple(jnp.asarray(t.astype(np.float32)) for t in (cos_q, sin_q, cos_k, sin_k))


def kernel(x_prompt, x_sample, c, cache_ckv, cache_krope, c_ctx, norm_g, w_mod, b_mod,
           w_in_e, conv_w, conv_b, filt_w1, filt_b1, filt_w2, filt_b2, filt_w3, hyena_skip,
           q_norm_g, kv_norm_g, w_uq, w_ukv, w_out_e, w_in_o, fnet_w, w_out_o, final_g):
    nb_p, seq_p, _ = x_prompt.shape
    nb_s, seq_s, _ = x_sample.shape
    tp = nb_p * seq_p
    ts = nb_s * seq_s
    tm = 512

    cond = jnp.concatenate([c_ctx[None, :], c, jnp.zeros((8 - 1 - nb_s, D_MODEL), F32)], axis=0)
    mod = _ada_params(cond, w_mod, b_mod)
    shift = mod[:, :, None, :D_MODEL]
    scale = (1.0 + mod[:, :, None, D_MODEL:2 * D_MODEL]) * norm_g[:, None, None, :]
    gate = mod[:, :, None, 2 * D_MODEL:]

    xp = x_prompt.reshape(tp, D_MODEL)
    xs = x_sample.reshape(ts, D_MODEL)

    w_in = _pack_w_in_even(w_in_e[0])
    qg = q_norm_g[0].reshape(1, Q_RANK)
    kvg = kv_norm_g[0].reshape(1, KV_RANK)
    wq_a, wq_b = _pack_w_uq(w_uq[0])
    wk, wv, vone = _pack_w_ukv(w_ukv[0])
    cos_qp, sin_qp, cos_kp, sin_kp = _rope_tables(tm, rotate=False)
    cos_qs, sin_qs, cos_ks, sin_ks = _rope_tables(seq_s, rotate=True)
    pr_p = _inproj_even(xp, scale[0, :1], shift[0, :1], w_in, qg, kvg, cos_kp, sin_kp, wk, wv,
                        vone, tm, True)
    pr_s = _inproj_even(xs, scale[0, 1:1 + nb_s], shift[0, 1:1 + nb_s], w_in, qg, kvg,
                        cos_ks, sin_ks, wk, wv, vone, tm, False,
                        cache=(cache_ckv, cache_krope, 0))
    ua_p, ga_p, cq_p, gb_p, k_p, v_p, ckv_p, kra_p = pr_p
    ua_s, ga_s, cq_s, gb_s, k_s, v_s = pr_s

    filt = (filt_w1[0], filt_b1[0], filt_w2[0], filt_b2[0], filt_w3[0])
    ya = []
    for ua, ga, nb, seq in ((ua_p, ga_p, nb_p, seq_p), (ua_s, ga_s, nb_s, seq_s)):
        h, inv_norm = _hyena_filter(seq, *filt)
        kf = _spectrum(h, inv_norm, seq)
        ya.append(_hyena(ua.reshape(nb, seq, 3 * D_A), ga.reshape(nb, seq, D_A),
                         conv_w[0], conv_b[0], hyena_skip[0], kf))
    ya_p = ya[0].reshape(tp, D_A)
    ya_s = ya[1].reshape(ts, D_A)

    ob_p = _attention(cq_p.reshape(nb_p, seq_p, Q_RANK), cos_qp, sin_qp, wq_a, wq_b,
                      k_p.reshape(nb_p, seq_p, -1), v_p.reshape(nb_p, seq_p, -1),
                      gb_p.reshape(nb_p, seq_p, D_B), seq_p, nseq=4)
    ob_s = _attention(cq_s.reshape(nb_s, seq_s, Q_RANK), cos_qs, sin_qs, wq_a, wq_b, k_s, v_s,
                      gb_s.reshape(nb_s, seq_s, D_B), 256)

    wo_e = w_out_e[0].astype(BF16)
    wi_o = w_in_o[0].astype(BF16)
    x1_p, u_p, g_p = _mid(xp, ya_p, ob_p.reshape(tp, D_B), gate[0, :1], wo_e,
                          scale[1, :1], shift[1, :1], wi_o, tp // tm, tm)
    x1_s, u_s, g_s = _mid(xs, ya_s, ob_s.reshape(ts, D_B), gate[0, 1:1 + nb_s], wo_e,
                          scale[1, 1:1 + nb_s], shift[1, 1:1 + nb_s], wi_o, seq_s // tm, tm)

    ab = _fnet_weights(fnet_w[0], seq_s)
    y_p = _fnet_short(u_p.reshape(nb_p, seq_p, D_C), ab, math.sqrt(seq_s / seq_p))
    y_s = _fnet_long(u_s.reshape(nb_s, seq_s, D_C), ab, 16, seq_s // 16)
    wo_o = w_out_o[0].astype(BF16)
    fg = final_g.reshape(1, D_MODEL)
    out_p = _final(x1_p, y_p.reshape(tp, D_C), g_p, gate[1, :1], wo_o, fg, tp // tm, tm)
    out_s = _final(x1_s, y_s.reshape(ts, D_C), g_s, gate[1, 1:1 + nb_s], wo_o, fg,
                   seq_s // tm, tm)

    state_ckv = ckv_p.reshape(nb_p, 1, seq_p, KV_RANK)
    state_krope = kra_p[:, :ROPE_DIM].reshape(nb_p, 1, seq_p, ROPE_DIM)
    return (out_p.reshape(nb_p, seq_p, D_MODEL), out_s.reshape(nb_s, seq_s, D_MODEL),
            state_ckv, state_krope)
```

```python
import functools
import math

import numpy as np
import jax
import jax.numpy as jnp
from jax import lax
from jax.experimental import pallas as pl
from jax.experimental.pallas import tpu as pltpu

F32 = jnp.float32
BF16 = jnp.bfloat16

D_MODEL = 1024
GRID_W = 64
EPS = 1e-6
D_A = 512
EMB_DIM = 33
DECAY_TARGET = 1e-2
MIN_DECAY = math.log(DECAY_TARGET) / 1.5
MAX_DECAY = math.log(DECAY_TARGET) / 0.3
N_HEADS = 8
Q_RANK = 256
KV_RANK = 256
NOPE_DIM = 64
ROPE_DIM = 32
V_DIM = 64
ROPE_BASE = 10000.0
D_B = N_HEADS * V_DIM
QK_DIM = NOPE_DIM + ROPE_DIM
ATTN_SCALE = 1.0 / math.sqrt(QK_DIM)
LOG2_E = math.log2(math.e)
D_C = 1024
N_GROUPS_C = 8
G_C = D_C // N_GROUPS_C

LANES = 128
HEAD_PAD = 128
VMEM_LIMIT = 56 * 1024 * 1024

_C_UA, _C_GA, _C_CQ, _C_CKV, _C_GB, _C_KRA, _C_KRB, _C_END = (
    0, 1536, 2048, 2304, 2560, 3072, 3200, 3328)


def _params(sem):
    return pltpu.CompilerParams(dimension_semantics=sem, vmem_limit_bytes=VMEM_LIMIT)


def _rms(x):
    return x * lax.rsqrt(jnp.mean(x * x, axis=-1, keepdims=True) + EPS)


def _silu(g):
    return g * jax.nn.sigmoid(g)


def _dot(a, b):
    return jnp.dot(a, b, preferred_element_type=F32)


def _ada_kernel(cond_ref, w_ref, b_ref, o_ref):
    c = _silu(cond_ref[...])
    o_ref[...] = jnp.dot(c, w_ref[...], preferred_element_type=F32,
                         precision=lax.Precision.HIGHEST) + b_ref[...]


def _ada_params(cond, w_mod, b_mod):
    depth = w_mod.shape[0]
    n = cond.shape[0]
    tn = 1024
    return pl.pallas_call(
        _ada_kernel,
        grid=(depth, 3 * D_MODEL // tn),
        in_specs=[pl.BlockSpec((n, D_MODEL), lambda i, j: (0, 0)),
                  pl.BlockSpec((None, D_MODEL, tn), lambda i, j: (i, 0, j)),
                  pl.BlockSpec((None, 1, tn), lambda i, j: (i, 0, j))],
        out_specs=pl.BlockSpec((None, n, tn), lambda i, j: (i, 0, j)),
        out_shape=jax.ShapeDtypeStruct((depth, n, 3 * D_MODEL), F32),
        compiler_params=_params(("arbitrary", "arbitrary")),
        name="ada_params",
    )(cond, w_mod, b_mod.reshape(depth, 1, 3 * D_MODEL))


def _inproj_even_kernel(x_ref, sc_ref, sh_ref, w_ref, qg_ref, kvg_ref, cos_ref, sin_ref,
                        wk_ref, wv_ref, vone_ref, *refs, nctx, emit_state):
    if nctx:
        cckv_ref, ckr_ref = refs[:2]
        refs = refs[2:]
    ua_ref, ga_ref, cq_ref, gb_ref, k_ref, v_ref = refs[:6]
    state_refs = refs[6:]

    def tokens():
        h = (_rms(x_ref[...]) * sc_ref[...] + sh_ref[...]).astype(BF16)

        def proj(a, b):
            return _dot(h, w_ref[:, a:b])

        ua_ref[...] = proj(_C_UA, _C_GA)
        ga_ref[...] = proj(_C_GA, _C_CQ)
        cq_ref[...] = (_rms(proj(_C_CQ, _C_CKV)) * qg_ref[...]).astype(BF16)
        gb_ref[...] = proj(_C_GB, _C_KRA)
        ckv = _rms(proj(_C_CKV, _C_GB)) * kvg_ref[...]
        kra = proj(_C_KRA, _C_KRB)
        krb = proj(_C_KRB, _C_END)
        c = ckv.astype(BF16)
        kr = pltpu.roll(kra * cos_ref[...] + krb * sin_ref[...], NOPE_DIM, 1)
        k = _dot(c, wk_ref[0:KV_RANK, :]) + jnp.concatenate([kr] * N_HEADS, axis=-1)
        k_ref[...] = k.astype(BF16)
        v_ref[...] = (_dot(c, wv_ref[...]) + vone_ref[...]).astype(BF16)
        if emit_state:
            state_refs[0][...] = ckv
            state_refs[1][...] = kra

    if not nctx:
        tokens()
        return

    t = pl.program_id(1)

    @pl.when(t < nctx)
    def _():
        c = cckv_ref[...].astype(BF16)
        kr = ckr_ref[...].astype(BF16)
        k_ref[...] = (_dot(c, wk_ref[0:KV_RANK, :])
                      + _dot(kr, wk_ref[KV_RANK:KV_RANK + ROPE_DIM, :])).astype(BF16)
        v_ref[...] = (_dot(c, wv_ref[...]) + vone_ref[...]).astype(BF16)

    pl.when(t >= nctx)(tokens)


def _inproj_even(x2d, sc, sh, w, qg, kvg, cos_k, sin_k, wk, wv, vone, tm, emit_state,
                 cache=None):
    t = x2d.shape[0]
    nb = sc.shape[0]
    tiles = t // nb // tm
    tiles_per_seq = cos_k.shape[0] // tm
    nctx = 0 if cache is None else cache[0].shape[2] // tm
    tok = lambda b, s: b * tiles + jnp.maximum(s - nctx, 0)
    row = lambda b, s: (tok(b, s), 0)
    fixed = lambda b, s: (0, 0)
    mod = lambda b, s: (b, 0, 0)
    pos = lambda b, s: (jnp.maximum(s - nctx, 0) % tiles_per_seq, 0)
    kvo = lambda b, s: (b, s, 0)
    widths = (3 * D_A, D_A, Q_RANK, D_B)
    dtypes = (F32, F32, BF16, F32)
    in_specs = [pl.BlockSpec((tm, D_MODEL), row),
                pl.BlockSpec((None, 1, D_MODEL), mod),
                pl.BlockSpec((None, 1, D_MODEL), mod),
                pl.BlockSpec((D_MODEL, _C_END), fixed),
                pl.BlockSpec((1, Q_RANK), fixed),
                pl.BlockSpec((1, KV_RANK), fixed),
                pl.BlockSpec((tm, LANES), pos),
                pl.BlockSpec((tm, LANES), pos),
                pl.BlockSpec((KV_RANK + LANES, N_HEADS * HEAD_PAD), fixed),
                pl.BlockSpec((KV_RANK, N_HEADS * HEAD_PAD), fixed),
                pl.BlockSpec((1, N_HEADS * HEAD_PAD), fixed)]
    args = [x2d, sc, sh, w, qg, kvg, cos_k, sin_k, wk, wv, vone]
    if nctx:
        layer = cache[2]
        ctx = lambda b, s: (b, layer, jnp.minimum(s, nctx - 1), 0)
        in_specs += [pl.BlockSpec((None, None, tm, KV_RANK), ctx),
                     pl.BlockSpec((None, None, tm, ROPE_DIM), ctx)]
        args += [cache[0], cache[1]]
    out_specs = [pl.BlockSpec((tm, wd), row) for wd in widths]
    out_shape = [jax.ShapeDtypeStruct((t, wd), dt) for wd, dt in zip(widths, dtypes)]
    for wd in (N_HEADS * HEAD_PAD, N_HEADS * HEAD_PAD):
        out_specs.append(pl.BlockSpec((None, tm, wd), kvo))
        out_shape.append(jax.ShapeDtypeStruct((nb, (nctx + tiles) * tm, wd), BF16))
    if emit_state:
        for wd in (KV_RANK, LANES):
            out_specs.append(pl.BlockSpec((tm, wd), row))
            out_shape.append(jax.ShapeDtypeStruct((t, wd), F32))
    return pl.pallas_call(
        functools.partial(_inproj_even_kernel, nctx=nctx, emit_state=emit_state),
        grid=(nb, nctx + tiles),
        in_specs=in_specs,
        out_specs=out_specs,
        out_shape=out_shape,
        compiler_params=_params(("parallel", "arbitrary")),
        name="inproj_even",
    )(*args)


def _acc(acc, coef, x):
    if x is None or abs(coef) < 1e-12:
        return acc
    if abs(abs(coef) - 1.0) < 1e-12:
        if acc is None:
            return x if coef > 0 else -x
        return acc + x if coef > 0 else acc - x
    return coef * x if acc is None else acc + coef * x


def _add(a, b):
    if a is None:
        return b
    return a if b is None else a + b


def _sub(a, b):
    if b is None:
        return a
    return -b if a is None else a - b


def _cmul(ar, ai, br, bi):
    def mul(x, y):
        return None if x is None or y is None else x * y
    return _sub(mul(ar, br), mul(ai, bi)), _add(mul(ar, bi), mul(ai, br))


def _radix_parts(h):
    singles = [0] + ([h // 2] if h >= 2 and h % 2 == 0 else [])
    pairs = [(n, h - n) for n in range(1, (h + 1) // 2)]
    return singles, pairs


def _radix_fwd(k1, h, single, pair_sum, pair_diff):
    singles, pairs = _radix_parts(h)
    ar = ai = None
    for n in singles:
        th = 2.0 * math.pi * n * k1 / (2 * h)
        x = single(n)
        ar = _acc(ar, math.cos(th), x)
        ai = _acc(ai, -math.sin(th), x)
    for i, (n, _) in enumerate(pairs):
        th = 2.0 * math.pi * n * k1 / (2 * h)
        re_src, im_src = (pair_sum, pair_diff) if k1 % 2 == 0 else (pair_diff, pair_sum)
        ar = _acc(ar, math.cos(th), re_src(i))
        ai = _acc(ai, -math.sin(th), im_src(i))
    return ar, ai


def _radix_inv(h, n_total, bpr, bpi, emit):
    singles, pairs = _radix_parts(h)

    def coefs(n, k1):
        th = 2.0 * math.pi * n * k1 / (2 * h)
        w = (1.0 if k1 in (0, h) else 2.0) / n_total
        return w * math.cos(th), w * math.sin(th)

    for n in singles:
        y = None
        for k1 in range(h + 1):
            c, s = coefs(n, k1)
            y = _acc(y, c, bpr(k1) if abs(c) > 1e-12 else None)
            y = _acc(y, -s, bpi(k1) if abs(s) > 1e-12 else None)
        emit(n, y)
    for n, m in pairs:
        acc = {(0, "c"): None, (0, "s"): None, (1, "c"): None, (1, "s"): None}
        for k1 in range(h + 1):
            c, s = coefs(n, k1)
            par = k1 % 2
            acc[(par, "c")] = _acc(acc[(par, "c")], c, bpr(k1) if abs(c) > 1e-12 else None)
            acc[(par, "s")] = _acc(acc[(par, "s")], s, bpi(k1) if abs(s) > 1e-12 else None)
        even_m, even_p = _sub(acc[(0, "c")], acc[(0, "s")]), _add(acc[(0, "c")], acc[(0, "s")])
        odd_m, odd_p = _sub(acc[(1, "c")], acc[(1, "s")]), _add(acc[(1, "c")], acc[(1, "s")])
        emit(n, _add(even_m, odd_m))
        emit(m, _sub(even_p, odd_p))


def _next_twiddle(tw, tw_ref):
    row = 0 if tw is None else tw[2] + 1
    return tw_ref[0, row], tw_ref[1, row], row


def _spectrum_kernel(hf_ref, hb_ref, inv_ref, tw1_ref, fh_ref, kf_ref, sd_scr,
                     *, h, n2, nk_pad):
    ct = hf_ref.shape[-1]
    row = lax.broadcasted_iota(jnp.int32, (n2, ct), 0)
    _, pairs = _radix_parts(h)

    def hb(n):
        return jnp.where(row == 0, 0.0, hb_ref[0]) if n == 0 else hb_ref[n]

    for i, (n, m) in enumerate(pairs):
        sd_scr[0, 0, i] = hf_ref[n] + hf_ref[m]
        sd_scr[0, 1, i] = hf_ref[n] - hf_ref[m]
        sd_scr[1, 0, i] = hb(n) + hb(m)
        sd_scr[1, 1, i] = hb(n) - hb(m)

    inv = inv_ref[...]
    tw = None
    for k1 in range(h + 1):
        far, fai = _radix_fwd(k1, h, lambda n: hf_ref[n],
                              lambda i: sd_scr[0, 0, i], lambda i: sd_scr[0, 1, i])
        bar, bai = _radix_fwd(k1, h, hb, lambda i: sd_scr[1, 0, i], lambda i: sd_scr[1, 1, i])
        if k1 > 0:
            tw = _next_twiddle(tw, tw1_ref)
            far, fai = _cmul(far, fai, tw[0], tw[1])
            bar, bai = _cmul(bar, bai, tw[0], tw[1])
        zero = jnp.zeros((n2, ct), F32)
        af = jnp.concatenate([zero if far is None else far, zero if fai is None else fai], axis=0)
        ab = jnp.concatenate([zero if bar is None else bar, zero if bai is None else bai], axis=0)
        a = jnp.concatenate([af, ab], axis=1)
        x = _dot(fh_ref[...], a.astype(BF16))
        lanes = slice(k1 * ct, (k1 + 1) * ct)
        kf_ref[0, :, lanes] = (x[:n2, :ct] + x[:n2, ct:]) * inv
        kf_ref[1, :, lanes] = (x[n2:, :ct] - x[n2:, ct:]) * inv
    if nk_pad > h + 1:
        pad = slice((h + 1) * ct, nk_pad * ct)
        kf_ref[:, :, pad] = jnp.zeros((2, n2, (nk_pad - h - 1) * ct), F32)


def _hyena_kernel(x0_ref, x1_ref, v_ref, ga_ref, *refs, **plan):
    o_ref = refs[10]

    def body(sq, carry):
        _hyena_sequence(x0_ref.at[sq], x1_ref.at[sq], v_ref.at[sq], ga_ref.at[sq], *refs[:10],
                        o_ref.at[sq], *refs[11:], **plan)
        return carry

    if o_ref.shape[0] == 1:
        body(0, 0)
    else:
        lax.fori_loop(0, o_ref.shape[0], body, 0)


def _hyena_sequence(x0_ref, x1_ref, v_ref, ga_ref,
                    w0_ref, w1_ref, wv_ref, b0_ref, b1_ref, bv_ref, skip_ref,
                    kf_ref, tw1_ref, f_ref,
                    o_ref, v_scr, sd_scr, a_scr, b_scr, *, seq, nslab, n2, nk_pad):
    ct = o_ref.shape[-1]
    h = nslab
    chunk = 2 * LANES
    kpc = chunk // ct
    row = lax.broadcasted_iota(jnp.int32, (n2, ct), 0)

    def short_conv(ref, w_ref, b_ref, s, start):
        u = ref[pl.ds(start, n2), :]
        prev = ref[pl.ds(jnp.maximum(start - 1, 0), 1), :]
        prev = jnp.where(s == 0, 0.0, prev)
        nxt = ref[pl.ds(jnp.minimum(start + n2, seq - 1), 1), :]
        nxt = jnp.where(s == nslab - 1, 0.0, nxt)
        up = jnp.where(row == 0, prev, pltpu.roll(u, 1, 0))
        un = jnp.where(row == n2 - 1, nxt, pltpu.roll(u, n2 - 1, 0))
        return b_ref[...] + up * w_ref[0:1, :] + u * w_ref[1:2, :] + un * w_ref[2:3, :]

    def prep(s, carry):
        start = pl.multiple_of(s * n2, n2)
        x0 = short_conv(x0_ref, w0_ref, b0_ref, s, start)
        x1 = short_conv(x1_ref, w1_ref, b1_ref, s, start)
        v_scr[s] = short_conv(v_ref, wv_ref, bv_ref, s, start) * x1
        o_ref[pl.ds(start, n2), :] = x0 * _silu(ga_ref[pl.ds(start, n2), :])
        return carry

    lax.fori_loop(0, nslab, prep, 0)

    _, pairs = _radix_parts(h)
    for i, (n, m) in enumerate(pairs):
        sd_scr[0, i] = v_scr[n] + v_scr[m]
        sd_scr[1, i] = v_scr[n] - v_scr[m]
    zero = jnp.zeros((n2, ct), BF16)
    tw = None
    for k1 in range(h + 1):
        ar, ai = _radix_fwd(k1, h, lambda n: v_scr[n], lambda i: sd_scr[0, i],
                            lambda i: sd_scr[1, i])
        if k1 > 0:
            tw = _next_twiddle(tw, tw1_ref)
            ar, ai = _cmul(ar, ai, tw[0], tw[1])
        lanes = slice(k1 * ct, (k1 + 1) * ct)
        a_scr[0:n2, lanes] = zero if ar is None else ar.astype(BF16)
        a_scr[n2:2 * n2, lanes] = zero if ai is None else ai.astype(BF16)
    for k1 in range(h + 1, nk_pad):
        lanes = slice(k1 * ct, (k1 + 1) * ct)
        a_scr[0:n2, lanes] = zero
        a_scr[n2:2 * n2, lanes] = zero

    def dft(j):
        return _dot(f_ref[...], a_scr[:, j * chunk:(j + 1) * chunk])

    tw = None
    nchunk = nk_pad // kpc
    x_next = dft(0)
    for j in range(nchunk):
        cols = slice(j * chunk, (j + 1) * chunk)
        x = x_next
        if j + 1 < nchunk:
            x_next = dft(j + 1)
        xr, xi = x[:n2], x[n2:]
        kr = kf_ref[0, :, cols]
        ki = kf_ref[1, :, cols]
        z = jnp.concatenate([xr * kr - xi * ki, -(xr * ki + xi * kr)], axis=0)
        bt = _dot(f_ref[...], z.astype(BF16))
        for kk in range(kpc):
            k1 = j * kpc + kk
            if k1 > h:
                continue
            lanes = slice(kk * ct, (kk + 1) * ct)
            br = bt[:n2, lanes]
            bi = -bt[n2:, lanes]
            if k1 > 0:
                tw = _next_twiddle(tw, tw1_ref)
                br, bi = _cmul(br, bi, tw[0], -tw[1])
            b_scr[0, k1] = br
            if k1 not in (0, h):
                b_scr[1, k1] = bi

    def emit(n, y):
        rows = pl.ds(n * n2, n2)
        o_ref[rows, :] = (y + v_scr[n] * skip_ref[...]) * o_ref[rows, :]

    _radix_inv(h, 2 * seq, lambda k1: b_scr[0, k1], lambda k1: b_scr[1, k1], emit)


def _dft_cos_sin(n2, shift=0):
    idx = np.arange(n2)
    ang = 2.0 * np.pi * (((idx[:, None] + shift) * idx[None, :]) % n2) / n2
    return np.cos(ang), np.sin(ang)


def _bf16_split(x):
    x32 = jnp.asarray(x.astype(np.float32))
    hi = x32.astype(BF16)
    lo = (x32 - hi.astype(F32)).astype(BF16)
    return hi, lo


def _dft_block(n2):
    c, s = _dft_cos_sin(n2)
    return np.block([[c, s], [-s, c]])


def _hyena_plan(seq):
    n2 = min(seq, 512)
    n1 = 2 * seq // n2
    ct = LANES if seq > 512 else 2 * LANES
    kpc = 2 * LANES // ct
    nk_pad = -(-(n1 // 2 + 1) // kpc) * kpc
    return n1, n2, ct, nk_pad


def _twiddle1(n1, n2, ct):
    ang = 2.0 * np.pi * np.arange(1, n1 // 2 + 1)[:, None] * np.arange(n2)[None, :] / (n1 * n2)
    tw = np.stack([np.cos(ang), -np.sin(ang)]).astype(np.float32)
    return jnp.broadcast_to(jnp.asarray(tw)[..., None], tw.shape + (ct,))


def _spectrum(h, inv_norm, seq):
    n1, n2, ct, nk_pad = _hyena_plan(seq)
    nslab = n1 // 2
    nct = D_A // ct
    npair = len(_radix_parts(nslab)[1])
    ffwd = _bf16_split(_dft_block(n2))[0]
    h3 = h.reshape(nslab, n2, 2 * D_A)
    fixed2 = lambda c: (0, 0)
    return pl.pallas_call(
        functools.partial(_spectrum_kernel, h=nslab, n2=n2, nk_pad=nk_pad),
        grid=(nct,),
        in_specs=[pl.BlockSpec((nslab, n2, ct), lambda c: (0, 0, c)),
                  pl.BlockSpec((nslab, n2, ct), lambda c: (0, 0, nct + c)),
                  pl.BlockSpec((1, ct), lambda c: (0, c)),
                  pl.BlockSpec((2, nslab, n2, ct), lambda c: (0, 0, 0, 0)),
                  pl.BlockSpec((2 * n2, 2 * n2), fixed2)],
        out_specs=pl.BlockSpec((None, 2, n2, nk_pad * ct), lambda c: (c, 0, 0, 0)),
        out_shape=jax.ShapeDtypeStruct((nct, 2, n2, nk_pad * ct), F32),
        scratch_shapes=[pltpu.VMEM((2, 2, max(npair, 1), n2, ct), F32)],
        compiler_params=_params(("parallel",)),
        name="filter_spectrum",
    )(h3, h3, inv_norm, _twiddle1(n1, n2, ct), ffwd)


def _hyena(ua, ga, conv_w, conv_b, skip, kf):
    bsz, seq, _ = ua.shape
    n1, n2, ct, nk_pad = _hyena_plan(seq)
    nslab = n1 // 2
    nct = D_A // ct
    npair = len(_radix_parts(nslab)[1])
    cb = conv_b.reshape(1, 3 * D_A)
    sk = skip.reshape(1, D_A)
    nseq = max(1, 1024 // seq)
    part = lambda p: pl.BlockSpec((nseq, seq, ct), lambda c, b: (b, 0, p * nct + c))
    wpart = lambda p: pl.BlockSpec((3, ct), lambda c, b: (0, p * nct + c))
    bpart = lambda p: pl.BlockSpec((1, ct), lambda c, b: (0, p * nct + c))
    return pl.pallas_call(
        functools.partial(_hyena_kernel, seq=seq, nslab=nslab, n2=n2, nk_pad=nk_pad),
        grid=(nct, bsz // nseq),
        in_specs=[part(0), part(1), part(2),
                  pl.BlockSpec((nseq, seq, ct), lambda c, b: (b, 0, c)),
                  wpart(0), wpart(1), wpart(2), bpart(0), bpart(1), bpart(2),
                  pl.BlockSpec((1, ct), lambda c, b: (0, c)),
                  pl.BlockSpec((None, 2, n2, nk_pad * ct), lambda c, b: (c, 0, 0, 0)),
                  pl.BlockSpec((2, nslab, n2, ct), lambda c, b: (0, 0, 0, 0)),
                  pl.BlockSpec((2 * n2, 2 * n2), lambda c, b: (0, 0))],
        out_specs=pl.BlockSpec((nseq, seq, ct), lambda c, b: (b, 0, c)),
        out_shape=jax.ShapeDtypeStruct((bsz, seq, D_A), F32),
        scratch_shapes=[pltpu.VMEM((nslab, n2, ct), F32),
                        pltpu.VMEM((2, max(npair, 1), n2, ct), F32),
                        pltpu.VMEM((2 * n2, nk_pad * ct), BF16),
                        pltpu.VMEM((2, nk_pad, n2, ct), F32)],
        compiler_params=_params(("parallel", "parallel")),
        name="hyena",
    )(ua, ua, ua, ga, conv_w, conv_w, conv_w, cb, cb, cb, sk, kf, _twiddle1(n1, n2, ct),
      _bf16_split(_dft_block(n2))[0])


def _hyena_filter_kernel(z_ref, w1_ref, b1_ref, w2_ref, b2_ref, w3_ref, del_ref,
                         o_ref, nrm_ref, *, seq):
    hp = lax.Precision.HIGHEST
    tm = o_ref.shape[0]
    i = pl.program_id(0)
    h = jnp.sin(jnp.dot(z_ref[...], w1_ref[...], precision=hp,
                        preferred_element_type=F32) + b1_ref[...])
    h = jnp.sin(jnp.dot(h, w2_ref[...], precision=hp, preferred_element_type=F32) + b2_ref[...])
    tcol = (i * tm + lax.broadcasted_iota(jnp.int32, (tm, 1), 0)).astype(F32) * (1.0 / (seq - 1))
    h = jnp.dot(h, w3_ref[...], precision=hp, preferred_element_type=F32) * jnp.exp(
        -tcol * del_ref[...])
    o_ref[...] = h

    @pl.when(i == 0)
    def _():
        nrm_ref[...] = jnp.zeros_like(nrm_ref)

    nrm_ref[...] += jnp.sum(jnp.abs(h), axis=0, keepdims=True)


def _hyena_filter(seq, w1, b1, w2, b2, w3):
    bands = (EMB_DIM - 1) // 2
    ang = (2.0 * np.pi * np.arange(seq)[:, None] / seq) * np.linspace(1e-4, bands - 1, bands)
    z = np.zeros((seq, LANES), np.float32)
    z[:, 0] = np.linspace(0.0, 1.0, seq)
    z[:, 1:1 + bands] = np.cos(ang)
    z[:, 1 + bands:EMB_DIM] = -np.sin(ang)
    deltas = np.abs(np.linspace(MIN_DECAY, MAX_DECAY, D_A)).astype(np.float32)
    deltas2 = np.concatenate([deltas, deltas])[None, :]
    w1p = jnp.pad(w1, ((0, LANES - EMB_DIM), (0, 0)))
    tm = min(seq, 1024)
    fo = w1.shape[1]
    fixed = lambda i: (0, 0)
    h, nrm = pl.pallas_call(
        functools.partial(_hyena_filter_kernel, seq=seq),
        grid=(seq // tm,),
        in_specs=[pl.BlockSpec((tm, LANES), lambda i: (i, 0)),
                  pl.BlockSpec((LANES, fo), fixed),
                  pl.BlockSpec((1, fo), fixed),
                  pl.BlockSpec((fo, fo), fixed),
                  pl.BlockSpec((1, fo), fixed),
                  pl.BlockSpec((fo, 2 * D_A), fixed),
                  pl.BlockSpec((1, 2 * D_A), fixed)],
        out_specs=[pl.BlockSpec((tm, 2 * D_A), lambda i: (i, 0)),
                   pl.BlockSpec((1, 2 * D_A), fixed)],
        out_shape=[jax.ShapeDtypeStruct((seq, 2 * D_A), F32),
                   jax.ShapeDtypeStruct((1, 2 * D_A), F32)],
        compiler_params=_params(("arbitrary",)),
        name="hyena_filter",
    )(jnp.asarray(z), w1p, b1.reshape(1, fo), w2, b2.reshape(1, fo), w3, jnp.asarray(deltas2))
    return h, 1.0 / (nrm[:, :D_A] + nrm[:, D_A:])


def _attn_kernel(cq_ref, cos_ref, sin_ref, wa_ref, wb_ref, k_ref, v_ref, gb_ref, o_ref, *, nseq):
    nt = (((1,), (1,)), ((), ()))
    tq = cq_ref.shape[1]
    cos = jnp.concatenate([cos_ref[...]] * N_HEADS, axis=-1)
    sin = jnp.concatenate([sin_ref[...]] * N_HEADS, axis=-1)
    lane = lax.broadcasted_iota(jnp.int32, (tq, 2 * V_DIM), 1)

    def one_sequence(sq):
        cq = cq_ref[sq]
        q = (_dot(cq, wa_ref[...]) * cos + _dot(cq, wb_ref[...]) * sin) * (ATTN_SCALE * LOG2_E)
        q = q.astype(BF16)

        def scores(h):
            hsl = slice(h * HEAD_PAD, (h + 1) * HEAD_PAD)
            return lax.dot_general(q[:, hsl], k_ref[sq, :, hsl], nt, preferred_element_type=F32)

        look = 1 if k_ref.shape[1] > 2 * tq else N_HEADS - 1
        pending = [scores(h) for h in range(look)]
        outs = []
        for h in range(N_HEADS):
            if h + look < N_HEADS:
                pending.append(scores(h + look))
            sb = pending.pop(0).astype(BF16)
            p = jnp.exp2(sb - jnp.max(sb, axis=-1, keepdims=True))
            of = _dot(p, v_ref[sq, :, h * HEAD_PAD:(h + 1) * HEAD_PAD])
            outs.append(of / of[:, V_DIM:V_DIM + 1])
            if h % 2 == 1:
                vsl = slice((h // 2) * 2 * V_DIM, (h // 2 + 1) * 2 * V_DIM)
                o = jnp.where(lane < V_DIM, outs[h - 1], pltpu.roll(outs[h], V_DIM, 1))
                o_ref[sq, :, vsl] = (o * _silu(gb_ref[sq, :, vsl])).astype(BF16)

    if nseq == 1:
        one_sequence(0)
    else:
        def body(sq, carry):
            one_sequence(sq)
            return carry

        lax.fori_loop(0, nseq, body, 0)


def _attention(cq, cos, sin, wa, wb, k, v, gb, tq, nseq=1):
    bsz, lq, _ = cq.shape
    lk = k.shape[1]
    fixed = lambda b, i: (0, 0)
    qrow = lambda b, i: (b, i, 0)
    kv = lambda b, i: (b, 0, 0)
    pos = lambda b, i: (i, 0)
    return pl.pallas_call(
        functools.partial(_attn_kernel, nseq=nseq),
        grid=(bsz // nseq, lq // tq),
        in_specs=[pl.BlockSpec((nseq, tq, Q_RANK), qrow),
                  pl.BlockSpec((tq, LANES), pos),
                  pl.BlockSpec((tq, LANES), pos),
                  pl.BlockSpec((Q_RANK, N_HEADS * HEAD_PAD), fixed),
                  pl.BlockSpec((Q_RANK, N_HEADS * HEAD_PAD), fixed),
                  pl.BlockSpec((nseq, lk, N_HEADS * HEAD_PAD), kv),
                  pl.BlockSpec((nseq, lk, N_HEADS * HEAD_PAD), kv),
                  pl.BlockSpec((nseq, tq, D_B), qrow)],
        out_specs=pl.BlockSpec((nseq, tq, D_B), qrow),
        out_shape=jax.ShapeDtypeStruct((bsz, lq, D_B), BF16),
        compiler_params=_params(("parallel", "arbitrary")),
        name="mla_attention",
    )(cq, cos, sin, wa, wb, k, v, gb)


def _mid_kernel(x_ref, ya_ref, ob_ref, gt_ref, wo_ref, sc_ref, sh_ref, wi_ref,
                x1_ref, u_ref, g_ref):
    y = _dot(ya_ref[...].astype(BF16), wo_ref[0:D_A, :]) + _dot(ob_ref[...], wo_ref[D_A:, :])
    x1 = x_ref[...] + gt_ref[...] * y
    x1_ref[...] = x1
    h = (_rms(x1) * sc_ref[...] + sh_ref[...]).astype(BF16)
    u_ref[...] = _dot(h, wi_ref[:, :D_C]).astype(BF16)
    g_ref[...] = _dot(h, wi_ref[:, D_C:]).astype(BF16)


def _mid(x2d, ya, ob, gt, wo, sc, sh, wi, tiles_per_mod, tm):
    t = x2d.shape[0]
    row = lambda i: (i, 0)
    fixed = lambda i: (0, 0)
    mod = lambda i: (i // tiles_per_mod, 0, 0)
    return pl.pallas_call(
        _mid_kernel,
        grid=(t // tm,),
        in_specs=[pl.BlockSpec((tm, D_MODEL), row),
                  pl.BlockSpec((tm, D_A), row),
                  pl.BlockSpec((tm, D_B), row),
                  pl.BlockSpec((None, 1, D_MODEL), mod),
                  pl.BlockSpec((D_A + D_B, D_MODEL), fixed),
                  pl.BlockSpec((None, 1, D_MODEL), mod),
                  pl.BlockSpec((None, 1, D_MODEL), mod),
                  pl.BlockSpec((D_MODEL, 2 * D_C), fixed)],
        out_specs=[pl.BlockSpec((tm, D_MODEL), row),
                   pl.BlockSpec((tm, D_C), row),
                   pl.BlockSpec((tm, D_C), row)],
        out_shape=[jax.ShapeDtypeStruct((t, D_MODEL), F32),
                   jax.ShapeDtypeStruct((t, D_C), BF16),
                   jax.ShapeDtypeStruct((t, D_C), BF16)],
        compiler_params=_params(("parallel",)),
        name="outproj_even_inproj_odd",
    )(x2d, ya, ob, gt, wo, sc, sh, wi)


def _fnet_weights_kernel(w_ref, cs_ref, o_ref):
    hp = lax.Precision.HIGHEST
    o_ref[...] = jnp.dot(cs_ref[...], w_ref[...], precision=hp,
                         preferred_element_type=F32).astype(BF16)


def _fnet_weights(fnet_w, seq):
    c, s = _dft_cos_sin(G_C)
    cs = jnp.asarray((np.concatenate([c, s], axis=0) / math.sqrt(seq * G_C)).astype(np.float32))
    return pl.pallas_call(
        _fnet_weights_kernel,
        grid=(N_GROUPS_C,),
        in_specs=[pl.BlockSpec((None, G_C, G_C), lambda g: (g, 0, 0)),
                  pl.BlockSpec((2 * G_C, G_C), lambda g: (0, 0))],
        out_specs=pl.BlockSpec((None, 2 * G_C, G_C), lambda g: (g, 0, 0)),
        out_shape=jax.ShapeDtypeStruct((N_GROUPS_C, 2 * G_C, G_C), BF16),
        compiler_params=_params(("parallel",)),
        name="fnet_weights",
    )(fnet_w, cs)


def _fnet_long_kernel(u_ref, tw1_ref, m_ref, ab_ref, o_ref, sd_scr, a_scr, *, n1, n2):
    h = n1 // 2
    ct = o_ref.shape[-1]
    for i, n in enumerate(range(1, h)):
        a = u_ref[n].astype(F32)
        b = u_ref[n1 - n].astype(F32)
        sd_scr[0, i] = a + b
        sd_scr[1, i] = a - b
    x0 = u_ref[0].astype(F32)
    xh = u_ref[h].astype(F32)
    zero = jnp.zeros((n2, ct), BF16)
    tw = None
    for k1 in range(h + 1):
        ar = x0 + xh if k1 % 2 == 0 else x0 - xh
        ai = None
        for i, n in enumerate(range(1, h)):
            th = 2.0 * math.pi * n * k1 / n1
            ar = _acc(ar, math.cos(th), sd_scr[0, i])
            ai = _acc(ai, -math.sin(th), sd_scr[1, i])
        if k1 > 0:
            tw = _next_twiddle(tw, tw1_ref)
            ar, ai = _cmul(ar, ai, tw[0], tw[1])
        lanes = slice(k1 * ct, (k1 + 1) * ct)
        a_scr[0:n2, lanes] = ar.astype(BF16)
        a_scr[n2:2 * n2, lanes] = zero if ai is None else ai.astype(BF16)

    def group_maps(k1, xr, xi):
        for gi in range(ct // G_C):
            gl = slice(gi * G_C, (gi + 1) * G_C)
            y = _dot(jnp.concatenate([xr[:, gl], xi[:, gl]], axis=1).astype(BF16), ab_ref[gi])
            o_ref[k1, :, gl] = y.astype(o_ref.dtype)

    def dft(k1):
        rows = 4 * n2 if 0 < k1 < h else 2 * n2
        return _dot(m_ref[0:rows, :], a_scr[:, k1 * ct:(k1 + 1) * ct])

    x_next = dft(0)
    for k1 in range(h + 1):
        x = x_next
        if k1 < h:
            x_next = dft(k1 + 1)
        group_maps(k1, x[:n2], x[n2:2 * n2])
        if 0 < k1 < h:
            group_maps(n1 - k1, x[2 * n2:3 * n2], x[3 * n2:])


def _fnet_long(u, ab, n1, n2):
    bsz, seq, _ = u.shape
    h = n1 // 2
    ng = 2
    ct = ng * G_C
    c, s = _dft_cos_sin(n2)
    ce, se = _dft_cos_sin(n2, shift=1)
    m = _bf16_split(np.block([[c, s], [-s, c], [ce, -se], [-se, -ce]]))[0]
    out = pl.pallas_call(
        functools.partial(_fnet_long_kernel, n1=n1, n2=n2),
        grid=(bsz, N_GROUPS_C // ng),
        in_specs=[pl.BlockSpec((None, n1, n2, ct), lambda b, g: (b, 0, 0, g)),
                  pl.BlockSpec((2, h, n2, ct), lambda b, g: (0, 0, 0, 0)),
                  pl.BlockSpec((4 * n2, 2 * n2), lambda b, g: (0, 0)),
                  pl.BlockSpec((ng, 2 * G_C, G_C), lambda b, g: (g, 0, 0))],
        out_specs=pl.BlockSpec((None, n1, n2, ct), lambda b, g: (b, 0, 0, g)),
        out_shape=jax.ShapeDtypeStruct((bsz, n1, n2, D_C), BF16),
        scratch_shapes=[pltpu.VMEM((2, h - 1, n2, ct), F32),
                        pltpu.VMEM((2 * n2, (h + 1) * ct), BF16)],
        compiler_params=_params(("parallel", "parallel")),
        name="fnet_long",
    )(u.reshape(bsz, n1, n2, D_C), _twiddle1(n1, n2, ct), m, ab)
    return out.transpose(0, 2, 1, 3).reshape(bsz, seq, D_C)


def _fnet_short_kernel(u_ref, f_ref, ab_ref, o_ref, *, seq, scale):
    def body(sq, carry):
        x = _dot(f_ref[...], u_ref[sq])
        xr, xi = x[:seq], x[seq:]
        for g in range(N_GROUPS_C):
            sl = slice(g * G_C, (g + 1) * G_C)
            xin = jnp.concatenate([xr[:, sl], xi[:, sl]], axis=1).astype(BF16)
            o_ref[sq, :, sl] = (_dot(xin, ab_ref[g]) * scale).astype(o_ref.dtype)
        return carry

    lax.fori_loop(0, u_ref.shape[0], body, 0)


def _fnet_short(u, ab, scale, nseq=4):
    bsz, seq, _ = u.shape
    c, s = _dft_cos_sin(seq)
    f = _bf16_split(np.concatenate([c, -s], axis=0))[0]
    return pl.pallas_call(
        functools.partial(_fnet_short_kernel, seq=seq, scale=scale),
        grid=(bsz // nseq,),
        in_specs=[pl.BlockSpec((nseq, seq, D_C), lambda b: (b, 0, 0)),
                  pl.BlockSpec((2 * seq, seq), lambda b: (0, 0)),
                  pl.BlockSpec((N_GROUPS_C, 2 * G_C, G_C), lambda b: (0, 0, 0))],
        out_specs=pl.BlockSpec((nseq, seq, D_C), lambda b: (b, 0, 0)),
        out_shape=jax.ShapeDtypeStruct((bsz, seq, D_C), BF16),
        compiler_params=_params(("parallel",)),
        name="fnet_short",
    )(u, f, ab)


def _final_kernel(x1_ref, y_ref, g_ref, gt_ref, wo_ref, fg_ref, o_ref):
    z = (y_ref[...].astype(F32) * _silu(g_ref[...].astype(F32))).astype(BF16)
    x2 = x1_ref[...] + gt_ref[...] * _dot(z, wo_ref[...])
    o_ref[...] = _rms(x2) * fg_ref[...]


def _final(x1, y, g, gt, wo, fg, tiles_per_mod, tm):
    t = x1.shape[0]
    row = lambda i: (i, 0)
    fixed = lambda i: (0, 0)
    mod = lambda i: (i // tiles_per_mod, 0, 0)
    return pl.pallas_call(
        _final_kernel,
        grid=(t // tm,),
        in_specs=[pl.BlockSpec((tm, D_MODEL), row),
                  pl.BlockSpec((tm, D_C), row),
                  pl.BlockSpec((tm, D_C), row),
                  pl.BlockSpec((None, 1, D_MODEL), mod),
                  pl.BlockSpec((D_C, D_MODEL), fixed),
                  pl.BlockSpec((1, D_MODEL), fixed)],
        out_specs=pl.BlockSpec((tm, D_MODEL), row),
        out_shape=jax.ShapeDtypeStruct((t, D_MODEL), F32),
        compiler_params=_params(("parallel",)),
        name="outproj_odd_final",
    )(x1, y, g, gt, wo, fg)


def _rope_swap(w):
    nf = ROPE_DIM // 4
    w4 = w.reshape(w.shape[:-1] + (2, 2, nf))
    return jnp.stack([-w4[..., 1, :], w4[..., 0, :]], axis=-2).reshape(w.shape)


def _pack_w_in_even(w):
    ua_ga_cq_ckv = w[:, :4 * D_A + Q_RANK + KV_RANK]
    o = 4 * D_A + Q_RANK + KV_RANK
    krope = w[:, o:o + ROPE_DIM]
    gb = w[:, o + ROPE_DIM:]
    zpad = jnp.zeros((w.shape[0], LANES - ROPE_DIM), w.dtype)
    return jnp.concatenate([ua_ga_cq_ckv, gb, krope, zpad, _rope_swap(krope), zpad],
                           axis=1).astype(BF16)


def _pack_w_uq(w):
    w3 = w.reshape(Q_RANK, N_HEADS, QK_DIM)
    zq = jnp.zeros((Q_RANK, N_HEADS, HEAD_PAD - QK_DIM), w.dtype)
    wa = jnp.concatenate([w3, zq], axis=-1)
    zn = jnp.zeros((Q_RANK, N_HEADS, NOPE_DIM), w.dtype)
    wb = jnp.concatenate([zn, _rope_swap(w3[..., NOPE_DIM:]), zq], axis=-1)
    shape = (Q_RANK, N_HEADS * HEAD_PAD)
    return wa.reshape(shape).astype(BF16), wb.reshape(shape).astype(BF16)


def _pack_w_ukv(w):
    w3 = w.reshape(KV_RANK, N_HEADS, NOPE_DIM + V_DIM)
    zk = jnp.zeros((KV_RANK, N_HEADS, HEAD_PAD - NOPE_DIM), w.dtype)
    wk_c = jnp.concatenate([w3[..., :NOPE_DIM], zk], axis=-1).reshape(KV_RANK, -1)
    eye = jnp.eye(LANES, ROPE_DIM, dtype=w.dtype)
    rope_rows = jnp.concatenate([jnp.zeros((LANES, NOPE_DIM), w.dtype), eye,
                                 jnp.zeros((LANES, HEAD_PAD - QK_DIM), w.dtype)], axis=-1)
    wk_r = jnp.tile(rope_rows, (1, N_HEADS))
    wk = jnp.concatenate([wk_c, wk_r], axis=0)
    zv = jnp.zeros((KV_RANK, N_HEADS, HEAD_PAD - V_DIM), w.dtype)
    wv = jnp.concatenate([w3[..., NOPE_DIM:], zv], axis=-1).reshape(KV_RANK, -1)
    ones = np.zeros((1, N_HEADS, HEAD_PAD), np.float32)
    ones[:, :, V_DIM] = 1.0
    return wk.astype(BF16), wv.astype(BF16), jnp.asarray(ones.reshape(1, -1))


def _rope_tables(seq, rotate):
    pos = np.arange(seq)
    half = ROPE_DIM // 2
    inv = ROPE_BASE ** (-np.arange(0, half, 2, dtype=np.float64) / half)
    r = (pos // GRID_W)[:, None] * inv
    c = (pos % GRID_W)[:, None] * inv
    ang = np.concatenate([r, r, c, c], axis=-1) * (1.0 if rotate else 0.0)
    cos_q = np.zeros((seq, LANES))
    sin_q = np.zeros((seq, LANES))
    cos_k = np.zeros((seq, LANES))
    sin_k = np.zeros((seq, LANES))
    cos_q[:, :NOPE_DIM] = 1.0
    cos_q[:, NOPE_DIM:QK_DIM] = np.cos(ang)
    sin_q[:, NOPE_DIM:QK_DIM] = np.sin(ang)
    cos_k[:, :ROPE_DIM] = np.cos(ang)
    sin_k[:, :ROPE_DIM] = np.sin(ang)
    return tuple(jnp.asarray(t.astype(np.float32)) for t in (cos_q, sin_q, cos_k, sin_k))


def kernel(x_prompt, x_sample, c, cache_ckv, cache_krope, c_ctx, norm_g, w_mod, b_mod,
           w_in_e, conv_w, conv_b, filt_w1, filt_b1, filt_w2, filt_b2, filt_w3, hyena_skip,
           q_norm_g, kv_norm_g, w_uq, w_ukv, w_out_e, w_in_o, fnet_w, w_out_o, final_g):
    nb_p, seq_p, _ = x_prompt.shape
    nb_s, seq_s, _ = x_sample.shape
    tp = nb_p * seq_p
    ts = nb_s * seq_s
    tm = 512

    cond = jnp.concatenate([c_ctx[None, :], c, jnp.zeros((8 - 1 - nb_s, D_MODEL), F32)], axis=0)
    mod = _ada_params(cond, w_mod, b_mod)
    shift = mod[:, :, None, :D_MODEL]
    scale = (1.0 + mod[:, :, None, D_MODEL:2 * D_MODEL]) * norm_g[:, None, None, :]
    gate = mod[:, :, None, 2 * D_MODEL:]

    xp = x_prompt.reshape(tp, D_MODEL)
    xs = x_sample.reshape(ts, D_MODEL)

    w_in = _pack_w_in_even(w_in_e[0])
    qg = q_norm_g[0].reshape(1, Q_RANK)
    kvg = kv_norm_g[0].reshape(1, KV_RANK)
    wq_a, wq_b = _pack_w_uq(w_uq[0])
    wk, wv, vone = _pack_w_ukv(w_ukv[0])
    cos_qp, sin_qp, cos_kp, sin_kp = _rope_tables(tm, rotate=False)
    cos_qs, sin_qs, cos_ks, sin_ks = _rope_tables(seq_s, rotate=True)
    pr_p = _inproj_even(xp, scale[0, :1], shift[0, :1], w_in, qg, kvg, cos_kp, sin_kp, wk, wv,
                        vone, tm, True)
    pr_s = _inproj_even(xs, scale[0, 1:1 + nb_s], shift[0, 1:1 + nb_s], w_in, qg, kvg,
                        cos_ks, sin_ks, wk, wv, vone, tm, False,
                        cache=(cache_ckv, cache_krope, 0))
    ua_p, ga_p, cq_p, gb_p, k_p, v_p, ckv_p, kra_p = pr_p
    ua_s, ga_s, cq_s, gb_s, k_s, v_s = pr_s

    filt = (filt_w1[0], filt_b1[0], filt_w2[0], filt_b2[0], filt_w3[0])
    ya = []
    for ua, ga, nb, seq in ((ua_p, ga_p, nb_p, seq_p), (ua_s, ga_s, nb_s, seq_s)):
        h, inv_norm = _hyena_filter(seq, *filt)
        kf = _spectrum(h, inv_norm, seq)
        ya.append(_hyena(ua.reshape(nb, seq, 3 * D_A), ga.reshape(nb, seq, D_A),
                         conv_w[0], conv_b[0], hyena_skip[0], kf))
    ya_p = ya[0].reshape(tp, D_A)
    ya_s = ya[1].reshape(ts, D_A)

    ob_p = _attention(cq_p.reshape(nb_p, seq_p, Q_RANK), cos_qp, sin_qp, wq_a, wq_b,
                      k_p.reshape(nb_p, seq_p, -1), v_p.reshape(nb_p, seq_p, -1),
                      gb_p.reshape(nb_p, seq_p, D_B), seq_p, nseq=4)
    ob_s = _attention(cq_s.reshape(nb_s, seq_s, Q_RANK), cos_qs, sin_qs, wq_a, wq_b, k_s, v_s,
                      gb_s.reshape(nb_s, seq_s, D_B), 256)

    wo_e = w_out_e[0].astype(BF16)
    wi_o = w_in_o[0].astype(BF16)
    x1_p, u_p, g_p = _mid(xp, ya_p, ob_p.reshape(tp, D_B), gate[0, :1], wo_e,
                          scale[1, :1], shift[1, :1], wi_o, tp // tm, tm)
    x1_s, u_s, g_s = _mid(xs, ya_s, ob_s.reshape(ts, D_B), gate[0, 1:1 + nb_s], wo_e,
                          scale[1, 1:1 + nb_s], shift[1, 1:1 + nb_s], wi_o, seq_s // tm, tm)

    ab = _fnet_weights(fnet_w[0], seq_s)
    y_p = _fnet_short(u_p.reshape(nb_p, seq_p, D_C), ab, math.sqrt(seq_s / seq_p))
    y_s = _fnet_long(u_s.reshape(nb_s, seq_s, D_C), ab, 16, seq_s // 16)
    wo_o = w_out_o[0].astype(BF16)
    fg = final_g.reshape(1, D_MODEL)
    out_p = _final(x1_p, y_p.reshape(tp, D_C), g_p, gate[1, :1], wo_o, fg, tp // tm, tm)
    out_s = _final(x1_s, y_s.reshape(ts, D_C), g_s, gate[1, 1:1 + nb_s], wo_o, fg,
                   seq_s // tm, tm)

    state_ckv = ckv_p.reshape(nb_p, 1, seq_p, KV_RANK)
    state_krope = kra_p[:, :ROPE_DIM].reshape(nb_p, 1, seq_p, ROPE_DIM)
    return (out_p.reshape(nb_p, seq_p, D_MODEL), out_s.reshape(nb_s, seq_s, D_MODEL),
            state_ckv, state_krope)
```

```python
import functools
import math

import numpy as np
import jax
import jax.numpy as jnp
from jax import lax
from jax.experimental import pallas as pl
from jax.experimental.pallas import tpu as pltpu

F32 = jnp.float32
BF16 = jnp.bfloat16

D_MODEL = 1024
GRID_W = 64
EPS = 1e-6
D_A = 512
EMB_DIM = 33
DECAY_TARGET = 1e-2
MIN_DECAY = math.log(DECAY_TARGET) / 1.5
MAX_DECAY = math.log(DECAY_TARGET) / 0.3
N_HEADS = 8
Q_RANK = 256
KV_RANK = 256
NOPE_DIM = 64
ROPE_DIM = 32
V_DIM = 64
ROPE_BASE = 10000.0
D_B = N_HEADS * V_DIM
QK_DIM = NOPE_DIM + ROPE_DIM
ATTN_SCALE = 1.0 / math.sqrt(QK_DIM)
LOG2_E = math.log2(math.e)
D_C = 1024
N_GROUPS_C = 8
G_C = D_C // N_GROUPS_C

LANES = 128
HEAD_PAD = 128
VMEM_LIMIT = 56 * 1024 * 1024

_C_UA, _C_GA, _C_CQ, _C_CKV, _C_GB, _C_KRA, _C_KRB, _C_END = (
    0, 1536, 2048, 2304, 2560, 3072, 3200, 3328)


def _params(sem):
    return pltpu.CompilerParams(dimension_semantics=sem, vmem_limit_bytes=VMEM_LIMIT)


def _rms(x):
    return x * lax.rsqrt(jnp.mean(x * x, axis=-1, keepdims=True) + EPS)


def _silu(g):
    return g * jax.nn.sigmoid(g)


def _dot(a, b):
    return jnp.dot(a, b, preferred_element_type=F32)


def _ada_kernel(cond_ref, w_ref, b_ref, o_ref):
    c = _silu(cond_ref[...])
    o_ref[...] = jnp.dot(c, w_ref[...], preferred_element_type=F32,
                         precision=lax.Precision.HIGHEST) + b_ref[...]


def _ada_params(cond, w_mod, b_mod):
    depth = w_mod.shape[0]
    n = cond.shape[0]
    tn = 1024
    return pl.pallas_call(
        _ada_kernel,
        grid=(depth, 3 * D_MODEL // tn),
        in_specs=[pl.BlockSpec((n, D_MODEL), lambda i, j: (0, 0)),
                  pl.BlockSpec((None, D_MODEL, tn), lambda i, j: (i, 0, j)),
                  pl.BlockSpec((None, 1, tn), lambda i, j: (i, 0, j))],
        out_specs=pl.BlockSpec((None, n, tn), lambda i, j: (i, 0, j)),
        out_shape=jax.ShapeDtypeStruct((depth, n, 3 * D_MODEL), F32),
        compiler_params=_params(("arbitrary", "arbitrary")),
        name="ada_params",
    )(cond, w_mod, b_mod.reshape(depth, 1, 3 * D_MODEL))


def _inproj_even_kernel(x_ref, sc_ref, sh_ref, w_ref, qg_ref, kvg_ref, cos_ref, sin_ref,
                        wk_ref, wv_ref, vone_ref, *refs, nctx, emit_state):
    if nctx:
        cckv_ref, ckr_ref = refs[:2]
        refs = refs[2:]
    ua_ref, ga_ref, cq_ref, gb_ref, k_ref, v_ref = refs[:6]
    state_refs = refs[6:]

    def tokens():
        h = (_rms(x_ref[...]) * sc_ref[...] + sh_ref[...]).astype(BF16)

        def proj(a, b):
            return _dot(h, w_ref[:, a:b])

        ua_ref[...] = proj(_C_UA, _C_GA)
        ga_ref[...] = proj(_C_GA, _C_CQ)
        cq_ref[...] = (_rms(proj(_C_CQ, _C_CKV)) * qg_ref[...]).astype(BF16)
        gb_ref[...] = proj(_C_GB, _C_KRA)
        ckv = _rms(proj(_C_CKV, _C_GB)) * kvg_ref[...]
        kra = proj(_C_KRA, _C_KRB)
        krb = proj(_C_KRB, _C_END)
        c = ckv.astype(BF16)
        kr = pltpu.roll(kra * cos_ref[...] + krb * sin_ref[...], NOPE_DIM, 1)
        k = _dot(c, wk_ref[0:KV_RANK, :]) + jnp.concatenate([kr] * N_HEADS, axis=-1)
        k_ref[...] = k.astype(BF16)
        v_ref[...] = (_dot(c, wv_ref[...]) + vone_ref[...]).astype(BF16)
        if emit_state:
            state_refs[0][...] = ckv
            state_refs[1][...] = kra

    if not nctx:
        tokens()
        return

    t = pl.program_id(1)

    @pl.when(t < nctx)
    def _():
        c = cckv_ref[...].astype(BF16)
        kr = ckr_ref[...].astype(BF16)
        k_ref[...] = (_dot(c, wk_ref[0:KV_RANK, :])
                      + _dot(kr, wk_ref[KV_RANK:KV_RANK + ROPE_DIM, :])).astype(BF16)
        v_ref[...] = (_dot(c, wv_ref[...]) + vone_ref[...]).astype(BF16)

    pl.when(t >= nctx)(tokens)


def _inproj_even(x2d, sc, sh, w, qg, kvg, cos_k, sin_k, wk, wv, vone, tm, emit_state,
                 cache=None):
    t = x2d.shape[0]
    nb = sc.shape[0]
    tiles = t // nb // tm
    tiles_per_seq = cos_k.shape[0] // tm
    nctx = 0 if cache is None else cache[0].shape[2] // tm
    tok = lambda b, s: b * tiles + jnp.maximum(s - nctx, 0)
    row = lambda b, s: (tok(b, s), 0)
    fixed = lambda b, s: (0, 0)
    mod = lambda b, s: (b, 0, 0)
    pos = lambda b, s: (jnp.maximum(s - nctx, 0) % tiles_per_seq, 0)
    kvo = lambda b, s: (b, s, 0)
    widths = (3 * D_A, D_A, Q_RANK, D_B)
    dtypes = (F32, F32, BF16, F32)
    in_specs = [pl.BlockSpec((tm, D_MODEL), row),
                pl.BlockSpec((None, 1, D_MODEL), mod),
                pl.BlockSpec((None, 1, D_MODEL), mod),
                pl.BlockSpec((D_MODEL, _C_END), fixed),
                pl.BlockSpec((1, Q_RANK), fixed),
                pl.BlockSpec((1, KV_RANK), fixed),
                pl.BlockSpec((tm, LANES), pos),
                pl.BlockSpec((tm, LANES), pos),
                pl.BlockSpec((KV_RANK + LANES, N_HEADS * HEAD_PAD), fixed),
                pl.BlockSpec((KV_RANK, N_HEADS * HEAD_PAD), fixed),
                pl.BlockSpec((1, N_HEADS * HEAD_PAD), fixed)]
    args = [x2d, sc, sh, w, qg, kvg, cos_k, sin_k, wk, wv, vone]
    if nctx:
        layer = cache[2]
        ctx = lambda b, s: (b, layer, jnp.minimum(s, nctx - 1), 0)
        in_specs += [pl.BlockSpec((None, None, tm, KV_RANK), ctx),
                     pl.BlockSpec((None, None, tm, ROPE_DIM), ctx)]
        args += [cache[0], cache[1]]
    out_specs = [pl.BlockSpec((tm, wd), row) for wd in widths]
    out_shape = [jax.ShapeDtypeStruct((t, wd), dt) for wd, dt in zip(widths, dtypes)]
    for wd in (N_HEADS * HEAD_PAD, N_HEADS * HEAD_PAD):
        out_specs.append(pl.BlockSpec((None, tm, wd), kvo))
        out_shape.append(jax.ShapeDtypeStruct((nb, (nctx + tiles) * tm, wd), BF16))
    if emit_state:
        for wd in (KV_RANK, LANES):
            out_specs.append(pl.BlockSpec((tm, wd), row))
            out_shape.append(jax.ShapeDtypeStruct((t, wd), F32))
    return pl.pallas_call(
        functools.partial(_inproj_even_kernel, nctx=nctx, emit_state=emit_state),
        grid=(nb, nctx + tiles),
        in_specs=in_specs,
        out_specs=out_specs,
        out_shape=out_shape,
        compiler_params=_params(("parallel", "arbitrary")),
        name="inproj_even",
    )(*args)


def _acc(acc, coef, x):
    if x is None or abs(coef) < 1e-12:
        return acc
    if abs(abs(coef) - 1.0) < 1e-12:
        if acc is None:
            return x if coef > 0 else -x
        return acc + x if coef > 0 else acc - x
    return coef * x if acc is None else acc + coef * x


def _add(a, b):
    if a is None:
        return b
    return a if b is None else a + b


def _sub(a, b):
    if b is None:
        return a
    return -b if a is None else a - b


def _cmul(ar, ai, br, bi):
    def mul(x, y):
        return None if x is None or y is None else x * y
    return _sub(mul(ar, br), mul(ai, bi)), _add(mul(ar, bi), mul(ai, br))


def _radix_parts(h):
    singles = [0] + ([h // 2] if h >= 2 and h % 2 == 0 else [])
    pairs = [(n, h - n) for n in range(1, (h + 1) // 2)]
    return singles, pairs


def _radix_fwd(k1, h, single, pair_sum, pair_diff):
    singles, pairs = _radix_parts(h)
    ar = ai = None
    for n in singles:
        th = 2.0 * math.pi * n * k1 / (2 * h)
        x = single(n)
        ar = _acc(ar, math.cos(th), x)
        ai = _acc(ai, -math.sin(th), x)
    for i, (n, _) in enumerate(pairs):
        th = 2.0 * math.pi * n * k1 / (2 * h)
        re_src, im_src = (pair_sum, pair_diff) if k1 % 2 == 0 else (pair_diff, pair_sum)
        ar = _acc(ar, math.cos(th), re_src(i))
        ai = _acc(ai, -math.sin(th), im_src(i))
    return ar, ai


def _radix_inv(h, n_total, bpr, bpi, emit):
    singles, pairs = _radix_parts(h)

    def coefs(n, k1):
        th = 2.0 * math.pi * n * k1 / (2 * h)
        w = (1.0 if k1 in (0, h) else 2.0) / n_total
        return w * math.cos(th), w * math.sin(th)

    for n in singles:
        y = None
        for k1 in range(h + 1):
            c, s = coefs(n, k1)
            y = _acc(y, c, bpr(k1) if abs(c) > 1e-12 else None)
            y = _acc(y, -s, bpi(k1) if abs(s) > 1e-12 else None)
        emit(n, y)
    for n, m in pairs:
        acc = {(0, "c"): None, (0, "s"): None, (1, "c"): None, (1, "s"): None}
        for k1 in range(h + 1):
            c, s = coefs(n, k1)
            par = k1 % 2
            acc[(par, "c")] = _acc(acc[(par, "c")], c, bpr(k1) if abs(c) > 1e-12 else None)
            acc[(par, "s")] = _acc(acc[(par, "s")], s, bpi(k1) if abs(s) > 1e-12 else None)
        even_m, even_p = _sub(acc[(0, "c")], acc[(0, "s")]), _add(acc[(0, "c")], acc[(0, "s")])
        odd_m, odd_p = _sub(acc[(1, "c")], acc[(1, "s")]), _add(acc[(1, "c")], acc[(1, "s")])
        emit(n, _add(even_m, odd_m))
        emit(m, _sub(even_p, odd_p))


def _next_twiddle(tw, tw_ref):
    row = 0 if tw is None else tw[2] + 1
    return tw_ref[0, row], tw_ref[1, row], row


def _spectrum_kernel(hf_ref, hb_ref, inv_ref, tw1_ref, fh_ref, kf_ref, sd_scr,
                     *, h, n2, nk_pad):
    ct = hf_ref.shape[-1]
    row = lax.broadcasted_iota(jnp.int32, (n2, ct), 0)
    _, pairs = _radix_parts(h)

    def hb(n):
        return jnp.where(row == 0, 0.0, hb_ref[0]) if n == 0 else hb_ref[n]

    for i, (n, m) in enumerate(pairs):
        sd_scr[0, 0, i] = hf_ref[n] + hf_ref[m]
        sd_scr[0, 1, i] = hf_ref[n] - hf_ref[m]
        sd_scr[1, 0, i] = hb(n) + hb(m)
        sd_scr[1, 1, i] = hb(n) - hb(m)

    inv = inv_ref[...]
    tw = None
    for k1 in range(h + 1):
        far, fai = _radix_fwd(k1, h, lambda n: hf_ref[n],
                              lambda i: sd_scr[0, 0, i], lambda i: sd_scr[0, 1, i])
        bar, bai = _radix_fwd(k1, h, hb, lambda i: sd_scr[1, 0, i], lambda i: sd_scr[1, 1, i])
        if k1 > 0:
            tw = _next_twiddle(tw, tw1_ref)
            far, fai = _cmul(far, fai, tw[0], tw[1])
            bar, bai = _cmul(bar, bai, tw[0], tw[1])
        zero = jnp.zeros((n2, ct), F32)
        af = jnp.concatenate([zero if far is None else far, zero if fai is None else fai], axis=0)
        ab = jnp.concatenate([zero if bar is None else bar, zero if bai is None else bai], axis=0)
        a = jnp.concatenate([af, ab], axis=1)
        x = _dot(fh_ref[...], a.astype(BF16))
        lanes = slice(k1 * ct, (k1 + 1) * ct)
        kf_ref[0, :, lanes] = (x[:n2, :ct] + x[:n2, ct:]) * inv
        kf_ref[1, :, lanes] = (x[n2:, :ct] - x[n2:, ct:]) * inv
    if nk_pad > h + 1:
        pad = slice((h + 1) * ct, nk_pad * ct)
        kf_ref[:, :, pad] = jnp.zeros((2, n2, (nk_pad - h - 1) * ct), F32)


def _hyena_kernel(x0_ref, x1_ref, v_ref, ga_ref, *refs, **plan):
    o_ref = refs[10]
    scratch = refs[11:]
    par = scratch[0].shape[0]

    def body(trip, carry):
        for slot in range(par):
            sq = trip * par + slot
            _hyena_sequence(x0_ref.at[sq], x1_ref.at[sq], v_ref.at[sq], ga_ref.at[sq],
                            *refs[:10], o_ref.at[sq], *[s.at[slot] for s in scratch], **plan)
        return carry

    trips = o_ref.shape[0] // par
    if trips == 1:
        body(0, 0)
    else:
        lax.fori_loop(0, trips, body, 0)


def _hyena_sequence(x0_ref, x1_ref, v_ref, ga_ref,
                    w0_ref, w1_ref, wv_ref, b0_ref, b1_ref, bv_ref, skip_ref,
                    kf_ref, tw1_ref, f_ref,
                    o_ref, v_scr, sd_scr, a_scr, b_scr, *, seq, nslab, n2, nk_pad):
    ct = o_ref.shape[-1]
    h = nslab
    chunk = 2 * LANES
    kpc = chunk // ct
    row = lax.broadcasted_iota(jnp.int32, (n2, ct), 0)

    def short_conv(ref, w_ref, b_ref, s, start):
        u = ref[pl.ds(start, n2), :]
        prev = ref[pl.ds(jnp.maximum(start - 1, 0), 1), :]
        prev = jnp.where(s == 0, 0.0, prev)
        nxt = ref[pl.ds(jnp.minimum(start + n2, seq - 1), 1), :]
        nxt = jnp.where(s == nslab - 1, 0.0, nxt)
        up = jnp.where(row == 0, prev, pltpu.roll(u, 1, 0))
        un = jnp.where(row == n2 - 1, nxt, pltpu.roll(u, n2 - 1, 0))
        return b_ref[...] + up * w_ref[0:1, :] + u * w_ref[1:2, :] + un * w_ref[2:3, :]

    def prep(s, carry):
        start = pl.multiple_of(s * n2, n2)
        x0 = short_conv(x0_ref, w0_ref, b0_ref, s, start)
        x1 = short_conv(x1_ref, w1_ref, b1_ref, s, start)
        v_scr[s] = short_conv(v_ref, wv_ref, bv_ref, s, start) * x1
        o_ref[pl.ds(start, n2), :] = x0 * _silu(ga_ref[pl.ds(start, n2), :])
        return carry

    lax.fori_loop(0, nslab, prep, 0)

    _, pairs = _radix_parts(h)
    for i, (n, m) in enumerate(pairs):
        sd_scr[0, i] = v_scr[n] + v_scr[m]
        sd_scr[1, i] = v_scr[n] - v_scr[m]
    zero = jnp.zeros((n2, ct), BF16)
    tw = None
    for k1 in range(h + 1):
        ar, ai = _radix_fwd(k1, h, lambda n: v_scr[n], lambda i: sd_scr[0, i],
                            lambda i: sd_scr[1, i])
        if k1 > 0:
            tw = _next_twiddle(tw, tw1_ref)
            ar, ai = _cmul(ar, ai, tw[0], tw[1])
        lanes = slice(k1 * ct, (k1 + 1) * ct)
        a_scr[0:n2, lanes] = zero if ar is None else ar.astype(BF16)
        a_scr[n2:2 * n2, lanes] = zero if ai is None else ai.astype(BF16)
    for k1 in range(h + 1, nk_pad):
        lanes = slice(k1 * ct, (k1 + 1) * ct)
        a_scr[0:n2, lanes] = zero
        a_scr[n2:2 * n2, lanes] = zero

    def dft(j):
        return _dot(f_ref[...], a_scr[:, j * chunk:(j + 1) * chunk])

    tw = None
    nchunk = nk_pad // kpc
    x_next = dft(0)
    for j in range(nchunk):
        cols = slice(j * chunk, (j + 1) * chunk)
        x = x_next
        if j + 1 < nchunk:
            x_next = dft(j + 1)
        xr, xi = x[:n2], x[n2:]
        kr = kf_ref[0, :, cols]
        ki = kf_ref[1, :, cols]
        z = jnp.concatenate([xr * kr - xi * ki, -(xr * ki + xi * kr)], axis=0)
        bt = _dot(f_ref[...], z.astype(BF16))
        for kk in range(kpc):
            k1 = j * kpc + kk
            if k1 > h:
                continue
            lanes = slice(kk * ct, (kk + 1) * ct)
            br = bt[:n2, lanes]
            bi = -bt[n2:, lanes]
            if k1 > 0:
                tw = _next_twiddle(tw, tw1_ref)
                br, bi = _cmul(br, bi, tw[0], -tw[1])
            b_scr[0, k1] = br
            if k1 not in (0, h):
                b_scr[1, k1] = bi

    def emit(n, y):
        rows = pl.ds(n * n2, n2)
        o_ref[rows, :] = (y + v_scr[n] * skip_ref[...]) * o_ref[rows, :]

    _radix_inv(h, 2 * seq, lambda k1: b_scr[0, k1], lambda k1: b_scr[1, k1], emit)


def _dft_cos_sin(n2, shift=0):
    idx = np.arange(n2)
    ang = 2.0 * np.pi * (((idx[:, None] + shift) * idx[None, :]) % n2) / n2
    return np.cos(ang), np.sin(ang)


def _bf16_split(x):
    x32 = jnp.asarray(x.astype(np.float32))
    hi = x32.astype(BF16)
    lo = (x32 - hi.astype(F32)).astype(BF16)
    return hi, lo


def _dft_block(n2):
    c, s = _dft_cos_sin(n2)
    return np.block([[c, s], [-s, c]])


def _hyena_plan(seq):
    n2 = min(seq, 512)
    n1 = 2 * seq // n2
    ct = LANES if seq > 512 else 2 * LANES
    kpc = 2 * LANES // ct
    nk_pad = -(-(n1 // 2 + 1) // kpc) * kpc
    return n1, n2, ct, nk_pad


def _twiddle1(n1, n2, ct):
    ang = 2.0 * np.pi * np.arange(1, n1 // 2 + 1)[:, None] * np.arange(n2)[None, :] / (n1 * n2)
    tw = np.stack([np.cos(ang), -np.sin(ang)]).astype(np.float32)
    return jnp.broadcast_to(jnp.asarray(tw)[..., None], tw.shape + (ct,))


def _spectrum(h, inv_norm, seq):
    n1, n2, ct, nk_pad = _hyena_plan(seq)
    nslab = n1 // 2
    nct = D_A // ct
    npair = len(_radix_parts(nslab)[1])
    ffwd = _bf16_split(_dft_block(n2))[0]
    h3 = h.reshape(nslab, n2, 2 * D_A)
    fixed2 = lambda c: (0, 0)
    return pl.pallas_call(
        functools.partial(_spectrum_kernel, h=nslab, n2=n2, nk_pad=nk_pad),
        grid=(nct,),
        in_specs=[pl.BlockSpec((nslab, n2, ct), lambda c: (0, 0, c)),
                  pl.BlockSpec((nslab, n2, ct), lambda c: (0, 0, nct + c)),
                  pl.BlockSpec((1, ct), lambda c: (0, c)),
                  pl.BlockSpec((2, nslab, n2, ct), lambda c: (0, 0, 0, 0)),
                  pl.BlockSpec((2 * n2, 2 * n2), fixed2)],
        out_specs=pl.BlockSpec((None, 2, n2, nk_pad * ct), lambda c: (c, 0, 0, 0)),
        out_shape=jax.ShapeDtypeStruct((nct, 2, n2, nk_pad * ct), F32),
        scratch_shapes=[pltpu.VMEM((2, 2, max(npair, 1), n2, ct), F32)],
        compiler_params=_params(("parallel",)),
        name="filter_spectrum",
    )(h3, h3, inv_norm, _twiddle1(n1, n2, ct), ffwd)


def _hyena(ua, ga, conv_w, conv_b, skip, kf):
    bsz, seq, _ = ua.shape
    n1, n2, ct, nk_pad = _hyena_plan(seq)
    nslab = n1 // 2
    nct = D_A // ct
    npair = len(_radix_parts(nslab)[1])
    cb = conv_b.reshape(1, 3 * D_A)
    sk = skip.reshape(1, D_A)
    nseq = max(1, 1024 // seq)
    par = min(nseq, 2)
    part = lambda p: pl.BlockSpec((nseq, seq, ct), lambda c, b: (b, 0, p * nct + c))
    wpart = lambda p: pl.BlockSpec((3, ct), lambda c, b: (0, p * nct + c))
    bpart = lambda p: pl.BlockSpec((1, ct), lambda c, b: (0, p * nct + c))
    return pl.pallas_call(
        functools.partial(_hyena_kernel, seq=seq, nslab=nslab, n2=n2, nk_pad=nk_pad),
        grid=(nct, bsz // nseq),
        in_specs=[part(0), part(1), part(2),
                  pl.BlockSpec((nseq, seq, ct), lambda c, b: (b, 0, c)),
                  wpart(0), wpart(1), wpart(2), bpart(0), bpart(1), bpart(2),
                  pl.BlockSpec((1, ct), lambda c, b: (0, c)),
                  pl.BlockSpec((None, 2, n2, nk_pad * ct), lambda c, b: (c, 0, 0, 0)),
                  pl.BlockSpec((2, nslab, n2, ct), lambda c, b: (0, 0, 0, 0)),
                  pl.BlockSpec((2 * n2, 2 * n2), lambda c, b: (0, 0))],
        out_specs=pl.BlockSpec((nseq, seq, ct), lambda c, b: (b, 0, c)),
        out_shape=jax.ShapeDtypeStruct((bsz, seq, D_A), F32),
        scratch_shapes=[pltpu.VMEM((par, nslab, n2, ct), F32),
                        pltpu.VMEM((par, 2, max(npair, 1), n2, ct), F32),
                        pltpu.VMEM((par, 2 * n2, nk_pad * ct), BF16),
                        pltpu.VMEM((par, 2, nk_pad, n2, ct), F32)],
        compiler_params=_params(("parallel", "parallel")),
        name="hyena",
    )(ua, ua, ua, ga, conv_w, conv_w, conv_w, cb, cb, cb, sk, kf, _twiddle1(n1, n2, ct),
      _bf16_split(_dft_block(n2))[0])


def _hyena_filter_kernel(z_ref, w1_ref, b1_ref, w2_ref, b2_ref, w3_ref, del_ref,
                         o_ref, nrm_ref, *, seq):
    hp = lax.Precision.HIGHEST
    tm = o_ref.shape[0]
    i = pl.program_id(0)
    h = jnp.sin(jnp.dot(z_ref[...], w1_ref[...], precision=hp,
                        preferred_element_type=F32) + b1_ref[...])
    h = jnp.sin(jnp.dot(h, w2_ref[...], precision=hp, preferred_element_type=F32) + b2_ref[...])
    tcol = (i * tm + lax.broadcasted_iota(jnp.int32, (tm, 1), 0)).astype(F32) * (1.0 / (seq - 1))
    h = jnp.dot(h, w3_ref[...], precision=hp, preferred_element_type=F32) * jnp.exp(
        -tcol * del_ref[...])
    o_ref[...] = h

    @pl.when(i == 0)
    def _():
        nrm_ref[...] = jnp.zeros_like(nrm_ref)

    nrm_ref[...] += jnp.sum(jnp.abs(h), axis=0, keepdims=True)


def _hyena_filter(seq, w1, b1, w2, b2, w3):
    bands = (EMB_DIM - 1) // 2
    ang = (2.0 * np.pi * np.arange(seq)[:, None] / seq) * np.linspace(1e-4, bands - 1, bands)
    z = np.zeros((seq, LANES), np.float32)
    z[:, 0] = np.linspace(0.0, 1.0, seq)
    z[:, 1:1 + bands] = np.cos(ang)
    z[:, 1 + bands:EMB_DIM] = -np.sin(ang)
    deltas = np.abs(np.linspace(MIN_DECAY, MAX_DECAY, D_A)).astype(np.float32)
    deltas2 = np.concatenate([deltas, deltas])[None, :]
    w1p = jnp.pad(w1, ((0, LANES - EMB_DIM), (0, 0)))
    tm = min(seq, 1024)
    fo = w1.shape[1]
    fixed = lambda i: (0, 0)
    h, nrm = pl.pallas_call(
        functools.partial(_hyena_filter_kernel, seq=seq),
        grid=(seq // tm,),
        in_specs=[pl.BlockSpec((tm, LANES), lambda i: (i, 0)),
                  pl.BlockSpec((LANES, fo), fixed),
                  pl.BlockSpec((1, fo), fixed),
                  pl.BlockSpec((fo, fo), fixed),
                  pl.BlockSpec((1, fo), fixed),
                  pl.BlockSpec((fo, 2 * D_A), fixed),
                  pl.BlockSpec((1, 2 * D_A), fixed)],
        out_specs=[pl.BlockSpec((tm, 2 * D_A), lambda i: (i, 0)),
                   pl.BlockSpec((1, 2 * D_A), fixed)],
        out_shape=[jax.ShapeDtypeStruct((seq, 2 * D_A), F32),
                   jax.ShapeDtypeStruct((1, 2 * D_A), F32)],
        compiler_params=_params(("arbitrary",)),
        name="hyena_filter",
    )(jnp.asarray(z), w1p, b1.reshape(1, fo), w2, b2.reshape(1, fo), w3, jnp.asarray(deltas2))
    return h, 1.0 / (nrm[:, :D_A] + nrm[:, D_A:])


def _attn_kernel(cq_ref, cos_ref, sin_ref, wa_ref, wb_ref, k_ref, v_ref, gb_ref, o_ref, *, nseq):
    nt = (((1,), (1,)), ((), ()))
    tq = cq_ref.shape[1]
    cos = jnp.concatenate([cos_ref[...]] * N_HEADS, axis=-1)
    sin = jnp.concatenate([sin_ref[...]] * N_HEADS, axis=-1)
    lane = lax.broadcasted_iota(jnp.int32, (tq, 2 * V_DIM), 1)

    def one_sequence(sq):
        cq = cq_ref[sq]
        q = (_dot(cq, wa_ref[...]) * cos + _dot(cq, wb_ref[...]) * sin) * (ATTN_SCALE * LOG2_E)
        q = q.astype(BF16)

        def scores(h):
            hsl = slice(h * HEAD_PAD, (h + 1) * HEAD_PAD)
            return lax.dot_general(q[:, hsl], k_ref[sq, :, hsl], nt, preferred_element_type=F32)

        look = 1 if k_ref.shape[1] > 2 * tq else N_HEADS - 1
        pending = [scores(h) for h in range(look)]
        outs = []
        for h in range(N_HEADS):
            if h + look < N_HEADS:
                pending.append(scores(h + look))
            sb = pending.pop(0).astype(BF16)
            p = jnp.exp2(sb - jnp.max(sb, axis=-1, keepdims=True))
            of = _dot(p, v_ref[sq, :, h * HEAD_PAD:(h + 1) * HEAD_PAD])
            outs.append(of / of[:, V_DIM:V_DIM + 1])
            if h % 2 == 1:
                vsl = slice((h // 2) * 2 * V_DIM, (h // 2 + 1) * 2 * V_DIM)
                o = jnp.where(lane < V_DIM, outs[h - 1], pltpu.roll(outs[h], V_DIM, 1))
                o_ref[sq, :, vsl] = (o * _silu(gb_ref[sq, :, vsl])).astype(BF16)

    if nseq == 1:
        one_sequence(0)
    else:
        def body(sq, carry):
            one_sequence(sq)
            return carry

        lax.fori_loop(0, nseq, body, 0)


def _attention(cq, cos, sin, wa, wb, k, v, gb, tq, nseq=1):
    bsz, lq, _ = cq.shape
    lk = k.shape[1]
    fixed = lambda b, i: (0, 0)
    qrow = lambda b, i: (b, i, 0)
    kv = lambda b, i: (b, 0, 0)
    pos = lambda b, i: (i, 0)
    return pl.pallas_call(
        functools.partial(_attn_kernel, nseq=nseq),
        grid=(bsz // nseq, lq // tq),
        in_specs=[pl.BlockSpec((nseq, tq, Q_RANK), qrow),
                  pl.BlockSpec((tq, LANES), pos),
                  pl.BlockSpec((tq, LANES), pos),
                  pl.BlockSpec((Q_RANK, N_HEADS * HEAD_PAD), fixed),
                  pl.BlockSpec((Q_RANK, N_HEADS * HEAD_PAD), fixed),
                  pl.BlockSpec((nseq, lk, N_HEADS * HEAD_PAD), kv),
                  pl.BlockSpec((nseq, lk, N_HEADS * HEAD_PAD), kv),
                  pl.BlockSpec((nseq, tq, D_B), qrow)],
        out_specs=pl.BlockSpec((nseq, tq, D_B), qrow),
        out_shape=jax.ShapeDtypeStruct((bsz, lq, D_B), BF16),
        compiler_params=_params(("parallel", "arbitrary")),
        name="mla_attention",
    )(cq, cos, sin, wa, wb, k, v, gb)


def _mid_kernel(x_ref, ya_ref, ob_ref, gt_ref, wo_ref, sc_ref, sh_ref, wi_ref,
                x1_ref, u_ref, g_ref):
    y = _dot(ya_ref[...].astype(BF16), wo_ref[0:D_A, :]) + _dot(ob_ref[...], wo_ref[D_A:, :])
    x1 = x_ref[...] + gt_ref[...] * y
    x1_ref[...] = x1
    h = (_rms(x1) * sc_ref[...] + sh_ref[...]).astype(BF16)
    u_ref[...] = _dot(h, wi_ref[:, :D_C]).astype(BF16)
    g_ref[...] = _dot(h, wi_ref[:, D_C:]).astype(BF16)


def _mid(x2d, ya, ob, gt, wo, sc, sh, wi, tiles_per_mod, tm):
    t = x2d.shape[0]
    row = lambda i: (i, 0)
    fixed = lambda i: (0, 0)
    mod = lambda i: (i // tiles_per_mod, 0, 0)
    return pl.pallas_call(
        _mid_kernel,
        grid=(t // tm,),
        in_specs=[pl.BlockSpec((tm, D_MODEL), row),
                  pl.BlockSpec((tm, D_A), row),
                  pl.BlockSpec((tm, D_B), row),
                  pl.BlockSpec((None, 1, D_MODEL), mod),
                  pl.BlockSpec((D_A + D_B, D_MODEL), fixed),
                  pl.BlockSpec((None, 1, D_MODEL), mod),
                  pl.BlockSpec((None, 1, D_MODEL), mod),
                  pl.BlockSpec((D_MODEL, 2 * D_C), fixed)],
        out_specs=[pl.BlockSpec((tm, D_MODEL), row),
                   pl.BlockSpec((tm, D_C), row),
                   pl.BlockSpec((tm, D_C), row)],
        out_shape=[jax.ShapeDtypeStruct((t, D_MODEL), F32),
                   jax.ShapeDtypeStruct((t, D_C), BF16),
                   jax.ShapeDtypeStruct((t, D_C), BF16)],
        compiler_params=_params(("parallel",)),
        name="outproj_even_inproj_odd",
    )(x2d, ya, ob, gt, wo, sc, sh, wi)


def _fnet_weights_kernel(w_ref, cs_ref, o_ref):
    hp = lax.Precision.HIGHEST
    o_ref[...] = jnp.dot(cs_ref[...], w_ref[...], precision=hp,
                         preferred_element_type=F32).astype(BF16)


def _fnet_weights(fnet_w, seq):
    c, s = _dft_cos_sin(G_C)
    cs = jnp.asarray((np.concatenate([c, s], axis=0) / math.sqrt(seq * G_C)).astype(np.float32))
    return pl.pallas_call(
        _fnet_weights_kernel,
        grid=(N_GROUPS_C,),
        in_specs=[pl.BlockSpec((None, G_C, G_C), lambda g: (g, 0, 0)),
                  pl.BlockSpec((2 * G_C, G_C), lambda g: (0, 0))],
        out_specs=pl.BlockSpec((None, 2 * G_C, G_C), lambda g: (g, 0, 0)),
        out_shape=jax.ShapeDtypeStruct((N_GROUPS_C, 2 * G_C, G_C), BF16),
        compiler_params=_params(("parallel",)),
        name="fnet_weights",
    )(fnet_w, cs)


def _fnet_long_kernel(u_ref, tw1_ref, m_ref, ab_ref, o_ref, sd_scr, a_scr, *, n1, n2):
    h = n1 // 2
    ct = o_ref.shape[-1]
    for i, n in enumerate(range(1, h)):
        a = u_ref[n].astype(F32)
        b = u_ref[n1 - n].astype(F32)
        sd_scr[0, i] = a + b
        sd_scr[1, i] = a - b
    x0 = u_ref[0].astype(F32)
    xh = u_ref[h].astype(F32)
    zero = jnp.zeros((n2, ct), BF16)
    tw = None
    for k1 in range(h + 1):
        ar = x0 + xh if k1 % 2 == 0 else x0 - xh
        ai = None
        for i, n in enumerate(range(1, h)):
            th = 2.0 * math.pi * n * k1 / n1
            ar = _acc(ar, math.cos(th), sd_scr[0, i])
            ai = _acc(ai, -math.sin(th), sd_scr[1, i])
        if k1 > 0:
            tw = _next_twiddle(tw, tw1_ref)
            ar, ai = _cmul(ar, ai, tw[0], tw[1])
        lanes = slice(k1 * ct, (k1 + 1) * ct)
        a_scr[0:n2, lanes] = ar.astype(BF16)
        a_scr[n2:2 * n2, lanes] = zero if ai is None else ai.astype(BF16)

    def group_maps(k1, xr, xi):
        for gi in range(ct // G_C):
            gl = slice(gi * G_C, (gi + 1) * G_C)
            y = _dot(jnp.concatenate([xr[:, gl], xi[:, gl]], axis=1).astype(BF16), ab_ref[gi])
            o_ref[k1, :, gl] = y.astype(o_ref.dtype)

    def dft(k1):
        rows = 4 * n2 if 0 < k1 < h else 2 * n2
        return _dot(m_ref[0:rows, :], a_scr[:, k1 * ct:(k1 + 1) * ct])

    x_next = dft(0)
    for k1 in range(h + 1):
        x = x_next
        if k1 < h:
            x_next = dft(k1 + 1)
        group_maps(k1, x[:n2], x[n2:2 * n2])
        if 0 < k1 < h:
            group_maps(n1 - k1, x[2 * n2:3 * n2], x[3 * n2:])


def _fnet_long(u, ab, n1, n2):
    bsz, seq, _ = u.shape
    h = n1 // 2
    ng = 2
    ct = ng * G_C
    c, s = _dft_cos_sin(n2)
    ce, se = _dft_cos_sin(n2, shift=1)
    m = _bf16_split(np.block([[c, s], [-s, c], [ce, -se], [-se, -ce]]))[0]
    out = pl.pallas_call(
        functools.partial(_fnet_long_kernel, n1=n1, n2=n2),
        grid=(bsz, N_GROUPS_C // ng),
        in_specs=[pl.BlockSpec((None, n1, n2, ct), lambda b, g: (b, 0, 0, g)),
                  pl.BlockSpec((2, h, n2, ct), lambda b, g: (0, 0, 0, 0)),
                  pl.BlockSpec((4 * n2, 2 * n2), lambda b, g: (0, 0)),
                  pl.BlockSpec((ng, 2 * G_C, G_C), lambda b, g: (g, 0, 0))],
        out_specs=pl.BlockSpec((None, n1, n2, ct), lambda b, g: (b, 0, 0, g)),
        out_shape=jax.ShapeDtypeStruct((bsz, n1, n2, D_C), BF16),
        scratch_shapes=[pltpu.VMEM((2, h - 1, n2, ct), F32),
                        pltpu.VMEM((2 * n2, (h + 1) * ct), BF16)],
        compiler_params=_params(("parallel", "parallel")),
        name="fnet_long",
    )(u.reshape(bsz, n1, n2, D_C), _twiddle1(n1, n2, ct), m, ab)
    return out.transpose(0, 2, 1, 3).reshape(bsz, seq, D_C)


def _fnet_short_kernel(u_ref, f_ref, ab_ref, o_ref, *, seq, scale):
    def body(sq, carry):
        x = _dot(f_ref[...], u_ref[sq])
        xr, xi = x[:seq], x[seq:]
        for g in range(N_GROUPS_C):
            sl = slice(g * G_C, (g + 1) * G_C)
            xin = jnp.concatenate([xr[:, sl], xi[:, sl]], axis=1).astype(BF16)
            o_ref[sq, :, sl] = (_dot(xin, ab_ref[g]) * scale).astype(o_ref.dtype)
        return carry

    lax.fori_loop(0, u_ref.shape[0], body, 0)


def _fnet_short(u, ab, scale, nseq=4):
    bsz, seq, _ = u.shape
    c, s = _dft_cos_sin(seq)
    f = _bf16_split(np.concatenate([c, -s], axis=0))[0]
    return pl.pallas_call(
        functools.partial(_fnet_short_kernel, seq=seq, scale=scale),
        grid=(bsz // nseq,),
        in_specs=[pl.BlockSpec((nseq, seq, D_C), lambda b: (b, 0, 0)),
                  pl.BlockSpec((2 * seq, seq), lambda b: (0, 0)),
                  pl.BlockSpec((N_GROUPS_C, 2 * G_C, G_C), lambda b: (0, 0, 0))],
        out_specs=pl.BlockSpec((nseq, seq, D_C), lambda b: (b, 0, 0)),
        out_shape=jax.ShapeDtypeStruct((bsz, seq, D_C), BF16),
        compiler_params=_params(("parallel",)),
        name="fnet_short",
    )(u, f, ab)


def _final_kernel(x1_ref, y_ref, g_ref, gt_ref, wo_ref, fg_ref, o_ref):
    z = (y_ref[...].astype(F32) * _silu(g_ref[...].astype(F32))).astype(BF16)
    x2 = x1_ref[...] + gt_ref[...] * _dot(z, wo_ref[...])
    o_ref[...] = _rms(x2) * fg_ref[...]


def _final(x1, y, g, gt, wo, fg, tiles_per_mod, tm):
    t = x1.shape[0]
    row = lambda i: (i, 0)
    fixed = lambda i: (0, 0)
    mod = lambda i: (i // tiles_per_mod, 0, 0)
    return pl.pallas_call(
        _final_kernel,
        grid=(t // tm,),
        in_specs=[pl.BlockSpec((tm, D_MODEL), row),
                  pl.BlockSpec((tm, D_C), row),
                  pl.BlockSpec((tm, D_C), row),
                  pl.BlockSpec((None, 1, D_MODEL), mod),
                  pl.BlockSpec((D_C, D_MODEL), fixed),
                  pl.BlockSpec((1, D_MODEL), fixed)],
        out_specs=pl.BlockSpec((tm, D_MODEL), row),
        out_shape=jax.ShapeDtypeStruct((t, D_MODEL), F32),
        compiler_params=_params(("parallel",)),
        name="outproj_odd_final",
    )(x1, y, g, gt, wo, fg)


def _rope_swap(w):
    nf = ROPE_DIM // 4
    w4 = w.reshape(w.shape[:-1] + (2, 2, nf))
    return jnp.stack([-w4[..., 1, :], w4[..., 0, :]], axis=-2).reshape(w.shape)


def _pack_w_in_even(w):
    ua_ga_cq_ckv = w[:, :4 * D_A + Q_RANK + KV_RANK]
    o = 4 * D_A + Q_RANK + KV_RANK
    krope = w[:, o:o + ROPE_DIM]
    gb = w[:, o + ROPE_DIM:]
    zpad = jnp.zeros((w.shape[0], LANES - ROPE_DIM), w.dtype)
    return jnp.concatenate([ua_ga_cq_ckv, gb, krope, zpad, _rope_swap(krope), zpad],
                           axis=1).astype(BF16)


def _pack_w_uq(w):
    w3 = w.reshape(Q_RANK, N_HEADS, QK_DIM)
    zq = jnp.zeros((Q_RANK, N_HEADS, HEAD_PAD - QK_DIM), w.dtype)
    wa = jnp.concatenate([w3, zq], axis=-1)
    zn = jnp.zeros((Q_RANK, N_HEADS, NOPE_DIM), w.dtype)
    wb = jnp.concatenate([zn, _rope_swap(w3[..., NOPE_DIM:]), zq], axis=-1)
    shape = (Q_RANK, N_HEADS * HEAD_PAD)
    return wa.reshape(shape).astype(BF16), wb.reshape(shape).astype(BF16)


def _pack_w_ukv(w):
    w3 = w.reshape(KV_RANK, N_HEADS, NOPE_DIM + V_DIM)
    zk = jnp.zeros((KV_RANK, N_HEADS, HEAD_PAD - NOPE_DIM), w.dtype)
    wk_c = jnp.concatenate([w3[..., :NOPE_DIM], zk], axis=-1).reshape(KV_RANK, -1)
    eye = jnp.eye(LANES, ROPE_DIM, dtype=w.dtype)
    rope_rows = jnp.concatenate([jnp.zeros((LANES, NOPE_DIM), w.dtype), eye,
                                 jnp.zeros((LANES, HEAD_PAD - QK_DIM), w.dtype)], axis=-1)
    wk_r = jnp.tile(rope_rows, (1, N_HEADS))
    wk = jnp.concatenate([wk_c, wk_r], axis=0)
    zv = jnp.zeros((KV_RANK, N_HEADS, HEAD_PAD - V_DIM), w.dtype)
    wv = jnp.concatenate([w3[..., NOPE_DIM:], zv], axis=-1).reshape(KV_RANK, -1)
    ones = np.zeros((1, N_HEADS, HEAD_PAD), np.float32)
    ones[:, :, V_DIM] = 1.0
    return wk.astype(BF16), wv.astype(BF16), jnp.asarray(ones.reshape(1, -1))


def _rope_tables(seq, rotate):
    pos = np.arange(seq)
    half = ROPE_DIM // 2
    inv = ROPE_BASE ** (-np.arange(0, half, 2, dtype=np.float64) / half)
    r = (pos // GRID_W)[:, None] * inv
    c = (pos % GRID_W)[:, None] * inv
    ang = np.concatenate([r, r, c, c], axis=-1) * (1.0 if rotate else 0.0)
    cos_q = np.zeros((seq, LANES))
    sin_q = np.zeros((seq, LANES))
    cos_k = np.zeros((seq, LANES))
    sin_k = np.zeros((seq, LANES))
    cos_q[:, :NOPE_DIM] = 1.0
    cos_q[:, NOPE_DIM:QK_DIM] = np.cos(ang)
    sin_q[:, NOPE_DIM:QK_DIM] = np.sin(ang)
    cos_k[:, :ROPE_DIM] = np.cos(ang)
    sin_k[:, :ROPE_DIM] = np.sin(ang)
    return tuple(jnp.asarray(t.astype(np.float32)) for t in (cos_q, sin_q, cos_k, sin_k))


def kernel(x_prompt, x_sample, c, cache_ckv, cache_krope, c_ctx, norm_g, w_mod, b_mod,
           w_in_e, conv_w, conv_b, filt_w1, filt_b1, filt_w2, filt_b2, filt_w3, hyena_skip,
           q_norm_g, kv_norm_g, w_uq, w_ukv, w_out_e, w_in_o, fnet_w, w_out_o, final_g):
    nb_p, seq_p, _ = x_prompt.shape
    nb_s, seq_s, _ = x_sample.shape
    tp = nb_p * seq_p
    ts = nb_s * seq_s
    tm = 512

    cond = jnp.concatenate([c_ctx[None, :], c, jnp.zeros((8 - 1 - nb_s, D_MODEL), F32)], axis=0)
    mod = _ada_params(cond, w_mod, b_mod)
    shift = mod[:, :, None, :D_MODEL]
    scale = (1.0 + mod[:, :, None, D_MODEL:2 * D_MODEL]) * norm_g[:, None, None, :]
    gate = mod[:, :, None, 2 * D_MODEL:]

    xp = x_prompt.reshape(tp, D_MODEL)
    xs = x_sample.reshape(ts, D_MODEL)

    w_in = _pack_w_in_even(w_in_e[0])
    qg = q_norm_g[0].reshape(1, Q_RANK)
    kvg = kv_norm_g[0].reshape(1, KV_RANK)
    wq_a, wq_b = _pack_w_uq(w_uq[0])
    wk, wv, vone = _pack_w_ukv(w_ukv[0])
    cos_qp, sin_qp, cos_kp, sin_kp = _rope_tables(tm, rotate=False)
    cos_qs, sin_qs, cos_ks, sin_ks = _rope_tables(seq_s, rotate=True)
    pr_p = _inproj_even(xp, scale[0, :1], shift[0, :1], w_in, qg, kvg, cos_kp, sin_kp, wk, wv,
                        vone, tm, True)
    pr_s = _inproj_even(xs, scale[0, 1:1 + nb_s], shift[0, 1:1 + nb_s], w_in, qg, kvg,
                        cos_ks, sin_ks, wk, wv, vone, tm, False,
                        cache=(cache_ckv, cache_krope, 0))
    ua_p, ga_p, cq_p, gb_p, k_p, v_p, ckv_p, kra_p = pr_p
    ua_s, ga_s, cq_s, gb_s, k_s, v_s = pr_s

    filt = (filt_w1[0], filt_b1[0], filt_w2[0], filt_b2[0], filt_w3[0])
    ya = []
    for ua, ga, nb, seq in ((ua_p, ga_p, nb_p, seq_p), (ua_s, ga_s, nb_s, seq_s)):
        h, inv_norm = _hyena_filter(seq, *filt)
        kf = _spectrum(h, inv_norm, seq)
        ya.append(_hyena(ua.reshape(nb, seq, 3 * D_A), ga.reshape(nb, seq, D_A),
                         conv_w[0], conv_b[0], hyena_skip[0], kf))
    ya_p = ya[0].reshape(tp, D_A)
    ya_s = ya[1].reshape(ts, D_A)

    ob_p = _attention(cq_p.reshape(nb_p, seq_p, Q_RANK), cos_qp, sin_qp, wq_a, wq_b,
                      k_p.reshape(nb_p, seq_p, -1), v_p.reshape(nb_p, seq_p, -1),
                      gb_p.reshape(nb_p, seq_p, D_B), seq_p, nseq=4)
    ob_s = _attention(cq_s.reshape(nb_s, seq_s, Q_RANK), cos_qs, sin_qs, wq_a, wq_b, k_s, v_s,
                      gb_s.reshape(nb_s, seq_s, D_B), 512)

    wo_e = w_out_e[0].astype(BF16)
    wi_o = w_in_o[0].astype(BF16)
    tw = 2 * tm
    x1_p, u_p, g_p = _mid(xp, ya_p, ob_p.reshape(tp, D_B), gate[0, :1], wo_e,
                          scale[1, :1], shift[1, :1], wi_o, tp // tw, tw)
    x1_s, u_s, g_s = _mid(xs, ya_s, ob_s.reshape(ts, D_B), gate[0, 1:1 + nb_s], wo_e,
                          scale[1, 1:1 + nb_s], shift[1, 1:1 + nb_s], wi_o, seq_s // tw, tw)

    ab = _fnet_weights(fnet_w[0], seq_s)
    y_p = _fnet_short(u_p.reshape(nb_p, seq_p, D_C), ab, math.sqrt(seq_s / seq_p))
    y_s = _fnet_long(u_s.reshape(nb_s, seq_s, D_C), ab, 16, seq_s // 16)
    wo_o = w_out_o[0].astype(BF16)
    fg = final_g.reshape(1, D_MODEL)
    out_p = _final(x1_p, y_p.reshape(tp, D_C), g_p, gate[1, :1], wo_o, fg, tp // tw, tw)
    out_s = _final(x1_s, y_s.reshape(ts, D_C), g_s, gate[1, 1:1 + nb_s], wo_o, fg,
                   seq_s // tw, tw)

    state_ckv = ckv_p.reshape(nb_p, 1, seq_p, KV_RANK)
    state_krope = kra_p[:, :ROPE_DIM].reshape(nb_p, 1, seq_p, ROPE_DIM)
    return (out_p.reshape(nb_p, seq_p, D_MODEL), out_s.reshape(nb_s, seq_s, D_MODEL),
            state_ckv, state_krope)
```

```python
import functools
import math

import numpy as np
import jax
import jax.numpy as jnp
from jax import lax
from jax.experimental import pallas as pl
from jax.experimental.pallas import tpu as pltpu

F32 = jnp.float32
BF16 = jnp.bfloat16

D_MODEL = 1024
GRID_W = 64
EPS = 1e-6
D_A = 512
EMB_DIM = 33
DECAY_TARGET = 1e-2
MIN_DECAY = math.log(DECAY_TARGET) / 1.5
MAX_DECAY = math.log(DECAY_TARGET) / 0.3
N_HEADS = 8
Q_RANK = 256
KV_RANK = 256
NOPE_DIM = 64
ROPE_DIM = 32
V_DIM = 64
ROPE_BASE = 10000.0
D_B = N_HEADS * V_DIM
QK_DIM = NOPE_DIM + ROPE_DIM
ATTN_SCALE = 1.0 / math.sqrt(QK_DIM)
LOG2_E = math.log2(math.e)
D_C = 1024
N_GROUPS_C = 8
G_C = D_C // N_GROUPS_C

LANES = 128
HEAD_PAD = 128
VMEM_LIMIT = 56 * 1024 * 1024

_C_UA, _C_GA, _C_CQ, _C_CKV, _C_GB, _C_KRA, _C_KRB, _C_END = (
    0, 1536, 2048, 2304, 2560, 3072, 3200, 3328)


def _params(sem):
    return pltpu.CompilerParams(dimension_semantics=sem, vmem_limit_bytes=VMEM_LIMIT)


def _rms(x):
    return x * lax.rsqrt(jnp.mean(x * x, axis=-1, keepdims=True) + EPS)


def _silu(g):
    return g * jax.nn.sigmoid(g)


def _dot(a, b):
    return jnp.dot(a, b, preferred_element_type=F32)


def _ada_kernel(cond_ref, w_ref, b_ref, o_ref):
    c = _silu(cond_ref[...])
    o_ref[...] = jnp.dot(c, w_ref[...], preferred_element_type=F32,
                         precision=lax.Precision.HIGHEST) + b_ref[...]


def _ada_params(cond, w_mod, b_mod):
    depth = w_mod.shape[0]
    n = cond.shape[0]
    tn = 1024
    return pl.pallas_call(
        _ada_kernel,
        grid=(depth, 3 * D_MODEL // tn),
        in_specs=[pl.BlockSpec((n, D_MODEL), lambda i, j: (0, 0)),
                  pl.BlockSpec((None, D_MODEL, tn), lambda i, j: (i, 0, j)),
                  pl.BlockSpec((None, 1, tn), lambda i, j: (i, 0, j))],
        out_specs=pl.BlockSpec((None, n, tn), lambda i, j: (i, 0, j)),
        out_shape=jax.ShapeDtypeStruct((depth, n, 3 * D_MODEL), F32),
        compiler_params=_params(("arbitrary", "arbitrary")),
        name="ada_params",
    )(cond, w_mod, b_mod.reshape(depth, 1, 3 * D_MODEL))


def _inproj_even_kernel(x_ref, sc_ref, sh_ref, w_ref, qg_ref, kvg_ref, cos_ref, sin_ref,
                        wk_ref, wv_ref, vone_ref, *refs, nctx, emit_state):
    if nctx:
        cckv_ref, ckr_ref = refs[:2]
        refs = refs[2:]
    ua_ref, ga_ref, cq_ref, gb_ref, k_ref, v_ref = refs[:6]
    state_refs = refs[6:]

    def tokens():
        h = (_rms(x_ref[...]) * sc_ref[...] + sh_ref[...]).astype(BF16)

        def proj(a, b):
            return _dot(h, w_ref[:, a:b])

        ua_ref[...] = proj(_C_UA, _C_GA)
        ga_ref[...] = proj(_C_GA, _C_CQ)
        cq_ref[...] = (_rms(proj(_C_CQ, _C_CKV)) * qg_ref[...]).astype(BF16)
        gb_ref[...] = proj(_C_GB, _C_KRA)
        ckv = _rms(proj(_C_CKV, _C_GB)) * kvg_ref[...]
        kr2 = proj(_C_KRA, _C_END)
        kra = kr2[:, :LANES]
        krb = kr2[:, LANES:]
        c = ckv.astype(BF16)
        kr = pltpu.roll(kra * cos_ref[...] + krb * sin_ref[...], NOPE_DIM, 1)
        k = _dot(c, wk_ref[0:KV_RANK, :]) + jnp.concatenate([kr] * N_HEADS, axis=-1)
        k_ref[...] = k.astype(BF16)
        v_ref[...] = (_dot(c, wv_ref[...]) + vone_ref[...]).astype(BF16)
        if emit_state:
            state_refs[0][...] = ckv
            state_refs[1][...] = kra

    if not nctx:
        tokens()
        return

    t = pl.program_id(1)

    @pl.when(t < nctx)
    def _():
        c = cckv_ref[...].astype(BF16)
        kr = ckr_ref[...].astype(BF16)
        k_ref[...] = (_dot(c, wk_ref[0:KV_RANK, :])
                      + _dot(kr, wk_ref[KV_RANK:KV_RANK + ROPE_DIM, :])).astype(BF16)
        v_ref[...] = (_dot(c, wv_ref[...]) + vone_ref[...]).astype(BF16)

    pl.when(t >= nctx)(tokens)


def _inproj_even(x2d, sc, sh, w, qg, kvg, cos_k, sin_k, wk, wv, vone, tm, emit_state,
                 cache=None):
    t = x2d.shape[0]
    nb = sc.shape[0]
    tiles = t // nb // tm
    tiles_per_seq = cos_k.shape[0] // tm
    nctx = 0 if cache is None else cache[0].shape[2] // tm
    tok = lambda b, s: b * tiles + jnp.maximum(s - nctx, 0)
    row = lambda b, s: (tok(b, s), 0)
    fixed = lambda b, s: (0, 0)
    mod = lambda b, s: (b, 0, 0)
    pos = lambda b, s: (jnp.maximum(s - nctx, 0) % tiles_per_seq, 0)
    kvo = lambda b, s: (b, s, 0)
    widths = (3 * D_A, D_A, Q_RANK, D_B)
    dtypes = (F32, F32, BF16, F32)
    in_specs = [pl.BlockSpec((tm, D_MODEL), row),
                pl.BlockSpec((None, 1, D_MODEL), mod),
                pl.BlockSpec((None, 1, D_MODEL), mod),
                pl.BlockSpec((D_MODEL, _C_END), fixed),
                pl.BlockSpec((1, Q_RANK), fixed),
                pl.BlockSpec((1, KV_RANK), fixed),
                pl.BlockSpec((tm, LANES), pos),
                pl.BlockSpec((tm, LANES), pos),
                pl.BlockSpec((KV_RANK + LANES, N_HEADS * HEAD_PAD), fixed),
                pl.BlockSpec((KV_RANK, N_HEADS * HEAD_PAD), fixed),
                pl.BlockSpec((1, N_HEADS * HEAD_PAD), fixed)]
    args = [x2d, sc, sh, w, qg, kvg, cos_k, sin_k, wk, wv, vone]
    if nctx:
        layer = cache[2]
        ctx = lambda b, s: (b, layer, jnp.minimum(s, nctx - 1), 0)
        in_specs += [pl.BlockSpec((None, None, tm, KV_RANK), ctx),
                     pl.BlockSpec((None, None, tm, ROPE_DIM), ctx)]
        args += [cache[0], cache[1]]
    out_specs = [pl.BlockSpec((tm, wd), row) for wd in widths]
    out_shape = [jax.ShapeDtypeStruct((t, wd), dt) for wd, dt in zip(widths, dtypes)]
    for wd in (N_HEADS * HEAD_PAD, N_HEADS * HEAD_PAD):
        out_specs.append(pl.BlockSpec((None, tm, wd), kvo))
        out_shape.append(jax.ShapeDtypeStruct((nb, (nctx + tiles) * tm, wd), BF16))
    if emit_state:
        for wd in (KV_RANK, LANES):
            out_specs.append(pl.BlockSpec((tm, wd), row))
            out_shape.append(jax.ShapeDtypeStruct((t, wd), F32))
    return pl.pallas_call(
        functools.partial(_inproj_even_kernel, nctx=nctx, emit_state=emit_state),
        grid=(nb, nctx + tiles),
        in_specs=in_specs,
        out_specs=out_specs,
        out_shape=out_shape,
        compiler_params=_params(("parallel", "arbitrary")),
        name="inproj_even",
    )(*args)


def _acc(acc, coef, x):
    if x is None or abs(coef) < 1e-12:
        return acc
    if abs(abs(coef) - 1.0) < 1e-12:
        if acc is None:
            return x if coef > 0 else -x
        return acc + x if coef > 0 else acc - x
    return coef * x if acc is None else acc + coef * x


def _add(a, b):
    if a is None:
        return b
    return a if b is None else a + b


def _sub(a, b):
    if b is None:
        return a
    return -b if a is None else a - b


def _cmul(ar, ai, br, bi):
    def mul(x, y):
        return None if x is None or y is None else x * y
    return _sub(mul(ar, br), mul(ai, bi)), _add(mul(ar, bi), mul(ai, br))


def _radix_parts(h):
    singles = [0] + ([h // 2] if h >= 2 and h % 2 == 0 else [])
    pairs = [(n, h - n) for n in range(1, (h + 1) // 2)]
    return singles, pairs


def _radix_fwd(k1, h, single, pair_sum, pair_diff):
    singles, pairs = _radix_parts(h)
    ar = ai = None
    for n in singles:
        th = 2.0 * math.pi * n * k1 / (2 * h)
        x = single(n)
        ar = _acc(ar, math.cos(th), x)
        ai = _acc(ai, -math.sin(th), x)
    for i, (n, _) in enumerate(pairs):
        th = 2.0 * math.pi * n * k1 / (2 * h)
        re_src, im_src = (pair_sum, pair_diff) if k1 % 2 == 0 else (pair_diff, pair_sum)
        ar = _acc(ar, math.cos(th), re_src(i))
        ai = _acc(ai, -math.sin(th), im_src(i))
    return ar, ai


def _radix_inv(h, n_total, bpr, bpi, emit):
    singles, pairs = _radix_parts(h)

    def coefs(n, k1):
        th = 2.0 * math.pi * n * k1 / (2 * h)
        w = (1.0 if k1 in (0, h) else 2.0) / n_total
        return w * math.cos(th), w * math.sin(th)

    for n in singles:
        y = None
        for k1 in range(h + 1):
            c, s = coefs(n, k1)
            y = _acc(y, c, bpr(k1) if abs(c) > 1e-12 else None)
            y = _acc(y, -s, bpi(k1) if abs(s) > 1e-12 else None)
        emit(n, y)
    for n, m in pairs:
        acc = {(0, "c"): None, (0, "s"): None, (1, "c"): None, (1, "s"): None}
        for k1 in range(h + 1):
            c, s = coefs(n, k1)
            par = k1 % 2
            acc[(par, "c")] = _acc(acc[(par, "c")], c, bpr(k1) if abs(c) > 1e-12 else None)
            acc[(par, "s")] = _acc(acc[(par, "s")], s, bpi(k1) if abs(s) > 1e-12 else None)
        even_m, even_p = _sub(acc[(0, "c")], acc[(0, "s")]), _add(acc[(0, "c")], acc[(0, "s")])
        odd_m, odd_p = _sub(acc[(1, "c")], acc[(1, "s")]), _add(acc[(1, "c")], acc[(1, "s")])
        emit(n, _add(even_m, odd_m))
        emit(m, _sub(even_p, odd_p))


def _next_twiddle(tw, tw_ref):
    row = 0 if tw is None else tw[2] + 1
    return tw_ref[0, row], tw_ref[1, row], row


def _spectrum_kernel(hf_ref, hb_ref, inv_ref, tw1_ref, fh_ref, kf_ref, sd_scr,
                     *, h, n2, nk_pad):
    ct = hf_ref.shape[-1]
    row = lax.broadcasted_iota(jnp.int32, (n2, ct), 0)
    _, pairs = _radix_parts(h)

    def hb(n):
        return jnp.where(row == 0, 0.0, hb_ref[0]) if n == 0 else hb_ref[n]

    for i, (n, m) in enumerate(pairs):
        sd_scr[0, 0, i] = hf_ref[n] + hf_ref[m]
        sd_scr[0, 1, i] = hf_ref[n] - hf_ref[m]
        sd_scr[1, 0, i] = hb(n) + hb(m)
        sd_scr[1, 1, i] = hb(n) - hb(m)

    inv = inv_ref[...]
    tw = None
    for k1 in range(h + 1):
        far, fai = _radix_fwd(k1, h, lambda n: hf_ref[n],
                              lambda i: sd_scr[0, 0, i], lambda i: sd_scr[0, 1, i])
        bar, bai = _radix_fwd(k1, h, hb, lambda i: sd_scr[1, 0, i], lambda i: sd_scr[1, 1, i])
        if k1 > 0:
            tw = _next_twiddle(tw, tw1_ref)
            far, fai = _cmul(far, fai, tw[0], tw[1])
            bar, bai = _cmul(bar, bai, tw[0], tw[1])
        zero = jnp.zeros((n2, ct), F32)
        af = jnp.concatenate([zero if far is None else far, zero if fai is None else fai], axis=0)
        ab = jnp.concatenate([zero if bar is None else bar, zero if bai is None else bai], axis=0)
        a = jnp.concatenate([af, ab], axis=1)
        x = _dot(fh_ref[...], a.astype(BF16))
        lanes = slice(k1 * ct, (k1 + 1) * ct)
        kf_ref[0, :, lanes] = (x[:n2, :ct] + x[:n2, ct:]) * inv
        kf_ref[1, :, lanes] = (x[n2:, :ct] - x[n2:, ct:]) * inv
    if nk_pad > h + 1:
        pad = slice((h + 1) * ct, nk_pad * ct)
        kf_ref[:, :, pad] = jnp.zeros((2, n2, (nk_pad - h - 1) * ct), F32)


def _hyena_kernel(x0_ref, x1_ref, v_ref, ga_ref, *refs, **plan):
    o_ref = refs[10]
    scratch = refs[11:]
    par = scratch[0].shape[0]

    def body(trip, carry):
        for slot in range(par):
            sq = trip * par + slot
            _hyena_sequence(x0_ref.at[sq], x1_ref.at[sq], v_ref.at[sq], ga_ref.at[sq],
                            *refs[:10], o_ref.at[sq], *[s.at[slot] for s in scratch], **plan)
        return carry

    trips = o_ref.shape[0] // par
    if trips == 1:
        body(0, 0)
    else:
        lax.fori_loop(0, trips, body, 0)


def _hyena_sequence(x0_ref, x1_ref, v_ref, ga_ref,
                    w0_ref, w1_ref, wv_ref, b0_ref, b1_ref, bv_ref, skip_ref,
                    kf_ref, tw1_ref, f_ref,
                    o_ref, v_scr, sd_scr, a_scr, b_scr, *, seq, nslab, n2, nk_pad):
    ct = o_ref.shape[-1]
    h = nslab
    chunk = 2 * LANES
    kpc = chunk // ct
    row = lax.broadcasted_iota(jnp.int32, (n2, ct), 0)

    def short_conv(ref, w_ref, b_ref, s, start):
        u = ref[pl.ds(start, n2), :]
        if isinstance(s, int):
            zero_row = jnp.zeros((1, ct), F32)
            prev = zero_row if s == 0 else ref[pl.ds(start - 1, 1), :]
            nxt = zero_row if s == nslab - 1 else ref[pl.ds(start + n2, 1), :]
        else:
            prev = ref[pl.ds(jnp.maximum(start - 1, 0), 1), :]
            prev = jnp.where(s == 0, 0.0, prev)
            nxt = ref[pl.ds(jnp.minimum(start + n2, seq - 1), 1), :]
            nxt = jnp.where(s == nslab - 1, 0.0, nxt)
        up = jnp.where(row == 0, prev, pltpu.roll(u, 1, 0))
        un = jnp.where(row == n2 - 1, nxt, pltpu.roll(u, n2 - 1, 0))
        return b_ref[...] + up * w_ref[0:1, :] + u * w_ref[1:2, :] + un * w_ref[2:3, :]

    def prep(s, carry):
        start = pl.multiple_of(s * n2, n2)
        x0 = short_conv(x0_ref, w0_ref, b0_ref, s, start)
        x1 = short_conv(x1_ref, w1_ref, b1_ref, s, start)
        v_scr[s] = short_conv(v_ref, wv_ref, bv_ref, s, start) * x1
        o_ref[pl.ds(start, n2), :] = x0 * _silu(ga_ref[pl.ds(start, n2), :])
        return carry

    lax.fori_loop(0, nslab, prep, 0)

    _, pairs = _radix_parts(h)
    for i, (n, m) in enumerate(pairs):
        sd_scr[0, i] = v_scr[n] + v_scr[m]
        sd_scr[1, i] = v_scr[n] - v_scr[m]
    zero = jnp.zeros((n2, ct), BF16)
    tw = None
    for k1 in range(h + 1):
        ar, ai = _radix_fwd(k1, h, lambda n: v_scr[n], lambda i: sd_scr[0, i],
                            lambda i: sd_scr[1, i])
        if k1 > 0:
            tw = _next_twiddle(tw, tw1_ref)
            ar, ai = _cmul(ar, ai, tw[0], tw[1])
        lanes = slice(k1 * ct, (k1 + 1) * ct)
        a_scr[0:n2, lanes] = zero if ar is None else ar.astype(BF16)
        a_scr[n2:2 * n2, lanes] = zero if ai is None else ai.astype(BF16)
    for k1 in range(h + 1, nk_pad):
        lanes = slice(k1 * ct, (k1 + 1) * ct)
        a_scr[0:n2, lanes] = zero
        a_scr[n2:2 * n2, lanes] = zero

    def dft(j):
        return _dot(f_ref[...], a_scr[:, j * chunk:(j + 1) * chunk])

    tw = None
    nchunk = nk_pad // kpc
    x_next = dft(0)
    for j in range(nchunk):
        cols = slice(j * chunk, (j + 1) * chunk)
        x = x_next
        if j + 1 < nchunk:
            x_next = dft(j + 1)
        xr, xi = x[:n2], x[n2:]
        kr = kf_ref[0, :, cols]
        ki = kf_ref[1, :, cols]
        z = jnp.concatenate([xr * kr - xi * ki, -(xr * ki + xi * kr)], axis=0)
        bt = _dot(f_ref[...], z.astype(BF16))
        for kk in range(kpc):
            k1 = j * kpc + kk
            if k1 > h:
                continue
            lanes = slice(kk * ct, (kk + 1) * ct)
            br = bt[:n2, lanes]
            bi = -bt[n2:, lanes]
            if k1 > 0:
                tw = _next_twiddle(tw, tw1_ref)
                br, bi = _cmul(br, bi, tw[0], -tw[1])
            b_scr[0, k1] = br
            if k1 not in (0, h):
                b_scr[1, k1] = bi

    def emit(n, y):
        rows = pl.ds(n * n2, n2)
        o_ref[rows, :] = (y + v_scr[n] * skip_ref[...]) * o_ref[rows, :]

    _radix_inv(h, 2 * seq, lambda k1: b_scr[0, k1], lambda k1: b_scr[1, k1], emit)


def _dft_cos_sin(n2, shift=0):
    idx = np.arange(n2)
    ang = 2.0 * np.pi * (((idx[:, None] + shift) * idx[None, :]) % n2) / n2
    return np.cos(ang), np.sin(ang)


def _bf16_split(x):
    x32 = jnp.asarray(x.astype(np.float32))
    hi = x32.astype(BF16)
    lo = (x32 - hi.astype(F32)).astype(BF16)
    return hi, lo


def _dft_block(n2):
    c, s = _dft_cos_sin(n2)
    return np.block([[c, s], [-s, c]])


def _hyena_plan(seq):
    n2 = min(seq, 512)
    n1 = 2 * seq // n2
    ct = LANES if seq > 512 else 2 * LANES
    kpc = 2 * LANES // ct
    nk_pad = -(-(n1 // 2 + 1) // kpc) * kpc
    return n1, n2, ct, nk_pad


def _twiddle1(n1, n2, ct):
    ang = 2.0 * np.pi * np.arange(1, n1 // 2 + 1)[:, None] * np.arange(n2)[None, :] / (n1 * n2)
    tw = np.stack([np.cos(ang), -np.sin(ang)]).astype(np.float32)
    return jnp.broadcast_to(jnp.asarray(tw)[..., None], tw.shape + (ct,))


def _spectrum(h, inv_norm, seq):
    n1, n2, ct, nk_pad = _hyena_plan(seq)
    nslab = n1 // 2
    nct = D_A // ct
    npair = len(_radix_parts(nslab)[1])
    ffwd = _bf16_split(_dft_block(n2))[0]
    h3 = h.reshape(nslab, n2, 2 * D_A)
    fixed2 = lambda c: (0, 0)
    return pl.pallas_call(
        functools.partial(_spectrum_kernel, h=nslab, n2=n2, nk_pad=nk_pad),
        grid=(nct,),
        in_specs=[pl.BlockSpec((nslab, n2, ct), lambda c: (0, 0, c)),
                  pl.BlockSpec((nslab, n2, ct), lambda c: (0, 0, nct + c)),
                  pl.BlockSpec((1, ct), lambda c: (0, c)),
                  pl.BlockSpec((2, nslab, n2, ct), lambda c: (0, 0, 0, 0)),
                  pl.BlockSpec((2 * n2, 2 * n2), fixed2)],
        out_specs=pl.BlockSpec((None, 2, n2, nk_pad * ct), lambda c: (c, 0, 0, 0)),
        out_shape=jax.ShapeDtypeStruct((nct, 2, n2, nk_pad * ct), F32),
        scratch_shapes=[pltpu.VMEM((2, 2, max(npair, 1), n2, ct), F32)],
        compiler_params=_params(("parallel",)),
        name="filter_spectrum",
    )(h3, h3, inv_norm, _twiddle1(n1, n2, ct), ffwd)


def _hyena(ua, ga, conv_w, conv_b, skip, kf):
    bsz, seq, _ = ua.shape
    n1, n2, ct, nk_pad = _hyena_plan(seq)
    nslab = n1 // 2
    nct = D_A // ct
    npair = len(_radix_parts(nslab)[1])
    cb = conv_b.reshape(1, 3 * D_A)
    sk = skip.reshape(1, D_A)
    nseq = max(1, 1024 // seq)
    par = min(nseq, 2)
    part = lambda p: pl.BlockSpec((nseq, seq, ct), lambda c, b: (b, 0, p * nct + c))
    wpart = lambda p: pl.BlockSpec((3, ct), lambda c, b: (0, p * nct + c))
    bpart = lambda p: pl.BlockSpec((1, ct), lambda c, b: (0, p * nct + c))
    return pl.pallas_call(
        functools.partial(_hyena_kernel, seq=seq, nslab=nslab, n2=n2, nk_pad=nk_pad),
        grid=(nct, bsz // nseq),
        in_specs=[part(0), part(1), part(2),
                  pl.BlockSpec((nseq, seq, ct), lambda c, b: (b, 0, c)),
                  wpart(0), wpart(1), wpart(2), bpart(0), bpart(1), bpart(2),
                  pl.BlockSpec((1, ct), lambda c, b: (0, c)),
                  pl.BlockSpec((None, 2, n2, nk_pad * ct), lambda c, b: (c, 0, 0, 0)),
                  pl.BlockSpec((2, nslab, n2, ct), lambda c, b: (0, 0, 0, 0)),
                  pl.BlockSpec((2 * n2, 2 * n2), lambda c, b: (0, 0))],
        out_specs=pl.BlockSpec((nseq, seq, ct), lambda c, b: (b, 0, c)),
        out_shape=jax.ShapeDtypeStruct((bsz, seq, D_A), F32),
        scratch_shapes=[pltpu.VMEM((par, nslab, n2, ct), F32),
                        pltpu.VMEM((par, 2, max(npair, 1), n2, ct), F32),
                        pltpu.VMEM((par, 2 * n2, nk_pad * ct), BF16),
                        pltpu.VMEM((par, 2, nk_pad, n2, ct), F32)],
        compiler_params=_params(("parallel", "parallel")),
        name="hyena",
    )(ua, ua, ua, ga, conv_w, conv_w, conv_w, cb, cb, cb, sk, kf, _twiddle1(n1, n2, ct),
      _bf16_split(_dft_block(n2))[0])


def _hyena_filter_kernel(z_ref, w1_ref, b1_ref, w2_ref, b2_ref, w3_ref, del_ref,
                         o_ref, nrm_ref, *, seq):
    hp = lax.Precision.HIGHEST
    tm = o_ref.shape[0]
    i = pl.program_id(0)
    h = jnp.sin(jnp.dot(z_ref[...], w1_ref[...], precision=hp,
                        preferred_element_type=F32) + b1_ref[...])
    h = jnp.sin(jnp.dot(h, w2_ref[...], precision=hp, preferred_element_type=F32) + b2_ref[...])
    tcol = (i * tm + lax.broadcasted_iota(jnp.int32, (tm, 1), 0)).astype(F32) * (1.0 / (seq - 1))
    h = jnp.dot(h, w3_ref[...], precision=hp, preferred_element_type=F32) * jnp.exp(
        -tcol * del_ref[...])
    o_ref[...] = h

    @pl.when(i == 0)
    def _():
        nrm_ref[...] = jnp.zeros_like(nrm_ref)

    nrm_ref[...] += jnp.sum(jnp.abs(h), axis=0, keepdims=True)


def _hyena_filter(seq, w1, b1, w2, b2, w3):
    bands = (EMB_DIM - 1) // 2
    ang = (2.0 * np.pi * np.arange(seq)[:, None] / seq) * np.linspace(1e-4, bands - 1, bands)
    z = np.zeros((seq, LANES), np.float32)
    z[:, 0] = np.linspace(0.0, 1.0, seq)
    z[:, 1:1 + bands] = np.cos(ang)
    z[:, 1 + bands:EMB_DIM] = -np.sin(ang)
    deltas = np.abs(np.linspace(MIN_DECAY, MAX_DECAY, D_A)).astype(np.float32)
    deltas2 = np.concatenate([deltas, deltas])[None, :]
    w1p = jnp.pad(w1, ((0, LANES - EMB_DIM), (0, 0)))
    tm = min(seq, 1024)
    fo = w1.shape[1]
    fixed = lambda i: (0, 0)
    h, nrm = pl.pallas_call(
        functools.partial(_hyena_filter_kernel, seq=seq),
        grid=(seq // tm,),
        in_specs=[pl.BlockSpec((tm, LANES), lambda i: (i, 0)),
                  pl.BlockSpec((LANES, fo), fixed),
                  pl.BlockSpec((1, fo), fixed),
                  pl.BlockSpec((fo, fo), fixed),
                  pl.BlockSpec((1, fo), fixed),
                  pl.BlockSpec((fo, 2 * D_A), fixed),
                  pl.BlockSpec((1, 2 * D_A), fixed)],
        out_specs=[pl.BlockSpec((tm, 2 * D_A), lambda i: (i, 0)),
                   pl.BlockSpec((1, 2 * D_A), fixed)],
        out_shape=[jax.ShapeDtypeStruct((seq, 2 * D_A), F32),
                   jax.ShapeDtypeStruct((1, 2 * D_A), F32)],
        compiler_params=_params(("arbitrary",)),
        name="hyena_filter",
    )(jnp.asarray(z), w1p, b1.reshape(1, fo), w2, b2.reshape(1, fo), w3, jnp.asarray(deltas2))
    return h, 1.0 / (nrm[:, :D_A] + nrm[:, D_A:])


def _attn_kernel(cq_ref, cos_ref, sin_ref, wa_ref, wb_ref, k_ref, v_ref, gb_ref, o_ref, *, nseq):
    nt = (((1,), (1,)), ((), ()))
    tq = cq_ref.shape[1]
    cos = jnp.concatenate([cos_ref[...]] * N_HEADS, axis=-1)
    sin = jnp.concatenate([sin_ref[...]] * N_HEADS, axis=-1)
    lane = lax.broadcasted_iota(jnp.int32, (tq, 2 * V_DIM), 1)

    def one_sequence(sq):
        cq = cq_ref[sq]
        q = (_dot(cq, wa_ref[...]) * cos + _dot(cq, wb_ref[...]) * sin) * (ATTN_SCALE * LOG2_E)
        q = q.astype(BF16)

        def scores(h):
            hsl = slice(h * HEAD_PAD, (h + 1) * HEAD_PAD)
            return lax.dot_general(q[:, hsl], k_ref[sq, :, hsl], nt, preferred_element_type=F32)

        look = 1 if k_ref.shape[1] > 2 * tq else N_HEADS - 1
        pending = [scores(h) for h in range(look)]
        outs = []
        for h in range(N_HEADS):
            if h + look < N_HEADS:
                pending.append(scores(h + look))
            sb = pending.pop(0).astype(BF16)
            p = jnp.exp2(sb - jnp.max(sb, axis=-1, keepdims=True))
            of = _dot(p, v_ref[sq, :, h * HEAD_PAD:(h + 1) * HEAD_PAD])
            outs.append(of / of[:, V_DIM:V_DIM + 1])
            if h % 2 == 1:
                vsl = slice((h // 2) * 2 * V_DIM, (h // 2 + 1) * 2 * V_DIM)
                o = jnp.where(lane < V_DIM, outs[h - 1], pltpu.roll(outs[h], V_DIM, 1))
                o_ref[sq, :, vsl] = (o * _silu(gb_ref[sq, :, vsl])).astype(BF16)

    if nseq == 1:
        one_sequence(0)
    else:
        def body(sq, carry):
            one_sequence(sq)
            return carry

        lax.fori_loop(0, nseq, body, 0)


def _attention(cq, cos, sin, wa, wb, k, v, gb, tq, nseq=1):
    bsz, lq, _ = cq.shape
    lk = k.shape[1]
    fixed = lambda b, i: (0, 0)
    qrow = lambda b, i: (b, i, 0)
    kv = lambda b, i: (b, 0, 0)
    pos = lambda b, i: (i, 0)
    return pl.pallas_call(
        functools.partial(_attn_kernel, nseq=nseq),
        grid=(bsz // nseq, lq // tq),
        in_specs=[pl.BlockSpec((nseq, tq, Q_RANK), qrow),
                  pl.BlockSpec((tq, LANES), pos),
                  pl.BlockSpec((tq, LANES), pos),
                  pl.BlockSpec((Q_RANK, N_HEADS * HEAD_PAD), fixed),
                  pl.BlockSpec((Q_RANK, N_HEADS * HEAD_PAD), fixed),
                  pl.BlockSpec((nseq, lk, N_HEADS * HEAD_PAD), kv),
                  pl.BlockSpec((nseq, lk, N_HEADS * HEAD_PAD), kv),
                  pl.BlockSpec((nseq, tq, D_B), qrow)],
        out_specs=pl.BlockSpec((nseq, tq, D_B), qrow),
        out_shape=jax.ShapeDtypeStruct((bsz, lq, D_B), BF16),
        compiler_params=_params(("parallel", "arbitrary")),
        name="mla_attention",
    )(cq, cos, sin, wa, wb, k, v, gb)


def _mid_kernel(x_ref, ya_ref, ob_ref, gt_ref, wo_ref, sc_ref, sh_ref, wi_ref,
                x1_ref, u_ref, g_ref):
    y = _dot(ya_ref[...].astype(BF16), wo_ref[0:D_A, :]) + _dot(ob_ref[...], wo_ref[D_A:, :])
    x1 = x_ref[...] + gt_ref[...] * y
    x1_ref[...] = x1.astype(x1_ref.dtype)
    h = (_rms(x1) * sc_ref[...] + sh_ref[...]).astype(BF16)
    u_ref[...] = _dot(h, wi_ref[:, :D_C]).astype(BF16)
    g_ref[...] = _dot(h, wi_ref[:, D_C:]).astype(BF16)


def _mid(x2d, ya, ob, gt, wo, sc, sh, wi, tiles_per_mod, tm):
    t = x2d.shape[0]
    row = lambda i: (i, 0)
    fixed = lambda i: (0, 0)
    mod = lambda i: (i // tiles_per_mod, 0, 0)
    return pl.pallas_call(
        _mid_kernel,
        grid=(t // tm,),
        in_specs=[pl.BlockSpec((tm, D_MODEL), row),
                  pl.BlockSpec((tm, D_A), row),
                  pl.BlockSpec((tm, D_B), row),
                  pl.BlockSpec((None, 1, D_MODEL), mod),
                  pl.BlockSpec((D_A + D_B, D_MODEL), fixed),
                  pl.BlockSpec((None, 1, D_MODEL), mod),
                  pl.BlockSpec((None, 1, D_MODEL), mod),
                  pl.BlockSpec((D_MODEL, 2 * D_C), fixed)],
        out_specs=[pl.BlockSpec((tm, D_MODEL), row),
                   pl.BlockSpec((tm, D_C), row),
                   pl.BlockSpec((tm, D_C), row)],
        out_shape=[jax.ShapeDtypeStruct((t, D_MODEL), BF16),
                   jax.ShapeDtypeStruct((t, D_C), BF16),
                   jax.ShapeDtypeStruct((t, D_C), BF16)],
        compiler_params=_params(("parallel",)),
        name="outproj_even_inproj_odd",
    )(x2d, ya, ob, gt, wo, sc, sh, wi)


def _fnet_weights_kernel(w_ref, cs_ref, o_ref):
    hp = lax.Precision.HIGHEST
    o_ref[...] = jnp.dot(cs_ref[...], w_ref[...], precision=hp,
                         preferred_element_type=F32).astype(BF16)


def _fnet_weights(fnet_w, seq):
    c, s = _dft_cos_sin(G_C)
    cs = jnp.asarray((np.concatenate([c, s], axis=0) / math.sqrt(seq * G_C)).astype(np.float32))
    return pl.pallas_call(
        _fnet_weights_kernel,
        grid=(N_GROUPS_C,),
        in_specs=[pl.BlockSpec((None, G_C, G_C), lambda g: (g, 0, 0)),
                  pl.BlockSpec((2 * G_C, G_C), lambda g: (0, 0))],
        out_specs=pl.BlockSpec((None, 2 * G_C, G_C), lambda g: (g, 0, 0)),
        out_shape=jax.ShapeDtypeStruct((N_GROUPS_C, 2 * G_C, G_C), BF16),
        compiler_params=_params(("parallel",)),
        name="fnet_weights",
    )(fnet_w, cs)


def _fnet_long_kernel(u_ref, tw1_ref, m_ref, ab_ref, o_ref, sd_scr, a_scr, *, n1, n2):
    h = n1 // 2
    ct = o_ref.shape[-1]
    for i, n in enumerate(range(1, h)):
        a = u_ref[n].astype(F32)
        b = u_ref[n1 - n].astype(F32)
        sd_scr[0, i] = a + b
        sd_scr[1, i] = a - b
    x0 = u_ref[0].astype(F32)
    xh = u_ref[h].astype(F32)
    zero = jnp.zeros((n2, ct), BF16)
    tw = None
    for k1 in range(h + 1):
        ar = x0 + xh if k1 % 2 == 0 else x0 - xh
        ai = None
        for i, n in enumerate(range(1, h)):
            th = 2.0 * math.pi * n * k1 / n1
            ar = _acc(ar, math.cos(th), sd_scr[0, i])
            ai = _acc(ai, -math.sin(th), sd_scr[1, i])
        if k1 > 0:
            tw = _next_twiddle(tw, tw1_ref)
            ar, ai = _cmul(ar, ai, tw[0], tw[1])
        lanes = slice(k1 * ct, (k1 + 1) * ct)
        a_scr[0:n2, lanes] = ar.astype(BF16)
        a_scr[n2:2 * n2, lanes] = zero if ai is None else ai.astype(BF16)

    def group_maps(k1, xr, xi):
        for gi in range(ct // G_C):
            gl = slice(gi * G_C, (gi + 1) * G_C)
            y = _dot(jnp.concatenate([xr[:, gl], xi[:, gl]], axis=1).astype(BF16), ab_ref[gi])
            o_ref[k1, :, gl] = y.astype(o_ref.dtype)

    def dft(k1):
        rows = 4 * n2 if 0 < k1 < h else 2 * n2
        return _dot(m_ref[0:rows, :], a_scr[:, k1 * ct:(k1 + 1) * ct])

    x_next = dft(0)
    for k1 in range(h + 1):
        x = x_next
        if k1 < h:
            x_next = dft(k1 + 1)
        group_maps(k1, x[:n2], x[n2:2 * n2])
        if 0 < k1 < h:
            group_maps(n1 - k1, x[2 * n2:3 * n2], x[3 * n2:])


def _fnet_long(u, ab, n1, n2):
    bsz, seq, _ = u.shape
    h = n1 // 2
    ng = 2
    ct = ng * G_C
    c, s = _dft_cos_sin(n2)
    ce, se = _dft_cos_sin(n2, shift=1)
    m = _bf16_split(np.block([[c, s], [-s, c], [ce, -se], [-se, -ce]]))[0]
    out = pl.pallas_call(
        functools.partial(_fnet_long_kernel, n1=n1, n2=n2),
        grid=(bsz, N_GROUPS_C // ng),
        in_specs=[pl.BlockSpec((None, n1, n2, ct), lambda b, g: (b, 0, 0, g)),
                  pl.BlockSpec((2, h, n2, ct), lambda b, g: (0, 0, 0, 0)),
                  pl.BlockSpec((4 * n2, 2 * n2), lambda b, g: (0, 0)),
                  pl.BlockSpec((ng, 2 * G_C, G_C), lambda b, g: (g, 0, 0))],
        out_specs=pl.BlockSpec((None, n1, n2, ct), lambda b, g: (b, 0, 0, g)),
        out_shape=jax.ShapeDtypeStruct((bsz, n1, n2, D_C), BF16),
        scratch_shapes=[pltpu.VMEM((2, h - 1, n2, ct), F32),
                        pltpu.VMEM((2 * n2, (h + 1) * ct), BF16)],
        compiler_params=_params(("parallel", "parallel")),
        name="fnet_long",
    )(u.reshape(bsz, n1, n2, D_C), _twiddle1(n1, n2, ct), m, ab)
    return out.transpose(0, 2, 1, 3).reshape(bsz, seq, D_C)


def _fnet_short_kernel(u_ref, f_ref, ab_ref, o_ref, *, seq, scale):
    def body(sq, carry):
        x = _dot(f_ref[...], u_ref[sq])
        xr, xi = x[:seq], x[seq:]
        for g in range(N_GROUPS_C):
            sl = slice(g * G_C, (g + 1) * G_C)
            xin = jnp.concatenate([xr[:, sl], xi[:, sl]], axis=1).astype(BF16)
            o_ref[sq, :, sl] = (_dot(xin, ab_ref[g]) * scale).astype(o_ref.dtype)
        return carry

    lax.fori_loop(0, u_ref.shape[0], body, 0)


def _fnet_short(u, ab, scale, nseq=4):
    bsz, seq, _ = u.shape
    c, s = _dft_cos_sin(seq)
    f = _bf16_split(np.concatenate([c, -s], axis=0))[0]
    return pl.pallas_call(
        functools.partial(_fnet_short_kernel, seq=seq, scale=scale),
        grid=(bsz // nseq,),
        in_specs=[pl.BlockSpec((nseq, seq, D_C), lambda b: (b, 0, 0)),
                  pl.BlockSpec((2 * seq, seq), lambda b: (0, 0)),
                  pl.BlockSpec((N_GROUPS_C, 2 * G_C, G_C), lambda b: (0, 0, 0))],
        out_specs=pl.BlockSpec((nseq, seq, D_C), lambda b: (b, 0, 0)),
        out_shape=jax.ShapeDtypeStruct((bsz, seq, D_C), BF16),
        compiler_params=_params(("parallel",)),
        name="fnet_short",
    )(u, f, ab)


def _final_kernel(x1_ref, y_ref, g_ref, gt_ref, wo_ref, fg_ref, o_ref):
    z = (y_ref[...].astype(F32) * _silu(g_ref[...].astype(F32))).astype(BF16)
    x2 = x1_ref[...].astype(F32) + gt_ref[...] * _dot(z, wo_ref[...])
    o_ref[...] = _rms(x2) * fg_ref[...]


def _final(x1, y, g, gt, wo, fg, tiles_per_mod, tm):
    t = x1.shape[0]
    row = lambda i: (i, 0)
    fixed = lambda i: (0, 0)
    mod = lambda i: (i // tiles_per_mod, 0, 0)
    return pl.pallas_call(
        _final_kernel,
        grid=(t // tm,),
        in_specs=[pl.BlockSpec((tm, D_MODEL), row),
                  pl.BlockSpec((tm, D_C), row),
                  pl.BlockSpec((tm, D_C), row),
                  pl.BlockSpec((None, 1, D_MODEL), mod),
                  pl.BlockSpec((D_C, D_MODEL), fixed),
                  pl.BlockSpec((1, D_MODEL), fixed)],
        out_specs=pl.BlockSpec((tm, D_MODEL), row),
        out_shape=jax.ShapeDtypeStruct((t, D_MODEL), F32),
        compiler_params=_params(("parallel",)),
        name="outproj_odd_final",
    )(x1, y, g, gt, wo, fg)


def _rope_swap(w):
    nf = ROPE_DIM // 4
    w4 = w.reshape(w.shape[:-1] + (2, 2, nf))
    return jnp.stack([-w4[..., 1, :], w4[..., 0, :]], axis=-2).reshape(w.shape)


def _pack_w_in_even(w):
    ua_ga_cq_ckv = w[:, :4 * D_A + Q_RANK + KV_RANK]
    o = 4 * D_A + Q_RANK + KV_RANK
    krope = w[:, o:o + ROPE_DIM]
    gb = w[:, o + ROPE_DIM:]
    zpad = jnp.zeros((w.shape[0], LANES - ROPE_DIM), w.dtype)
    return jnp.concatenate([ua_ga_cq_ckv, gb, krope, zpad, _rope_swap(krope), zpad],
                           axis=1).astype(BF16)


def _pack_w_uq(w):
    w3 = w.reshape(Q_RANK, N_HEADS, QK_DIM)
    zq = jnp.zeros((Q_RANK, N_HEADS, HEAD_PAD - QK_DIM), w.dtype)
    wa = jnp.concatenate([w3, zq], axis=-1)
    zn = jnp.zeros((Q_RANK, N_HEADS, NOPE_DIM), w.dtype)
    wb = jnp.concatenate([zn, _rope_swap(w3[..., NOPE_DIM:]), zq], axis=-1)
    shape = (Q_RANK, N_HEADS * HEAD_PAD)
    return wa.reshape(shape).astype(BF16), wb.reshape(shape).astype(BF16)


def _pack_w_ukv(w):
    w3 = w.reshape(KV_RANK, N_HEADS, NOPE_DIM + V_DIM)
    zk = jnp.zeros((KV_RANK, N_HEADS, HEAD_PAD - NOPE_DIM), w.dtype)
    wk_c = jnp.concatenate([w3[..., :NOPE_DIM], zk], axis=-1).reshape(KV_RANK, -1)
    eye = jnp.eye(LANES, ROPE_DIM, dtype=w.dtype)
    rope_rows = jnp.concatenate([jnp.zeros((LANES, NOPE_DIM), w.dtype), eye,
                                 jnp.zeros((LANES, HEAD_PAD - QK_DIM), w.dtype)], axis=-1)
    wk_r = jnp.tile(rope_rows, (1, N_HEADS))
    wk = jnp.concatenate([wk_c, wk_r], axis=0)
    zv = jnp.zeros((KV_RANK, N_HEADS, HEAD_PAD - V_DIM), w.dtype)
    wv = jnp.concatenate([w3[..., NOPE_DIM:], zv], axis=-1).reshape(KV_RANK, -1)
    ones = np.zeros((1, N_HEADS, HEAD_PAD), np.float32)
    ones[:, :, V_DIM] = 1.0
    return wk.astype(BF16), wv.astype(BF16), jnp.asarray(ones.reshape(1, -1))


def _rope_tables(seq, rotate):
    pos = np.arange(seq)
    half = ROPE_DIM // 2
    inv = ROPE_BASE ** (-np.arange(0, half, 2, dtype=np.float64) / half)
    r = (pos // GRID_W)[:, None] * inv
    c = (pos % GRID_W)[:, None] * inv
    ang = np.concatenate([r, r, c, c], axis=-1) * (1.0 if rotate else 0.0)
    cos_q = np.zeros((seq, LANES))
    sin_q = np.zeros((seq, LANES))
    cos_k = np.zeros((seq, LANES))
    sin_k = np.zeros((seq, LANES))
    cos_q[:, :NOPE_DIM] = 1.0
    cos_q[:, NOPE_DIM:QK_DIM] = np.cos(ang)
    sin_q[:, NOPE_DIM:QK_DIM] = np.sin(ang)
    cos_k[:, :ROPE_DIM] = np.cos(ang)
    sin_k[:, :ROPE_DIM] = np.sin(ang)
    return tuple(jnp.asarray(t.astype(np.float32)) for t in (cos_q, sin_q, cos_k, sin_k))


def kernel(x_prompt, x_sample, c, cache_ckv, cache_krope, c_ctx, norm_g, w_mod, b_mod,
           w_in_e, conv_w, conv_b, filt_w1, filt_b1, filt_w2, filt_b2, filt_w3, hyena_skip,
           q_norm_g, kv_norm_g, w_uq, w_ukv, w_out_e, w_in_o, fnet_w, w_out_o, final_g):
    nb_p, seq_p, _ = x_prompt.shape
    nb_s, seq_s, _ = x_sample.shape
    tp = nb_p * seq_p
    ts = nb_s * seq_s
    tm = 512

    cond = jnp.concatenate([c_ctx[None, :], c, jnp.zeros((8 - 1 - nb_s, D_MODEL), F32)], axis=0)
    mod = _ada_params(cond, w_mod, b_mod)
    shift = mod[:, :, None, :D_MODEL]
    scale = (1.0 + mod[:, :, None, D_MODEL:2 * D_MODEL]) * norm_g[:, None, None, :]
    gate = mod[:, :, None, 2 * D_MODEL:]

    xp = x_prompt.reshape(tp, D_MODEL)
    xs = x_sample.reshape(ts, D_MODEL)

    w_in = _pack_w_in_even(w_in_e[0])
    qg = q_norm_g[0].reshape(1, Q_RANK)
    kvg = kv_norm_g[0].reshape(1, KV_RANK)
    wq_a, wq_b = _pack_w_uq(w_uq[0])
    wk, wv, vone = _pack_w_ukv(w_ukv[0])
    cos_qp, sin_qp, cos_kp, sin_kp = _rope_tables(tm, rotate=False)
    cos_qs, sin_qs, cos_ks, sin_ks = _rope_tables(seq_s, rotate=True)
    pr_p = _inproj_even(xp, scale[0, :1], shift[0, :1], w_in, qg, kvg, cos_kp, sin_kp, wk, wv,
                        vone, tm, True)
    pr_s = _inproj_even(xs, scale[0, 1:1 + nb_s], shift[0, 1:1 + nb_s], w_in, qg, kvg,
                        cos_ks, sin_ks, wk, wv, vone, tm, False,
                        cache=(cache_ckv, cache_krope, 0))
    ua_p, ga_p, cq_p, gb_p, k_p, v_p, ckv_p, kra_p = pr_p
    ua_s, ga_s, cq_s, gb_s, k_s, v_s = pr_s

    filt = (filt_w1[0], filt_b1[0], filt_w2[0], filt_b2[0], filt_w3[0])
    ya = []
    for ua, ga, nb, seq in ((ua_p, ga_p, nb_p, seq_p), (ua_s, ga_s, nb_s, seq_s)):
        h, inv_norm = _hyena_filter(seq, *filt)
        kf = _spectrum(h, inv_norm, seq)
        ya.append(_hyena(ua.reshape(nb, seq, 3 * D_A), ga.reshape(nb, seq, D_A),
                         conv_w[0], conv_b[0], hyena_skip[0], kf))
    ya_p = ya[0].reshape(tp, D_A)
    ya_s = ya[1].reshape(ts, D_A)

    ob_p = _attention(cq_p.reshape(nb_p, seq_p, Q_RANK), cos_qp, sin_qp, wq_a, wq_b,
                      k_p.reshape(nb_p, seq_p, -1), v_p.reshape(nb_p, seq_p, -1),
                      gb_p.reshape(nb_p, seq_p, D_B), seq_p, nseq=4)
    ob_s = _attention(cq_s.reshape(nb_s, seq_s, Q_RANK), cos_qs, sin_qs, wq_a, wq_b, k_s, v_s,
                      gb_s.reshape(nb_s, seq_s, D_B), 512)

    wo_e = w_out_e[0].astype(BF16)
    wi_o = w_in_o[0].astype(BF16)
    tw = 2 * tm
    x1_p, u_p, g_p = _mid(xp, ya_p, ob_p.reshape(tp, D_B), gate[0, :1], wo_e,
                          scale[1, :1], shift[1, :1], wi_o, tp // tw, tw)
    x1_s, u_s, g_s = _mid(xs, ya_s, ob_s.reshape(ts, D_B), gate[0, 1:1 + nb_s], wo_e,
                          scale[1, 1:1 + nb_s], shift[1, 1:1 + nb_s], wi_o, seq_s // tw, tw)

    ab = _fnet_weights(fnet_w[0], seq_s)
    y_p = _fnet_short(u_p.reshape(nb_p, seq_p, D_C), ab, math.sqrt(seq_s / seq_p))
    y_s = _fnet_long(u_s.reshape(nb_s, seq_s, D_C), ab, 16, seq_s // 16)
    wo_o = w_out_o[0].astype(BF16)
    fg = final_g.reshape(1, D_MODEL)
    out_p = _final(x1_p, y_p.reshape(tp, D_C), g_p, gate[1, :1], wo_o, fg, tp // tw, tw)
    out_s = _final(x1_s, y_s.reshape(ts, D_C), g_s, gate[1, 1:1 + nb_s], wo_o, fg,
                   seq_s // tw, tw)

    state_ckv = ckv_p.reshape(nb_p, 1, seq_p, KV_RANK)
    state_krope = kra_p[:, :ROPE_DIM].reshape(nb_p, 1, seq_p, ROPE_DIM)
    return (out_p.reshape(nb_p, seq_p, D_MODEL), out_s.reshape(nb_s, seq_s, D_MODEL),
            state_ckv, state_krope)
```

```python
import functools
import math

import numpy as np
import jax
import jax.numpy as jnp
from jax import lax
from jax.experimental import pallas as pl
from jax.experimental.pallas import tpu as pltpu

F32 = jnp.float32
BF16 = jnp.bfloat16

D_MODEL = 1024
GRID_W = 64
EPS = 1e-6
D_A = 512
EMB_DIM = 33
DECAY_TARGET = 1e-2
MIN_DECAY = math.log(DECAY_TARGET) / 1.5
MAX_DECAY = math.log(DECAY_TARGET) / 0.3
N_HEADS = 8
Q_RANK = 256
KV_RANK = 256
NOPE_DIM = 64
ROPE_DIM = 32
V_DIM = 64
ROPE_BASE = 10000.0
D_B = N_HEADS * V_DIM
QK_DIM = NOPE_DIM + ROPE_DIM
ATTN_SCALE = 1.0 / math.sqrt(QK_DIM)
LOG2_E = math.log2(math.e)
D_C = 1024
N_GROUPS_C = 8
G_C = D_C // N_GROUPS_C

LANES = 128
HEAD_PAD = 128
VMEM_LIMIT = 56 * 1024 * 1024

_C_UA, _C_GA, _C_CQ, _C_CKV, _C_GB, _C_KRA, _C_KRB, _C_END = (
    0, 1536, 2048, 2304, 2560, 3072, 3200, 3328)


def _params(sem):
    return pltpu.CompilerParams(dimension_semantics=sem, vmem_limit_bytes=VMEM_LIMIT)


def _rms(x):
    return x * lax.rsqrt(jnp.mean(x * x, axis=-1, keepdims=True) + EPS)


def _silu(g):
    return g * jax.nn.sigmoid(g)


def _dot(a, b):
    return jnp.dot(a, b, preferred_element_type=F32)


def _ada_kernel(cond_ref, w_ref, b_ref, o_ref):
    c = _silu(cond_ref[...])
    o_ref[...] = jnp.dot(c, w_ref[...], preferred_element_type=F32,
                         precision=lax.Precision.HIGHEST) + b_ref[...]


def _ada_params(cond, w_mod, b_mod):
    depth = w_mod.shape[0]
    n = cond.shape[0]
    tn = 1024
    return pl.pallas_call(
        _ada_kernel,
        grid=(depth, 3 * D_MODEL // tn),
        in_specs=[pl.BlockSpec((n, D_MODEL), lambda i, j: (0, 0)),
                  pl.BlockSpec((None, D_MODEL, tn), lambda i, j: (i, 0, j)),
                  pl.BlockSpec((None, 1, tn), lambda i, j: (i, 0, j))],
        out_specs=pl.BlockSpec((None, n, tn), lambda i, j: (i, 0, j)),
        out_shape=jax.ShapeDtypeStruct((depth, n, 3 * D_MODEL), F32),
        compiler_params=_params(("arbitrary", "arbitrary")),
        name="ada_params",
    )(cond, w_mod, b_mod.reshape(depth, 1, 3 * D_MODEL))


def _inproj_even_kernel(x_ref, sc_ref, sh_ref, w_ref, qg_ref, kvg_ref, cos_ref, sin_ref,
                        wk_ref, wv_ref, vone_ref, *refs, nctx, emit_state):
    if nctx:
        cckv_ref, ckr_ref = refs[:2]
        refs = refs[2:]
    ua_ref, ga_ref, cq_ref, gb_ref, k_ref, v_ref = refs[:6]
    state_refs = refs[6:]

    def tokens():
        h = (_rms(x_ref[...]) * sc_ref[...] + sh_ref[...]).astype(BF16)

        def proj(a, b):
            return _dot(h, w_ref[:, a:b])

        ua_ref[...] = proj(_C_UA, _C_GA)
        ga_ref[...] = proj(_C_GA, _C_CQ)
        cq_ref[...] = (_rms(proj(_C_CQ, _C_CKV)) * qg_ref[...]).astype(BF16)
        gb_ref[...] = proj(_C_GB, _C_KRA)
        ckv = _rms(proj(_C_CKV, _C_GB)) * kvg_ref[...]
        kr2 = proj(_C_KRA, _C_END)
        kra = kr2[:, :LANES]
        krb = kr2[:, LANES:]
        c = ckv.astype(BF16)
        kr = pltpu.roll(kra * cos_ref[...] + krb * sin_ref[...], NOPE_DIM, 1)
        k = _dot(c, wk_ref[0:KV_RANK, :]) + jnp.concatenate([kr] * N_HEADS, axis=-1)
        k_ref[...] = k.astype(BF16)
        v_ref[...] = (_dot(c, wv_ref[...]) + vone_ref[...]).astype(BF16)
        if emit_state:
            state_refs[0][...] = ckv
            state_refs[1][...] = kra

    if not nctx:
        tokens()
        return

    t = pl.program_id(1)

    @pl.when(t < nctx)
    def _():
        c = cckv_ref[...].astype(BF16)
        kr = ckr_ref[...].astype(BF16)
        k_ref[...] = (_dot(c, wk_ref[0:KV_RANK, :])
                      + _dot(kr, wk_ref[KV_RANK:KV_RANK + ROPE_DIM, :])).astype(BF16)
        v_ref[...] = (_dot(c, wv_ref[...]) + vone_ref[...]).astype(BF16)

    pl.when(t >= nctx)(tokens)


def _inproj_even(x2d, sc, sh, w, qg, kvg, cos_k, sin_k, wk, wv, vone, tm, emit_state,
                 cache=None):
    t = x2d.shape[0]
    nb = sc.shape[0]
    tiles = t // nb // tm
    tiles_per_seq = cos_k.shape[0] // tm
    nctx = 0 if cache is None else cache[0].shape[2] // tm
    tok = lambda b, s: b * tiles + jnp.maximum(s - nctx, 0)
    row = lambda b, s: (tok(b, s), 0)
    fixed = lambda b, s: (0, 0)
    mod = lambda b, s: (b, 0, 0)
    pos = lambda b, s: (jnp.maximum(s - nctx, 0) % tiles_per_seq, 0)
    kvo = lambda b, s: (b, s, 0)
    widths = (3 * D_A, D_A, Q_RANK, D_B)
    dtypes = (F32, F32, BF16, F32)
    in_specs = [pl.BlockSpec((tm, D_MODEL), row),
                pl.BlockSpec((None, 1, D_MODEL), mod),
                pl.BlockSpec((None, 1, D_MODEL), mod),
                pl.BlockSpec((D_MODEL, _C_END), fixed),
                pl.BlockSpec((1, Q_RANK), fixed),
                pl.BlockSpec((1, KV_RANK), fixed),
                pl.BlockSpec((tm, LANES), pos),
                pl.BlockSpec((tm, LANES), pos),
                pl.BlockSpec((KV_RANK + LANES, N_HEADS * HEAD_PAD), fixed),
                pl.BlockSpec((KV_RANK, N_HEADS * HEAD_PAD), fixed),
                pl.BlockSpec((1, N_HEADS * HEAD_PAD), fixed)]
    args = [x2d, sc, sh, w, qg, kvg, cos_k, sin_k, wk, wv, vone]
    if nctx:
        layer = cache[2]
        ctx = lambda b, s: (b, layer, jnp.minimum(s, nctx - 1), 0)
        in_specs += [pl.BlockSpec((None, None, tm, KV_RANK), ctx),
                     pl.BlockSpec((None, None, tm, ROPE_DIM), ctx)]
        args += [cache[0], cache[1]]
    out_specs = [pl.BlockSpec((tm, wd), row) for wd in widths]
    out_shape = [jax.ShapeDtypeStruct((t, wd), dt) for wd, dt in zip(widths, dtypes)]
    for wd in (N_HEADS * HEAD_PAD, N_HEADS * HEAD_PAD):
        out_specs.append(pl.BlockSpec((None, tm, wd), kvo))
        out_shape.append(jax.ShapeDtypeStruct((nb, (nctx + tiles) * tm, wd), BF16))
    if emit_state:
        for wd in (KV_RANK, LANES):
            out_specs.append(pl.BlockSpec((tm, wd), row))
            out_shape.append(jax.ShapeDtypeStruct((t, wd), F32))
    return pl.pallas_call(
        functools.partial(_inproj_even_kernel, nctx=nctx, emit_state=emit_state),
        grid=(nb, nctx + tiles),
        in_specs=in_specs,
        out_specs=out_specs,
        out_shape=out_shape,
        compiler_params=_params(("parallel", "arbitrary")),
        name="inproj_even",
    )(*args)


def _acc(acc, coef, x):
    if x is None or abs(coef) < 1e-12:
        return acc
    if abs(abs(coef) - 1.0) < 1e-12:
        if acc is None:
            return x if coef > 0 else -x
        return acc + x if coef > 0 else acc - x
    return coef * x if acc is None else acc + coef * x


def _add(a, b):
    if a is None:
        return b
    return a if b is None else a + b


def _sub(a, b):
    if b is None:
        return a
    return -b if a is None else a - b


def _cmul(ar, ai, br, bi):
    def mul(x, y):
        return None if x is None or y is None else x * y
    return _sub(mul(ar, br), mul(ai, bi)), _add(mul(ar, bi), mul(ai, br))


def _radix_parts(h):
    singles = [0] + ([h // 2] if h >= 2 and h % 2 == 0 else [])
    pairs = [(n, h - n) for n in range(1, (h + 1) // 2)]
    return singles, pairs


def _radix_fwd(k1, h, single, pair_sum, pair_diff):
    singles, pairs = _radix_parts(h)
    ar = ai = None
    for n in singles:
        th = 2.0 * math.pi * n * k1 / (2 * h)
        x = single(n)
        ar = _acc(ar, math.cos(th), x)
        ai = _acc(ai, -math.sin(th), x)
    for i, (n, _) in enumerate(pairs):
        th = 2.0 * math.pi * n * k1 / (2 * h)
        re_src, im_src = (pair_sum, pair_diff) if k1 % 2 == 0 else (pair_diff, pair_sum)
        ar = _acc(ar, math.cos(th), re_src(i))
        ai = _acc(ai, -math.sin(th), im_src(i))
    return ar, ai


def _radix_inv(h, n_total, bpr, bpi, emit):
    singles, pairs = _radix_parts(h)

    def coefs(n, k1):
        th = 2.0 * math.pi * n * k1 / (2 * h)
        w = (1.0 if k1 in (0, h) else 2.0) / n_total
        return w * math.cos(th), w * math.sin(th)

    for n in singles:
        y = None
        for k1 in range(h + 1):
            c, s = coefs(n, k1)
            y = _acc(y, c, bpr(k1) if abs(c) > 1e-12 else None)
            y = _acc(y, -s, bpi(k1) if abs(s) > 1e-12 else None)
        emit(n, y)
    for n, m in pairs:
        acc = {(0, "c"): None, (0, "s"): None, (1, "c"): None, (1, "s"): None}
        for k1 in range(h + 1):
            c, s = coefs(n, k1)
            par = k1 % 2
            acc[(par, "c")] = _acc(acc[(par, "c")], c, bpr(k1) if abs(c) > 1e-12 else None)
            acc[(par, "s")] = _acc(acc[(par, "s")], s, bpi(k1) if abs(s) > 1e-12 else None)
        even_m, even_p = _sub(acc[(0, "c")], acc[(0, "s")]), _add(acc[(0, "c")], acc[(0, "s")])
        odd_m, odd_p = _sub(acc[(1, "c")], acc[(1, "s")]), _add(acc[(1, "c")], acc[(1, "s")])
        emit(n, _add(even_m, odd_m))
        emit(m, _sub(even_p, odd_p))


def _next_twiddle(tw, tw_ref):
    row = 0 if tw is None else tw[2] + 1
    return tw_ref[0, row], tw_ref[1, row], row


def _spectrum_kernel(hf_ref, hb_ref, inv_ref, tw1_ref, fh_ref, kf_ref, sd_scr,
                     *, h, n2, nk_pad):
    ct = hf_ref.shape[-1]
    row = lax.broadcasted_iota(jnp.int32, (n2, ct), 0)
    _, pairs = _radix_parts(h)

    def hb(n):
        return jnp.where(row == 0, 0.0, hb_ref[0]) if n == 0 else hb_ref[n]

    for i, (n, m) in enumerate(pairs):
        sd_scr[0, 0, i] = hf_ref[n] + hf_ref[m]
        sd_scr[0, 1, i] = hf_ref[n] - hf_ref[m]
        sd_scr[1, 0, i] = hb(n) + hb(m)
        sd_scr[1, 1, i] = hb(n) - hb(m)

    inv = inv_ref[...]
    tw = None
    for k1 in range(h + 1):
        far, fai = _radix_fwd(k1, h, lambda n: hf_ref[n],
                              lambda i: sd_scr[0, 0, i], lambda i: sd_scr[0, 1, i])
        bar, bai = _radix_fwd(k1, h, hb, lambda i: sd_scr[1, 0, i], lambda i: sd_scr[1, 1, i])
        if k1 > 0:
            tw = _next_twiddle(tw, tw1_ref)
            far, fai = _cmul(far, fai, tw[0], tw[1])
            bar, bai = _cmul(bar, bai, tw[0], tw[1])
        zero = jnp.zeros((n2, ct), F32)
        af = jnp.concatenate([zero if far is None else far, zero if fai is None else fai], axis=0)
        ab = jnp.concatenate([zero if bar is None else bar, zero if bai is None else bai], axis=0)
        a = jnp.concatenate([af, ab], axis=1)
        x = _dot(fh_ref[...], a.astype(BF16))
        lanes = slice(k1 * ct, (k1 + 1) * ct)
        kf_ref[0, :, lanes] = (x[:n2, :ct] + x[:n2, ct:]) * inv
        kf_ref[1, :, lanes] = (x[n2:, :ct] - x[n2:, ct:]) * inv
    if nk_pad > h + 1:
        pad = slice((h + 1) * ct, nk_pad * ct)
        kf_ref[:, :, pad] = jnp.zeros((2, n2, (nk_pad - h - 1) * ct), F32)


def _hyena_kernel(x0_ref, x1_ref, v_ref, ga_ref, *refs, **plan):
    o_ref = refs[10]
    scratch = refs[11:]
    par = scratch[0].shape[0]

    def body(trip, carry):
        for slot in range(par):
            sq = trip * par + slot
            _hyena_sequence(x0_ref.at[sq], x1_ref.at[sq], v_ref.at[sq], ga_ref.at[sq],
                            *refs[:10], o_ref.at[sq], *[s.at[slot] for s in scratch], **plan)
        return carry

    trips = o_ref.shape[0] // par
    if trips == 1:
        body(0, 0)
    else:
        lax.fori_loop(0, trips, body, 0)


def _hyena_sequence(x0_ref, x1_ref, v_ref, ga_ref,
                    w0_ref, w1_ref, wv_ref, b0_ref, b1_ref, bv_ref, skip_ref,
                    kf_ref, tw1_ref, f_ref,
                    o_ref, v_scr, sd_scr, a_scr, b_scr, *, seq, nslab, n2, nk_pad):
    ct = o_ref.shape[-1]
    h = nslab
    chunk = 2 * LANES
    kpc = chunk // ct
    row = lax.broadcasted_iota(jnp.int32, (n2, ct), 0)

    def short_conv(ref, w_ref, b_ref, s, start):
        u = ref[pl.ds(start, n2), :]
        if isinstance(s, int):
            zero_row = jnp.zeros((1, ct), F32)
            prev = zero_row if s == 0 else ref[pl.ds(start - 1, 1), :]
            nxt = zero_row if s == nslab - 1 else ref[pl.ds(start + n2, 1), :]
        else:
            prev = ref[pl.ds(jnp.maximum(start - 1, 0), 1), :]
            prev = jnp.where(s == 0, 0.0, prev)
            nxt = ref[pl.ds(jnp.minimum(start + n2, seq - 1), 1), :]
            nxt = jnp.where(s == nslab - 1, 0.0, nxt)
        up = jnp.where(row == 0, prev, pltpu.roll(u, 1, 0))
        un = jnp.where(row == n2 - 1, nxt, pltpu.roll(u, n2 - 1, 0))
        return b_ref[...] + up * w_ref[0:1, :] + u * w_ref[1:2, :] + un * w_ref[2:3, :]

    def prep(s, carry):
        start = pl.multiple_of(s * n2, n2)
        x0 = short_conv(x0_ref, w0_ref, b0_ref, s, start)
        x1 = short_conv(x1_ref, w1_ref, b1_ref, s, start)
        v_scr[s] = short_conv(v_ref, wv_ref, bv_ref, s, start) * x1
        o_ref[pl.ds(start, n2), :] = x0 * _silu(ga_ref[pl.ds(start, n2), :])
        return carry

    lax.fori_loop(0, nslab, prep, 0)

    _, pairs = _radix_parts(h)
    for i, (n, m) in enumerate(pairs):
        sd_scr[0, i] = v_scr[n] + v_scr[m]
        sd_scr[1, i] = v_scr[n] - v_scr[m]
    zero = jnp.zeros((n2, ct), BF16)
    tw = None
    for k1 in range(h + 1):
        ar, ai = _radix_fwd(k1, h, lambda n: v_scr[n], lambda i: sd_scr[0, i],
                            lambda i: sd_scr[1, i])
        if k1 > 0:
            tw = _next_twiddle(tw, tw1_ref)
            ar, ai = _cmul(ar, ai, tw[0], tw[1])
        lanes = slice(k1 * ct, (k1 + 1) * ct)
        a_scr[0:n2, lanes] = zero if ar is None else ar.astype(BF16)
        a_scr[n2:2 * n2, lanes] = zero if ai is None else ai.astype(BF16)
    for k1 in range(h + 1, nk_pad):
        lanes = slice(k1 * ct, (k1 + 1) * ct)
        a_scr[0:n2, lanes] = zero
        a_scr[n2:2 * n2, lanes] = zero

    def dft(j):
        return _dot(f_ref[...], a_scr[:, j * chunk:(j + 1) * chunk])

    tw = None
    nchunk = nk_pad // kpc
    x_next = dft(0)
    for j in range(nchunk):
        cols = slice(j * chunk, (j + 1) * chunk)
        x = x_next
        if j + 1 < nchunk:
            x_next = dft(j + 1)
        xr, xi = x[:n2], x[n2:]
        kr = kf_ref[0, :, cols]
        ki = kf_ref[1, :, cols]
        z = jnp.concatenate([xr * kr - xi * ki, -(xr * ki + xi * kr)], axis=0)
        bt = _dot(f_ref[...], z.astype(BF16))
        for kk in range(kpc):
            k1 = j * kpc + kk
            if k1 > h:
                continue
            lanes = slice(kk * ct, (kk + 1) * ct)
            br = bt[:n2, lanes]
            bi = -bt[n2:, lanes]
            if k1 > 0:
                tw = _next_twiddle(tw, tw1_ref)
                br, bi = _cmul(br, bi, tw[0], -tw[1])
            b_scr[0, k1] = br
            if k1 not in (0, h):
                b_scr[1, k1] = bi

    def emit(n, y):
        rows = pl.ds(n * n2, n2)
        o_ref[rows, :] = (y + v_scr[n] * skip_ref[...]) * o_ref[rows, :]

    _radix_inv(h, 2 * seq, lambda k1: b_scr[0, k1], lambda k1: b_scr[1, k1], emit)


def _dft_cos_sin(n2, shift=0):
    idx = np.arange(n2)
    ang = 2.0 * np.pi * (((idx[:, None] + shift) * idx[None, :]) % n2) / n2
    return np.cos(ang), np.sin(ang)


def _bf16_split(x):
    x32 = jnp.asarray(x.astype(np.float32))
    hi = x32.astype(BF16)
    lo = (x32 - hi.astype(F32)).astype(BF16)
    return hi, lo


def _dft_block(n2):
    c, s = _dft_cos_sin(n2)
    return np.block([[c, s], [-s, c]])


def _hyena_plan(seq):
    n2 = min(seq, 512)
    n1 = 2 * seq // n2
    ct = LANES if seq > 512 else 2 * LANES
    kpc = 2 * LANES // ct
    nk_pad = -(-(n1 // 2 + 1) // kpc) * kpc
    return n1, n2, ct, nk_pad


def _twiddle1(n1, n2, ct):
    ang = 2.0 * np.pi * np.arange(1, n1 // 2 + 1)[:, None] * np.arange(n2)[None, :] / (n1 * n2)
    tw = np.stack([np.cos(ang), -np.sin(ang)]).astype(np.float32)
    return jnp.broadcast_to(jnp.asarray(tw)[..., None], tw.shape + (ct,))


def _spectrum(h, inv_norm, seq):
    n1, n2, ct, nk_pad = _hyena_plan(seq)
    nslab = n1 // 2
    nct = D_A // ct
    npair = len(_radix_parts(nslab)[1])
    ffwd = _bf16_split(_dft_block(n2))[0]
    h3 = h.reshape(nslab, n2, 2 * D_A)
    fixed2 = lambda c: (0, 0)
    return pl.pallas_call(
        functools.partial(_spectrum_kernel, h=nslab, n2=n2, nk_pad=nk_pad),
        grid=(nct,),
        in_specs=[pl.BlockSpec((nslab, n2, ct), lambda c: (0, 0, c)),
                  pl.BlockSpec((nslab, n2, ct), lambda c: (0, 0, nct + c)),
                  pl.BlockSpec((1, ct), lambda c: (0, c)),
                  pl.BlockSpec((2, nslab, n2, ct), lambda c: (0, 0, 0, 0)),
                  pl.BlockSpec((2 * n2, 2 * n2), fixed2)],
        out_specs=pl.BlockSpec((None, 2, n2, nk_pad * ct), lambda c: (c, 0, 0, 0)),
        out_shape=jax.ShapeDtypeStruct((nct, 2, n2, nk_pad * ct), F32),
        scratch_shapes=[pltpu.VMEM((2, 2, max(npair, 1), n2, ct), F32)],
        compiler_params=_params(("parallel",)),
        name="filter_spectrum",
    )(h3, h3, inv_norm, _twiddle1(n1, n2, ct), ffwd)


def _hyena(ua, ga, conv_w, conv_b, skip, kf):
    bsz, seq, _ = ua.shape
    n1, n2, ct, nk_pad = _hyena_plan(seq)
    nslab = n1 // 2
    nct = D_A // ct
    npair = len(_radix_parts(nslab)[1])
    cb = conv_b.reshape(1, 3 * D_A)
    sk = skip.reshape(1, D_A)
    nseq = max(1, 1024 // seq)
    par = min(nseq, 2)
    part = lambda p: pl.BlockSpec((nseq, seq, ct), lambda c, b: (b, 0, p * nct + c))
    wpart = lambda p: pl.BlockSpec((3, ct), lambda c, b: (0, p * nct + c))
    bpart = lambda p: pl.BlockSpec((1, ct), lambda c, b: (0, p * nct + c))
    return pl.pallas_call(
        functools.partial(_hyena_kernel, seq=seq, nslab=nslab, n2=n2, nk_pad=nk_pad),
        grid=(nct, bsz // nseq),
        in_specs=[part(0), part(1), part(2),
                  pl.BlockSpec((nseq, seq, ct), lambda c, b: (b, 0, c)),
                  wpart(0), wpart(1), wpart(2), bpart(0), bpart(1), bpart(2),
                  pl.BlockSpec((1, ct), lambda c, b: (0, c)),
                  pl.BlockSpec((None, 2, n2, nk_pad * ct), lambda c, b: (c, 0, 0, 0)),
                  pl.BlockSpec((2, nslab, n2, ct), lambda c, b: (0, 0, 0, 0)),
                  pl.BlockSpec((2 * n2, 2 * n2), lambda c, b: (0, 0))],
        out_specs=pl.BlockSpec((nseq, seq, ct), lambda c, b: (b, 0, c)),
        out_shape=jax.ShapeDtypeStruct((bsz, seq, D_A), F32),
        scratch_shapes=[pltpu.VMEM((par, nslab, n2, ct), F32),
                        pltpu.VMEM((par, 2, max(npair, 1), n2, ct), F32),
                        pltpu.VMEM((par, 2 * n2, nk_pad * ct), BF16),
                        pltpu.VMEM((par, 2, nk_pad, n2, ct), F32)],
        compiler_params=_params(("parallel", "parallel")),
        name="hyena",
    )(ua, ua, ua, ga, conv_w, conv_w, conv_w, cb, cb, cb, sk, kf, _twiddle1(n1, n2, ct),
      _bf16_split(_dft_block(n2))[0])


def _hyena_filter_kernel(z_ref, w1_ref, b1_ref, w2_ref, b2_ref, w3_ref, del_ref,
                         o_ref, nrm_ref, *, seq):
    hp = lax.Precision.HIGHEST
    tm = o_ref.shape[0]
    half = tm // 2
    wd = o_ref.shape[1]
    i = pl.program_id(0)
    h = jnp.sin(jnp.dot(z_ref[...], w1_ref[...], precision=hp,
                        preferred_element_type=F32) + b1_ref[...])
    h = jnp.sin(jnp.dot(h, w2_ref[...], precision=hp, preferred_element_type=F32) + b2_ref[...])
    h = jnp.dot(h, w3_ref[...], precision=hp, preferred_element_type=F32)

    @pl.when(i == 0)
    def _():
        nrm_ref[...] = jnp.zeros_like(nrm_ref)

    for part in range(2):
        row0 = i * tm + part * half
        tcol = (row0 + lax.broadcasted_iota(jnp.int32, (half, 1), 0)).astype(F32) * (
            1.0 / (seq - 1))
        hp_part = h[:, part * wd:(part + 1) * wd] * jnp.exp(-tcol * del_ref[...])
        o_ref[part * half:(part + 1) * half, :] = hp_part
        nrm_ref[...] += jnp.sum(jnp.abs(hp_part), axis=0, keepdims=True)


def _hyena_filter(seq, w1, b1, w2, b2, w3):
    bands = (EMB_DIM - 1) // 2
    ang = (2.0 * np.pi * np.arange(seq)[:, None] / seq) * np.linspace(1e-4, bands - 1, bands)
    z = np.zeros((seq, LANES), np.float32)
    z[:, 0] = np.linspace(0.0, 1.0, seq)
    z[:, 1:1 + bands] = np.cos(ang)
    z[:, 1 + bands:EMB_DIM] = -np.sin(ang)
    deltas = np.abs(np.linspace(MIN_DECAY, MAX_DECAY, D_A)).astype(np.float32)
    deltas2 = np.concatenate([deltas, deltas])[None, :]
    tm = min(seq, 1024)
    half = tm // 2
    fo = w1.shape[1]
    zp = jnp.asarray(z.reshape(seq // tm, 2, half, LANES).transpose(0, 2, 1, 3)
                     .reshape(seq // 2, 2 * LANES))
    eye2 = jnp.eye(2, dtype=F32)
    w1p = jnp.kron(eye2, jnp.pad(w1, ((0, LANES - EMB_DIM), (0, 0))))
    w2p = jnp.kron(eye2, w2)
    w3p = jnp.kron(eye2, w3)
    b1p = jnp.tile(b1.reshape(1, fo), (1, 2))
    b2p = jnp.tile(b2.reshape(1, fo), (1, 2))
    fixed = lambda i: (0, 0)
    h, nrm = pl.pallas_call(
        functools.partial(_hyena_filter_kernel, seq=seq),
        grid=(seq // tm,),
        in_specs=[pl.BlockSpec((half, 2 * LANES), lambda i: (i, 0)),
                  pl.BlockSpec((2 * LANES, 2 * fo), fixed),
                  pl.BlockSpec((1, 2 * fo), fixed),
                  pl.BlockSpec((2 * fo, 2 * fo), fixed),
                  pl.BlockSpec((1, 2 * fo), fixed),
                  pl.BlockSpec((2 * fo, 4 * D_A), fixed),
                  pl.BlockSpec((1, 2 * D_A), fixed)],
        out_specs=[pl.BlockSpec((tm, 2 * D_A), lambda i: (i, 0)),
                   pl.BlockSpec((1, 2 * D_A), fixed)],
        out_shape=[jax.ShapeDtypeStruct((seq, 2 * D_A), F32),
                   jax.ShapeDtypeStruct((1, 2 * D_A), F32)],
        compiler_params=_params(("arbitrary",)),
        name="hyena_filter",
    )(zp, w1p, b1p, w2p, b2p, w3p, jnp.asarray(deltas2))
    return h, 1.0 / (nrm[:, :D_A] + nrm[:, D_A:])


def _attn_kernel(cq_ref, cos_ref, sin_ref, wa_ref, wb_ref, k_ref, v_ref, gb_ref, o_ref, *, nseq):
    nt = (((1,), (1,)), ((), ()))
    tq = cq_ref.shape[1]
    cos = jnp.concatenate([cos_ref[...]] * N_HEADS, axis=-1)
    sin = jnp.concatenate([sin_ref[...]] * N_HEADS, axis=-1)
    lane = lax.broadcasted_iota(jnp.int32, (tq, 2 * V_DIM), 1)

    def one_sequence(sq):
        cq = cq_ref[sq]
        q = (_dot(cq, wa_ref[...]) * cos + _dot(cq, wb_ref[...]) * sin) * (ATTN_SCALE * LOG2_E)
        q = q.astype(BF16)

        def scores(h):
            hsl = slice(h * HEAD_PAD, (h + 1) * HEAD_PAD)
            return lax.dot_general(q[:, hsl], k_ref[sq, :, hsl], nt, preferred_element_type=F32)

        look = 1 if k_ref.shape[1] > 2 * tq else N_HEADS - 1
        pending = [scores(h) for h in range(look)]
        outs = []
        for h in range(N_HEADS):
            if h + look < N_HEADS:
                pending.append(scores(h + look))
            sb = pending.pop(0).astype(BF16)
            p = jnp.exp2(sb - jnp.max(sb, axis=-1, keepdims=True))
            of = _dot(p, v_ref[sq, :, h * HEAD_PAD:(h + 1) * HEAD_PAD])
            outs.append(of / of[:, V_DIM:V_DIM + 1])
            if h % 2 == 1:
                vsl = slice((h // 2) * 2 * V_DIM, (h // 2 + 1) * 2 * V_DIM)
                o = jnp.where(lane < V_DIM, outs[h - 1], pltpu.roll(outs[h], V_DIM, 1))
                o_ref[sq, :, vsl] = (o * _silu(gb_ref[sq, :, vsl])).astype(BF16)

    if nseq == 1:
        one_sequence(0)
    else:
        def body(sq, carry):
            one_sequence(sq)
            return carry

        lax.fori_loop(0, nseq, body, 0)


def _attention(cq, cos, sin, wa, wb, k, v, gb, tq, nseq=1):
    bsz, lq, _ = cq.shape
    lk = k.shape[1]
    fixed = lambda b, i: (0, 0)
    qrow = lambda b, i: (b, i, 0)
    kv = lambda b, i: (b, 0, 0)
    pos = lambda b, i: (i, 0)
    return pl.pallas_call(
        functools.partial(_attn_kernel, nseq=nseq),
        grid=(bsz // nseq, lq // tq),
        in_specs=[pl.BlockSpec((nseq, tq, Q_RANK), qrow),
                  pl.BlockSpec((tq, LANES), pos),
                  pl.BlockSpec((tq, LANES), pos),
                  pl.BlockSpec((Q_RANK, N_HEADS * HEAD_PAD), fixed),
                  pl.BlockSpec((Q_RANK, N_HEADS * HEAD_PAD), fixed),
                  pl.BlockSpec((nseq, lk, N_HEADS * HEAD_PAD), kv),
                  pl.BlockSpec((nseq, lk, N_HEADS * HEAD_PAD), kv),
                  pl.BlockSpec((nseq, tq, D_B), qrow)],
        out_specs=pl.BlockSpec((nseq, tq, D_B), qrow),
        out_shape=jax.ShapeDtypeStruct((bsz, lq, D_B), BF16),
        compiler_params=_params(("parallel", "arbitrary")),
        name="mla_attention",
    )(cq, cos, sin, wa, wb, k, v, gb)


def _mid_kernel(x_ref, ya_ref, ob_ref, gt_ref, wo32_ref, sc_ref, sh_ref, wi32_ref,
                x1_ref, u_ref, g_ref, wo_ref, wi_ref):
    @pl.when(pl.program_id(0) == 0)
    def _():
        wo_ref[...] = wo32_ref[...].astype(BF16)
        wi_ref[...] = wi32_ref[...].astype(BF16)

    y = _dot(ya_ref[...].astype(BF16), wo_ref[0:D_A, :]) + _dot(ob_ref[...], wo_ref[D_A:, :])
    x1 = x_ref[...] + gt_ref[...] * y
    x1_ref[...] = x1.astype(x1_ref.dtype)
    h = (_rms(x1) * sc_ref[...] + sh_ref[...]).astype(BF16)
    u_ref[...] = _dot(h, wi_ref[:, :D_C]).astype(BF16)
    g_ref[...] = _dot(h, wi_ref[:, D_C:]).astype(BF16)


def _mid(x2d, ya, ob, gt, wo, sc, sh, wi, tiles_per_mod, tm):
    t = x2d.shape[0]
    row = lambda i: (i, 0)
    fixed = lambda i: (0, 0)
    mod = lambda i: (i // tiles_per_mod, 0, 0)
    return pl.pallas_call(
        _mid_kernel,
        grid=(t // tm,),
        in_specs=[pl.BlockSpec((tm, D_MODEL), row),
                  pl.BlockSpec((tm, D_A), row),
                  pl.BlockSpec((tm, D_B), row),
                  pl.BlockSpec((None, 1, D_MODEL), mod),
                  pl.BlockSpec((D_A + D_B, D_MODEL), fixed),
                  pl.BlockSpec((None, 1, D_MODEL), mod),
                  pl.BlockSpec((None, 1, D_MODEL), mod),
                  pl.BlockSpec((D_MODEL, 2 * D_C), fixed)],
        out_specs=[pl.BlockSpec((tm, D_MODEL), row),
                   pl.BlockSpec((tm, D_C), row),
                   pl.BlockSpec((tm, D_C), row)],
        out_shape=[jax.ShapeDtypeStruct((t, D_MODEL), BF16),
                   jax.ShapeDtypeStruct((t, D_C), BF16),
                   jax.ShapeDtypeStruct((t, D_C), BF16)],
        scratch_shapes=[pltpu.VMEM((D_A + D_B, D_MODEL), BF16),
                        pltpu.VMEM((D_MODEL, 2 * D_C), BF16)],
        compiler_params=_params(("arbitrary",)),
        name="outproj_even_inproj_odd",
    )(x2d, ya, ob, gt, wo, sc, sh, wi)


def _fnet_weights_kernel(w_ref, cs_ref, o_ref):
    hp = lax.Precision.HIGHEST
    o_ref[...] = jnp.dot(cs_ref[...], w_ref[...], precision=hp,
                         preferred_element_type=F32).astype(BF16)


def _fnet_weights(fnet_w, seq):
    c, s = _dft_cos_sin(G_C)
    cs = jnp.asarray((np.concatenate([c, s], axis=0) / math.sqrt(seq * G_C)).astype(np.float32))
    return pl.pallas_call(
        _fnet_weights_kernel,
        grid=(N_GROUPS_C,),
        in_specs=[pl.BlockSpec((None, G_C, G_C), lambda g: (g, 0, 0)),
                  pl.BlockSpec((2 * G_C, G_C), lambda g: (0, 0))],
        out_specs=pl.BlockSpec((None, 2 * G_C, G_C), lambda g: (g, 0, 0)),
        out_shape=jax.ShapeDtypeStruct((N_GROUPS_C, 2 * G_C, G_C), BF16),
        compiler_params=_params(("parallel",)),
        name="fnet_weights",
    )(fnet_w, cs)


def _fnet_long_kernel(u_ref, tw1_ref, m_ref, ab_ref, o_ref, sd_scr, a_scr, *, n1, n2):
    h = n1 // 2
    ct = o_ref.shape[-1]
    for i, n in enumerate(range(1, h)):
        a = u_ref[n].astype(F32)
        b = u_ref[n1 - n].astype(F32)
        sd_scr[0, i] = a + b
        sd_scr[1, i] = a - b
    x0 = u_ref[0].astype(F32)
    xh = u_ref[h].astype(F32)
    zero = jnp.zeros((n2, ct), BF16)
    tw = None
    for k1 in range(h + 1):
        ar = x0 + xh if k1 % 2 == 0 else x0 - xh
        ai = None
        for i, n in enumerate(range(1, h)):
            th = 2.0 * math.pi * n * k1 / n1
            ar = _acc(ar, math.cos(th), sd_scr[0, i])
            ai = _acc(ai, -math.sin(th), sd_scr[1, i])
        if k1 > 0:
            tw = _next_twiddle(tw, tw1_ref)
            ar, ai = _cmul(ar, ai, tw[0], tw[1])
        lanes = slice(k1 * ct, (k1 + 1) * ct)
        a_scr[0:n2, lanes] = ar.astype(BF16)
        a_scr[n2:2 * n2, lanes] = zero if ai is None else ai.astype(BF16)

    def group_maps(k1, xr, xi):
        for gi in range(ct // G_C):
            gl = slice(gi * G_C, (gi + 1) * G_C)
            y = _dot(jnp.concatenate([xr[:, gl], xi[:, gl]], axis=1).astype(BF16), ab_ref[gi])
            o_ref[k1, :, gl] = y.astype(o_ref.dtype)

    def dft(k1):
        rows = 4 * n2 if 0 < k1 < h else 2 * n2
        return _dot(m_ref[0:rows, :], a_scr[:, k1 * ct:(k1 + 1) * ct])

    x_next = dft(0)
    for k1 in range(h + 1):
        x = x_next
        if k1 < h:
            x_next = dft(k1 + 1)
        group_maps(k1, x[:n2], x[n2:2 * n2])
        if 0 < k1 < h:
            group_maps(n1 - k1, x[2 * n2:3 * n2], x[3 * n2:])


def _fnet_long(u, ab, n1, n2):
    bsz, seq, _ = u.shape
    h = n1 // 2
    ng = 2
    ct = ng * G_C
    c, s = _dft_cos_sin(n2)
    ce, se = _dft_cos_sin(n2, shift=1)
    m = _bf16_split(np.block([[c, s], [-s, c], [ce, -se], [-se, -ce]]))[0]
    out = pl.pallas_call(
        functools.partial(_fnet_long_kernel, n1=n1, n2=n2),
        grid=(bsz, N_GROUPS_C // ng),
        in_specs=[pl.BlockSpec((None, n1, n2, ct), lambda b, g: (b, 0, 0, g)),
                  pl.BlockSpec((2, h, n2, ct), lambda b, g: (0, 0, 0, 0)),
                  pl.BlockSpec((4 * n2, 2 * n2), lambda b, g: (0, 0)),
                  pl.BlockSpec((ng, 2 * G_C, G_C), lambda b, g: (g, 0, 0))],
        out_specs=pl.BlockSpec((None, n1, n2, ct), lambda b, g: (b, 0, 0, g)),
        out_shape=jax.ShapeDtypeStruct((bsz, n1, n2, D_C), BF16),
        scratch_shapes=[pltpu.VMEM((2, h - 1, n2, ct), F32),
                        pltpu.VMEM((2 * n2, (h + 1) * ct), BF16)],
        compiler_params=_params(("parallel", "parallel")),
        name="fnet_long",
    )(u.reshape(bsz, n1, n2, D_C), _twiddle1(n1, n2, ct), m, ab)
    return out.transpose(0, 2, 1, 3).reshape(bsz, seq, D_C)


def _fnet_short_kernel(u_ref, f_ref, ab_ref, o_ref, *, seq, scale):
    def body(sq, carry):
        x = _dot(f_ref[...], u_ref[sq])
        xr, xi = x[:seq], x[seq:]
        for g in range(N_GROUPS_C):
            sl = slice(g * G_C, (g + 1) * G_C)
            xin = jnp.concatenate([xr[:, sl], xi[:, sl]], axis=1).astype(BF16)
            o_ref[sq, :, sl] = (_dot(xin, ab_ref[g]) * scale).astype(o_ref.dtype)
        return carry

    lax.fori_loop(0, u_ref.shape[0], body, 0)


def _fnet_short(u, ab, scale, nseq=4):
    bsz, seq, _ = u.shape
    c, s = _dft_cos_sin(seq)
    f = _bf16_split(np.concatenate([c, -s], axis=0))[0]
    return pl.pallas_call(
        functools.partial(_fnet_short_kernel, seq=seq, scale=scale),
        grid=(bsz // nseq,),
        in_specs=[pl.BlockSpec((nseq, seq, D_C), lambda b: (b, 0, 0)),
                  pl.BlockSpec((2 * seq, seq), lambda b: (0, 0)),
                  pl.BlockSpec((N_GROUPS_C, 2 * G_C, G_C), lambda b: (0, 0, 0))],
        out_specs=pl.BlockSpec((nseq, seq, D_C), lambda b: (b, 0, 0)),
        out_shape=jax.ShapeDtypeStruct((bsz, seq, D_C), BF16),
        compiler_params=_params(("parallel",)),
        name="fnet_short",
    )(u, f, ab)


def _final_kernel(x1_ref, y_ref, g_ref, gt_ref, wo32_ref, fg_ref, o_ref, wo_ref):
    @pl.when(pl.program_id(0) == 0)
    def _():
        wo_ref[...] = wo32_ref[...].astype(BF16)

    z = (y_ref[...].astype(F32) * _silu(g_ref[...].astype(F32))).astype(BF16)
    x2 = x1_ref[...].astype(F32) + gt_ref[...] * _dot(z, wo_ref[...])
    o_ref[...] = _rms(x2) * fg_ref[...]


def _final(x1, y, g, gt, wo, fg, tiles_per_mod, tm):
    t = x1.shape[0]
    row = lambda i: (i, 0)
    fixed = lambda i: (0, 0)
    mod = lambda i: (i // tiles_per_mod, 0, 0)
    return pl.pallas_call(
        _final_kernel,
        grid=(t // tm,),
        in_specs=[pl.BlockSpec((tm, D_MODEL), row),
                  pl.BlockSpec((tm, D_C), row),
                  pl.BlockSpec((tm, D_C), row),
                  pl.BlockSpec((None, 1, D_MODEL), mod),
                  pl.BlockSpec((D_C, D_MODEL), fixed),
                  pl.BlockSpec((1, D_MODEL), fixed)],
        out_specs=pl.BlockSpec((tm, D_MODEL), row),
        out_shape=jax.ShapeDtypeStruct((t, D_MODEL), F32),
        scratch_shapes=[pltpu.VMEM((D_C, D_MODEL), BF16)],
        compiler_params=_params(("arbitrary",)),
        name="outproj_odd_final",
    )(x1, y, g, gt, wo, fg)


def _rope_swap(w):
    nf = ROPE_DIM // 4
    w4 = w.reshape(w.shape[:-1] + (2, 2, nf))
    return jnp.stack([-w4[..., 1, :], w4[..., 0, :]], axis=-2).reshape(w.shape)


def _pack_w_in_even(w):
    ua_ga_cq_ckv = w[:, :4 * D_A + Q_RANK + KV_RANK]
    o = 4 * D_A + Q_RANK + KV_RANK
    krope = w[:, o:o + ROPE_DIM]
    gb = w[:, o + ROPE_DIM:]
    zpad = jnp.zeros((w.shape[0], LANES - ROPE_DIM), w.dtype)
    return jnp.concatenate([ua_ga_cq_ckv, gb, krope, zpad, _rope_swap(krope), zpad],
                           axis=1).astype(BF16)


def _pack_w_uq(w):
    w3 = w.reshape(Q_RANK, N_HEADS, QK_DIM)
    zq = jnp.zeros((Q_RANK, N_HEADS, HEAD_PAD - QK_DIM), w.dtype)
    wa = jnp.concatenate([w3, zq], axis=-1)
    zn = jnp.zeros((Q_RANK, N_HEADS, NOPE_DIM), w.dtype)
    wb = jnp.concatenate([zn, _rope_swap(w3[..., NOPE_DIM:]), zq], axis=-1)
    shape = (Q_RANK, N_HEADS * HEAD_PAD)
    return wa.reshape(shape).astype(BF16), wb.reshape(shape).astype(BF16)


def _pack_w_ukv(w):
    w3 = w.reshape(KV_RANK, N_HEADS, NOPE_DIM + V_DIM)
    zk = jnp.zeros((KV_RANK, N_HEADS, HEAD_PAD - NOPE_DIM), w.dtype)
    wk_c = jnp.concatenate([w3[..., :NOPE_DIM], zk], axis=-1).reshape(KV_RANK, -1)
    eye = jnp.eye(LANES, ROPE_DIM, dtype=w.dtype)
    rope_rows = jnp.concatenate([jnp.zeros((LANES, NOPE_DIM), w.dtype), eye,
                                 jnp.zeros((LANES, HEAD_PAD - QK_DIM), w.dtype)], axis=-1)
    wk_r = jnp.tile(rope_rows, (1, N_HEADS))
    wk = jnp.concatenate([wk_c, wk_r], axis=0)
    zv = jnp.zeros((KV_RANK, N_HEADS, HEAD_PAD - V_DIM), w.dtype)
    wv = jnp.concatenate([w3[..., NOPE_DIM:], zv], axis=-1).reshape(KV_RANK, -1)
    ones = np.zeros((1, N_HEADS, HEAD_PAD), np.float32)
    ones[:, :, V_DIM] = 1.0
    return wk.astype(BF16), wv.astype(BF16), jnp.asarray(ones.reshape(1, -1))


def _rope_tables(seq, rotate):
    pos = np.arange(seq)
    half = ROPE_DIM // 2
    inv = ROPE_BASE ** (-np.arange(0, half, 2, dtype=np.float64) / half)
    r = (pos // GRID_W)[:, None] * inv
    c = (pos % GRID_W)[:, None] * inv
    ang = np.concatenate([r, r, c, c], axis=-1) * (1.0 if rotate else 0.0)
    cos_q = np.zeros((seq, LANES))
    sin_q = np.zeros((seq, LANES))
    cos_k = np.zeros((seq, LANES))
    sin_k = np.zeros((seq, LANES))
    cos_q[:, :NOPE_DIM] = 1.0
    cos_q[:, NOPE_DIM:QK_DIM] = np.cos(ang)
    sin_q[:, NOPE_DIM:QK_DIM] = np.sin(ang)
    cos_k[:, :ROPE_DIM] = np.cos(ang)
    sin_k[:, :ROPE_DIM] = np.sin(ang)
    return tuple(jnp.asarray(t.astype(np.float32)) for t in (cos_q, sin_q, cos_k, sin_k))


def kernel(x_prompt, x_sample, c, cache_ckv, cache_krope, c_ctx, norm_g, w_mod, b_mod,
           w_in_e, conv_w, conv_b, filt_w1, filt_b1, filt_w2, filt_b2, filt_w3, hyena_skip,
           q_norm_g, kv_norm_g, w_uq, w_ukv, w_out_e, w_in_o, fnet_w, w_out_o, final_g):
    nb_p, seq_p, _ = x_prompt.shape
    nb_s, seq_s, _ = x_sample.shape
    tp = nb_p * seq_p
    ts = nb_s * seq_s
    tm = 512

    cond = jnp.concatenate([c_ctx[None, :], c, jnp.zeros((8 - 1 - nb_s, D_MODEL), F32)], axis=0)
    mod = _ada_params(cond, w_mod, b_mod)
    shift = mod[:, :, None, :D_MODEL]
    scale = (1.0 + mod[:, :, None, D_MODEL:2 * D_MODEL]) * norm_g[:, None, None, :]
    gate = mod[:, :, None, 2 * D_MODEL:]

    xp = x_prompt.reshape(tp, D_MODEL)
    xs = x_sample.reshape(ts, D_MODEL)

    w_in = _pack_w_in_even(w_in_e[0])
    qg = q_norm_g[0].reshape(1, Q_RANK)
    kvg = kv_norm_g[0].reshape(1, KV_RANK)
    wq_a, wq_b = _pack_w_uq(w_uq[0])
    wk, wv, vone = _pack_w_ukv(w_ukv[0])
    cos_qp, sin_qp, cos_kp, sin_kp = _rope_tables(tm, rotate=False)
    cos_qs, sin_qs, cos_ks, sin_ks = _rope_tables(seq_s, rotate=True)
    pr_p = _inproj_even(xp, scale[0, :1], shift[0, :1], w_in, qg, kvg, cos_kp, sin_kp, wk, wv,
                        vone, tm, True)
    pr_s = _inproj_even(xs, scale[0, 1:1 + nb_s], shift[0, 1:1 + nb_s], w_in, qg, kvg,
                        cos_ks, sin_ks, wk, wv, vone, tm, False,
                        cache=(cache_ckv, cache_krope, 0))
    ua_p, ga_p, cq_p, gb_p, k_p, v_p, ckv_p, kra_p = pr_p
    ua_s, ga_s, cq_s, gb_s, k_s, v_s = pr_s

    filt = (filt_w1[0], filt_b1[0], filt_w2[0], filt_b2[0], filt_w3[0])
    ya = []
    for ua, ga, nb, seq in ((ua_p, ga_p, nb_p, seq_p), (ua_s, ga_s, nb_s, seq_s)):
        h, inv_norm = _hyena_filter(seq, *filt)
        kf = _spectrum(h, inv_norm, seq)
        ya.append(_hyena(ua.reshape(nb, seq, 3 * D_A), ga.reshape(nb, seq, D_A),
                         conv_w[0], conv_b[0], hyena_skip[0], kf))
    ya_p = ya[0].reshape(tp, D_A)
    ya_s = ya[1].reshape(ts, D_A)

    ob_p = _attention(cq_p.reshape(nb_p, seq_p, Q_RANK), cos_qp, sin_qp, wq_a, wq_b,
                      k_p.reshape(nb_p, seq_p, -1), v_p.reshape(nb_p, seq_p, -1),
                      gb_p.reshape(nb_p, seq_p, D_B), seq_p, nseq=4)
    ob_s = _attention(cq_s.reshape(nb_s, seq_s, Q_RANK), cos_qs, sin_qs, wq_a, wq_b, k_s, v_s,
                      gb_s.reshape(nb_s, seq_s, D_B), 512)

    wo_e = w_out_e[0]
    wi_o = w_in_o[0]
    tw = 2 * tm
    x1_p, u_p, g_p = _mid(xp, ya_p, ob_p.reshape(tp, D_B), gate[0, :1], wo_e,
                          scale[1, :1], shift[1, :1], wi_o, tp // tw, tw)
    x1_s, u_s, g_s = _mid(xs, ya_s, ob_s.reshape(ts, D_B), gate[0, 1:1 + nb_s], wo_e,
                          scale[1, 1:1 + nb_s], shift[1, 1:1 + nb_s], wi_o, seq_s // tw, tw)

    ab = _fnet_weights(fnet_w[0], seq_s)
    y_p = _fnet_short(u_p.reshape(nb_p, seq_p, D_C), ab, math.sqrt(seq_s / seq_p))
    y_s = _fnet_long(u_s.reshape(nb_s, seq_s, D_C), ab, 16, seq_s // 16)
    wo_o = w_out_o[0]
    fg = final_g.reshape(1, D_MODEL)
    out_p = _final(x1_p, y_p.reshape(tp, D_C), g_p, gate[1, :1], wo_o, fg, tp // tw, tw)
    out_s = _final(x1_s, y_s.reshape(ts, D_C), g_s, gate[1, 1:1 + nb_s], wo_o, fg,
                   seq_s // tw, tw)

    state_ckv = ckv_p.reshape(nb_p, 1, seq_p, KV_RANK)
    state_krope = kra_p[:, :ROPE_DIM].reshape(nb_p, 1, seq_p, ROPE_DIM)
    return (out_p.reshape(nb_p, seq_p, D_MODEL), out_s.reshape(nb_s, seq_s, D_MODEL),
            state_ckv, state_krope)
```

```python
import functools
import math

import numpy as np
import jax
import jax.numpy as jnp
from jax import lax
from jax.experimental import pallas as pl
from jax.experimental.pallas import tpu as pltpu

F32 = jnp.float32
BF16 = jnp.bfloat16

D_MODEL = 1024
GRID_W = 64
EPS = 1e-6
D_A = 512
EMB_DIM = 33
DECAY_TARGET = 1e-2
MIN_DECAY = math.log(DECAY_TARGET) / 1.5
MAX_DECAY = math.log(DECAY_TARGET) / 0.3
N_HEADS = 8
Q_RANK = 256
KV_RANK = 256
NOPE_DIM = 64
ROPE_DIM = 32
V_DIM = 64
ROPE_BASE = 10000.0
D_B = N_HEADS * V_DIM
QK_DIM = NOPE_DIM + ROPE_DIM
ATTN_SCALE = 1.0 / math.sqrt(QK_DIM)
LOG2_E = math.log2(math.e)
D_C = 1024
N_GROUPS_C = 8
G_C = D_C // N_GROUPS_C

LANES = 128
MXU_LANES = 2 * LANES
HEAD_PAD = LANES
VMEM_LIMIT = 56 * 1024 * 1024

ROW_TILE = 512
WIDE_ROW_TILE = 2 * ROW_TILE
ATTN_Q_TILE = 512
SHORT_SEQ_ROWS = 1024
DFT_MAX_N2 = 512
FNET_RADIX = 16
FILTER_TILE = 1024

_C_UA, _C_GA, _C_CQ, _C_CKV, _C_GB, _C_KRA, _C_KRB, _C_END = (
    0, 1536, 2048, 2304, 2560, 3072, 3200, 3328)


def _params(sem):
    return pltpu.CompilerParams(dimension_semantics=sem, vmem_limit_bytes=VMEM_LIMIT)


def _rms(x):
    return x * lax.rsqrt(jnp.mean(x * x, axis=-1, keepdims=True) + EPS)


def _silu(g):
    return g * jax.nn.sigmoid(g)


def _dot(a, b):
    return jnp.dot(a, b, preferred_element_type=F32)


def _ada_kernel(cond_ref, w_ref, b_ref, o_ref):
    c = _silu(cond_ref[...])
    o_ref[...] = jnp.dot(c, w_ref[...], preferred_element_type=F32,
                         precision=lax.Precision.HIGHEST) + b_ref[...]


def _ada_params(cond, w_mod, b_mod):
    depth = w_mod.shape[0]
    n = cond.shape[0]
    tn = 1024
    return pl.pallas_call(
        _ada_kernel,
        grid=(depth, 3 * D_MODEL // tn),
        in_specs=[pl.BlockSpec((n, D_MODEL), lambda i, j: (0, 0)),
                  pl.BlockSpec((None, D_MODEL, tn), lambda i, j: (i, 0, j)),
                  pl.BlockSpec((None, 1, tn), lambda i, j: (i, 0, j))],
        out_specs=pl.BlockSpec((None, n, tn), lambda i, j: (i, 0, j)),
        out_shape=jax.ShapeDtypeStruct((depth, n, 3 * D_MODEL), F32),
        compiler_params=_params(("arbitrary", "arbitrary")),
        name="ada_params",
    )(cond, w_mod, b_mod.reshape(depth, 1, 3 * D_MODEL))


def _inproj_even_kernel(x_ref, sc_ref, sh_ref, w_ref, qg_ref, kvg_ref, cos_ref, sin_ref,
                        wk_ref, wv_ref, vone_ref, *refs, nctx, emit_state):
    if nctx:
        cckv_ref, ckr_ref = refs[:2]
        refs = refs[2:]
    ua_ref, ga_ref, cq_ref, gb_ref, k_ref, v_ref = refs[:6]
    state_refs = refs[6:]

    def tokens():
        h = (_rms(x_ref[...]) * sc_ref[...] + sh_ref[...]).astype(BF16)

        def proj(a, b):
            return _dot(h, w_ref[:, a:b])

        ua_ref[...] = proj(_C_UA, _C_GA)
        ga_ref[...] = proj(_C_GA, _C_CQ)
        cq_ref[...] = (_rms(proj(_C_CQ, _C_CKV)) * qg_ref[...]).astype(BF16)
        gb_ref[...] = proj(_C_GB, _C_KRA)
        ckv = _rms(proj(_C_CKV, _C_GB)) * kvg_ref[...]
        kr2 = proj(_C_KRA, _C_END)
        kra = kr2[:, :LANES]
        krb = kr2[:, LANES:]
        c = ckv.astype(BF16)
        kr = pltpu.roll(kra * cos_ref[...] + krb * sin_ref[...], NOPE_DIM, 1)
        k = _dot(c, wk_ref[0:KV_RANK, :]) + jnp.concatenate([kr] * N_HEADS, axis=-1)
        k_ref[...] = k.astype(BF16)
        v_ref[...] = (_dot(c, wv_ref[...]) + vone_ref[...]).astype(BF16)
        if emit_state:
            state_refs[0][...] = ckv
            state_refs[1][...] = kra

    if not nctx:
        tokens()
        return

    t = pl.program_id(1)

    @pl.when(t < nctx)
    def _():
        c = cckv_ref[...].astype(BF16)
        kr = ckr_ref[...].astype(BF16)
        k_ref[...] = (_dot(c, wk_ref[0:KV_RANK, :])
                      + _dot(kr, wk_ref[KV_RANK:KV_RANK + ROPE_DIM, :])).astype(BF16)
        v_ref[...] = (_dot(c, wv_ref[...]) + vone_ref[...]).astype(BF16)

    pl.when(t >= nctx)(tokens)


def _inproj_even(x2d, sc, sh, w, qg, kvg, cos_k, sin_k, wk, wv, vone, tm, emit_state,
                 cache=None):
    t = x2d.shape[0]
    nb = sc.shape[0]
    tiles = t // nb // tm
    tiles_per_seq = cos_k.shape[0] // tm
    nctx = 0 if cache is None else cache[0].shape[2] // tm
    tok = lambda b, s: b * tiles + jnp.maximum(s - nctx, 0)
    row = lambda b, s: (tok(b, s), 0)
    fixed = lambda b, s: (0, 0)
    mod = lambda b, s: (b, 0, 0)
    pos = lambda b, s: (jnp.maximum(s - nctx, 0) % tiles_per_seq, 0)
    kvo = lambda b, s: (b, s, 0)
    widths = (3 * D_A, D_A, Q_RANK, D_B)
    dtypes = (F32, F32, BF16, F32)
    in_specs = [pl.BlockSpec((tm, D_MODEL), row),
                pl.BlockSpec((None, 1, D_MODEL), mod),
                pl.BlockSpec((None, 1, D_MODEL), mod),
                pl.BlockSpec((D_MODEL, _C_END), fixed),
                pl.BlockSpec((1, Q_RANK), fixed),
                pl.BlockSpec((1, KV_RANK), fixed),
                pl.BlockSpec((tm, LANES), pos),
                pl.BlockSpec((tm, LANES), pos),
                pl.BlockSpec((KV_RANK + LANES, N_HEADS * HEAD_PAD), fixed),
                pl.BlockSpec((KV_RANK, N_HEADS * HEAD_PAD), fixed),
                pl.BlockSpec((1, N_HEADS * HEAD_PAD), fixed)]
    args = [x2d, sc, sh, w, qg, kvg, cos_k, sin_k, wk, wv, vone]
    if nctx:
        layer = cache[2]
        ctx = lambda b, s: (b, layer, jnp.minimum(s, nctx - 1), 0)
        in_specs += [pl.BlockSpec((None, None, tm, KV_RANK), ctx),
                     pl.BlockSpec((None, None, tm, ROPE_DIM), ctx)]
        args += [cache[0], cache[1]]
    out_specs = [pl.BlockSpec((tm, wd), row) for wd in widths]
    out_shape = [jax.ShapeDtypeStruct((t, wd), dt) for wd, dt in zip(widths, dtypes)]
    for wd in (N_HEADS * HEAD_PAD, N_HEADS * HEAD_PAD):
        out_specs.append(pl.BlockSpec((None, tm, wd), kvo))
        out_shape.append(jax.ShapeDtypeStruct((nb, (nctx + tiles) * tm, wd), BF16))
    if emit_state:
        for wd in (KV_RANK, LANES):
            out_specs.append(pl.BlockSpec((tm, wd), row))
            out_shape.append(jax.ShapeDtypeStruct((t, wd), F32))
    return pl.pallas_call(
        functools.partial(_inproj_even_kernel, nctx=nctx, emit_state=emit_state),
        grid=(nb, nctx + tiles),
        in_specs=in_specs,
        out_specs=out_specs,
        out_shape=out_shape,
        compiler_params=_params(("parallel", "arbitrary")),
        name="inproj_even",
    )(*args)


def _acc(acc, coef, x):
    if x is None or abs(coef) < 1e-12:
        return acc
    if abs(abs(coef) - 1.0) < 1e-12:
        if acc is None:
            return x if coef > 0 else -x
        return acc + x if coef > 0 else acc - x
    return coef * x if acc is None else acc + coef * x


def _add(a, b):
    if a is None:
        return b
    return a if b is None else a + b


def _sub(a, b):
    if b is None:
        return a
    return -b if a is None else a - b


def _cmul(ar, ai, br, bi):
    def mul(x, y):
        return None if x is None or y is None else x * y
    return _sub(mul(ar, br), mul(ai, bi)), _add(mul(ar, bi), mul(ai, br))


def _radix_parts(h):
    singles = [0] + ([h // 2] if h >= 2 and h % 2 == 0 else [])
    pairs = [(n, h - n) for n in range(1, (h + 1) // 2)]
    return singles, pairs


def _radix_fwd(k1, h, single, pair_sum, pair_diff):
    singles, pairs = _radix_parts(h)
    ar = ai = None
    for n in singles:
        th = 2.0 * math.pi * n * k1 / (2 * h)
        x = single(n)
        ar = _acc(ar, math.cos(th), x)
        ai = _acc(ai, -math.sin(th), x)
    for i, (n, _) in enumerate(pairs):
        th = 2.0 * math.pi * n * k1 / (2 * h)
        re_src, im_src = (pair_sum, pair_diff) if k1 % 2 == 0 else (pair_diff, pair_sum)
        ar = _acc(ar, math.cos(th), re_src(i))
        ai = _acc(ai, -math.sin(th), im_src(i))
    return ar, ai


def _radix_inv(h, n_total, bpr, bpi, emit):
    singles, pairs = _radix_parts(h)

    def coefs(n, k1):
        th = 2.0 * math.pi * n * k1 / (2 * h)
        w = (1.0 if k1 in (0, h) else 2.0) / n_total
        return w * math.cos(th), w * math.sin(th)

    for n in singles:
        y = None
        for k1 in range(h + 1):
            c, s = coefs(n, k1)
            y = _acc(y, c, bpr(k1) if abs(c) > 1e-12 else None)
            y = _acc(y, -s, bpi(k1) if abs(s) > 1e-12 else None)
        emit(n, y)
    for n, m in pairs:
        acc = {(0, "c"): None, (0, "s"): None, (1, "c"): None, (1, "s"): None}
        for k1 in range(h + 1):
            c, s = coefs(n, k1)
            par = k1 % 2
            acc[(par, "c")] = _acc(acc[(par, "c")], c, bpr(k1) if abs(c) > 1e-12 else None)
            acc[(par, "s")] = _acc(acc[(par, "s")], s, bpi(k1) if abs(s) > 1e-12 else None)
        even_m, even_p = _sub(acc[(0, "c")], acc[(0, "s")]), _add(acc[(0, "c")], acc[(0, "s")])
        odd_m, odd_p = _sub(acc[(1, "c")], acc[(1, "s")]), _add(acc[(1, "c")], acc[(1, "s")])
        emit(n, _add(even_m, odd_m))
        emit(m, _sub(even_p, odd_p))


def _next_twiddle(tw, tw_ref):
    row = 0 if tw is None else tw[2] + 1
    return tw_ref[0, row], tw_ref[1, row], row


def _spectrum_kernel(hf_ref, hb_ref, inv_ref, tw1_ref, fh_ref, kf_ref, sd_scr,
                     *, h, n2, nk_pad):
    ct = hf_ref.shape[-1]
    row = lax.broadcasted_iota(jnp.int32, (n2, ct), 0)
    _, pairs = _radix_parts(h)

    def hb(n):
        return jnp.where(row == 0, 0.0, hb_ref[0]) if n == 0 else hb_ref[n]

    for i, (n, m) in enumerate(pairs):
        sd_scr[0, 0, i] = hf_ref[n] + hf_ref[m]
        sd_scr[0, 1, i] = hf_ref[n] - hf_ref[m]
        sd_scr[1, 0, i] = hb(n) + hb(m)
        sd_scr[1, 1, i] = hb(n) - hb(m)

    inv = inv_ref[...]
    tw = None
    for k1 in range(h + 1):
        far, fai = _radix_fwd(k1, h, lambda n: hf_ref[n],
                              lambda i: sd_scr[0, 0, i], lambda i: sd_scr[0, 1, i])
        bar, bai = _radix_fwd(k1, h, hb, lambda i: sd_scr[1, 0, i], lambda i: sd_scr[1, 1, i])
        if k1 > 0:
            tw = _next_twiddle(tw, tw1_ref)
            far, fai = _cmul(far, fai, tw[0], tw[1])
            bar, bai = _cmul(bar, bai, tw[0], tw[1])
        zero = jnp.zeros((n2, ct), F32)
        af = jnp.concatenate([zero if far is None else far, zero if fai is None else fai], axis=0)
        ab = jnp.concatenate([zero if bar is None else bar, zero if bai is None else bai], axis=0)
        a = jnp.concatenate([af, ab], axis=1)
        x = _dot(fh_ref[...], a.astype(BF16))
        lanes = slice(k1 * ct, (k1 + 1) * ct)
        kf_ref[0, :, lanes] = (x[:n2, :ct] + x[:n2, ct:]) * inv
        kf_ref[1, :, lanes] = (x[n2:, :ct] - x[n2:, ct:]) * inv
    if nk_pad > h + 1:
        pad = slice((h + 1) * ct, nk_pad * ct)
        kf_ref[:, :, pad] = jnp.zeros((2, n2, (nk_pad - h - 1) * ct), F32)


def _hyena_kernel(x0_ref, x1_ref, v_ref, ga_ref, *refs, **plan):
    o_ref = refs[10]
    scratch = refs[11:]
    par = scratch[0].shape[0]

    def body(trip, carry):
        for slot in range(par):
            sq = trip * par + slot
            _hyena_sequence(x0_ref.at[sq], x1_ref.at[sq], v_ref.at[sq], ga_ref.at[sq],
                            *refs[:10], o_ref.at[sq], *[s.at[slot] for s in scratch], **plan)
        return carry

    trips = o_ref.shape[0] // par
    if trips == 1:
        body(0, 0)
    else:
        lax.fori_loop(0, trips, body, 0)


def _hyena_sequence(x0_ref, x1_ref, v_ref, ga_ref,
                    w0_ref, w1_ref, wv_ref, b0_ref, b1_ref, bv_ref, skip_ref,
                    kf_ref, tw1_ref, f_ref,
                    o_ref, v_scr, sd_scr, a_scr, b_scr, *, seq, nslab, n2, nk_pad):
    ct = o_ref.shape[-1]
    h = nslab
    chunk = MXU_LANES
    kpc = chunk // ct
    row = lax.broadcasted_iota(jnp.int32, (n2, ct), 0)

    def short_conv(ref, w_ref, b_ref, s, start):
        u = ref[pl.ds(start, n2), :]
        if isinstance(s, int):
            zero_row = jnp.zeros((1, ct), F32)
            prev = zero_row if s == 0 else ref[pl.ds(start - 1, 1), :]
            nxt = zero_row if s == nslab - 1 else ref[pl.ds(start + n2, 1), :]
        else:
            prev = ref[pl.ds(jnp.maximum(start - 1, 0), 1), :]
            prev = jnp.where(s == 0, 0.0, prev)
            nxt = ref[pl.ds(jnp.minimum(start + n2, seq - 1), 1), :]
            nxt = jnp.where(s == nslab - 1, 0.0, nxt)
        up = jnp.where(row == 0, prev, pltpu.roll(u, 1, 0))
        un = jnp.where(row == n2 - 1, nxt, pltpu.roll(u, n2 - 1, 0))
        return b_ref[...] + up * w_ref[0:1, :] + u * w_ref[1:2, :] + un * w_ref[2:3, :]

    def interior_conv(ref, w_ref, b_ref, s, start):
        up = ref[pl.ds(start - 1, n2), :]
        u = ref[pl.ds(start, n2), :]
        un = ref[pl.ds(start + 1, n2), :]
        return b_ref[...] + up * w_ref[0:1, :] + u * w_ref[1:2, :] + un * w_ref[2:3, :]

    def prep_slab(s, start, conv):
        x0 = conv(x0_ref, w0_ref, b0_ref, s, start)
        x1 = conv(x1_ref, w1_ref, b1_ref, s, start)
        v_scr[s] = conv(v_ref, wv_ref, bv_ref, s, start) * x1
        o_ref[pl.ds(start, n2), :] = x0 * _silu(ga_ref[pl.ds(start, n2), :])

    def prep(s, carry):
        prep_slab(s, s * n2, interior_conv)
        return carry

    for s in sorted({0, nslab - 1}):
        prep_slab(s, s * n2, short_conv)
    if nslab > 2:
        lax.fori_loop(1, nslab - 1, prep, 0)

    _, pairs = _radix_parts(h)
    for i, (n, m) in enumerate(pairs):
        sd_scr[0, i] = v_scr[n] + v_scr[m]
        sd_scr[1, i] = v_scr[n] - v_scr[m]
    zero = jnp.zeros((n2, ct), BF16)

    def stage1(k1):
        lanes = slice(k1 * ct, (k1 + 1) * ct)
        ar = ai = None
        if k1 <= h:
            ar, ai = _radix_fwd(k1, h, lambda n: v_scr[n], lambda i: sd_scr[0, i],
                                lambda i: sd_scr[1, i])
        if 0 < k1 <= h:
            ar, ai = _cmul(ar, ai, tw1_ref[0, k1 - 1], tw1_ref[1, k1 - 1])
        a_scr[0:n2, lanes] = zero if ar is None else ar.astype(BF16)
        a_scr[n2:2 * n2, lanes] = zero if ai is None else ai.astype(BF16)

    def dft(j):
        for k1 in range(j * kpc, (j + 1) * kpc):
            stage1(k1)
        return _dot(f_ref[...], a_scr[:, j * chunk:(j + 1) * chunk])

    tw = None
    nchunk = nk_pad // kpc
    x_next = dft(0)
    for j in range(nchunk):
        cols = slice(j * chunk, (j + 1) * chunk)
        x = x_next
        if j + 1 < nchunk:
            x_next = dft(j + 1)
        xr, xi = x[:n2], x[n2:]
        kr = kf_ref[0, :, cols]
        ki = kf_ref[1, :, cols]
        z = jnp.concatenate([xr * kr - xi * ki, -(xr * ki + xi * kr)], axis=0)
        bt = _dot(f_ref[...], z.astype(BF16))
        for kk in range(kpc):
            k1 = j * kpc + kk
            if k1 > h:
                continue
            lanes = slice(kk * ct, (kk + 1) * ct)
            br = bt[:n2, lanes]
            bi = -bt[n2:, lanes]
            if k1 > 0:
                tw = _next_twiddle(tw, tw1_ref)
                br, bi = _cmul(br, bi, tw[0], -tw[1])
            b_scr[0, k1] = br
            if k1 not in (0, h):
                b_scr[1, k1] = bi

    def emit(n, y):
        rows = pl.ds(n * n2, n2)
        o_ref[rows, :] = (y + v_scr[n] * skip_ref[...]) * o_ref[rows, :]

    _radix_inv(h, 2 * seq, lambda k1: b_scr[0, k1], lambda k1: b_scr[1, k1], emit)


def _dft_cos_sin(n2, shift=0):
    idx = np.arange(n2)
    ang = 2.0 * np.pi * (((idx[:, None] + shift) * idx[None, :]) % n2) / n2
    return np.cos(ang), np.sin(ang)


def _bf16_split(x):
    x32 = jnp.asarray(x.astype(np.float32))
    hi = x32.astype(BF16)
    lo = (x32 - hi.astype(F32)).astype(BF16)
    return hi, lo


def _dft_block(n2):
    c, s = _dft_cos_sin(n2)
    return np.block([[c, s], [-s, c]])


def _hyena_plan(seq):
    n2 = min(seq, DFT_MAX_N2)
    n1 = 2 * seq // n2
    ct = LANES if seq > DFT_MAX_N2 else MXU_LANES
    kpc = MXU_LANES // ct
    nk_pad = -(-(n1 // 2 + 1) // kpc) * kpc
    return n1, n2, ct, nk_pad


def _twiddle1(n1, n2, ct):
    ang = 2.0 * np.pi * np.arange(1, n1 // 2 + 1)[:, None] * np.arange(n2)[None, :] / (n1 * n2)
    tw = np.stack([np.cos(ang), -np.sin(ang)]).astype(np.float32)
    return jnp.broadcast_to(jnp.asarray(tw)[..., None], tw.shape + (ct,))


def _spectrum(h, inv_norm, seq):
    n1, n2, ct, nk_pad = _hyena_plan(seq)
    nslab = n1 // 2
    nct = D_A // ct
    npair = len(_radix_parts(nslab)[1])
    ffwd = _bf16_split(_dft_block(n2))[0]
    h3 = h.reshape(nslab, n2, 2 * D_A)
    fixed2 = lambda c: (0, 0)
    return pl.pallas_call(
        functools.partial(_spectrum_kernel, h=nslab, n2=n2, nk_pad=nk_pad),
        grid=(nct,),
        in_specs=[pl.BlockSpec((nslab, n2, ct), lambda c: (0, 0, c)),
                  pl.BlockSpec((nslab, n2, ct), lambda c: (0, 0, nct + c)),
                  pl.BlockSpec((1, ct), lambda c: (0, c)),
                  pl.BlockSpec((2, nslab, n2, ct), lambda c: (0, 0, 0, 0)),
                  pl.BlockSpec((2 * n2, 2 * n2), fixed2)],
        out_specs=pl.BlockSpec((None, 2, n2, nk_pad * ct), lambda c: (c, 0, 0, 0)),
        out_shape=jax.ShapeDtypeStruct((nct, 2, n2, nk_pad * ct), F32),
        scratch_shapes=[pltpu.VMEM((2, 2, max(npair, 1), n2, ct), F32)],
        compiler_params=_params(("parallel",)),
        name="filter_spectrum",
    )(h3, h3, inv_norm, _twiddle1(n1, n2, ct), ffwd)


def _hyena(ua, ga, conv_w, conv_b, skip, kf):
    bsz, seq, _ = ua.shape
    n1, n2, ct, nk_pad = _hyena_plan(seq)
    nslab = n1 // 2
    nct = D_A // ct
    npair = len(_radix_parts(nslab)[1])
    cb = conv_b.reshape(1, 3 * D_A)
    sk = skip.reshape(1, D_A)
    nseq = max(1, SHORT_SEQ_ROWS // seq)
    par = min(nseq, 2)
    part = lambda p: pl.BlockSpec((nseq, seq, ct), lambda c, b: (b, 0, p * nct + c))
    wpart = lambda p: pl.BlockSpec((3, ct), lambda c, b: (0, p * nct + c))
    bpart = lambda p: pl.BlockSpec((1, ct), lambda c, b: (0, p * nct + c))
    return pl.pallas_call(
        functools.partial(_hyena_kernel, seq=seq, nslab=nslab, n2=n2, nk_pad=nk_pad),
        grid=(nct, bsz // nseq),
        in_specs=[part(0), part(1), part(2),
                  pl.BlockSpec((nseq, seq, ct), lambda c, b: (b, 0, c)),
                  wpart(0), wpart(1), wpart(2), bpart(0), bpart(1), bpart(2),
                  pl.BlockSpec((1, ct), lambda c, b: (0, c)),
                  pl.BlockSpec((None, 2, n2, nk_pad * ct), lambda c, b: (c, 0, 0, 0)),
                  pl.BlockSpec((2, nslab, n2, ct), lambda c, b: (0, 0, 0, 0)),
                  pl.BlockSpec((2 * n2, 2 * n2), lambda c, b: (0, 0))],
        out_specs=pl.BlockSpec((nseq, seq, ct), lambda c, b: (b, 0, c)),
        out_shape=jax.ShapeDtypeStruct((bsz, seq, D_A), F32),
        scratch_shapes=[pltpu.VMEM((par, nslab, n2, ct), F32),
                        pltpu.VMEM((par, 2, max(npair, 1), n2, ct), F32),
                        pltpu.VMEM((par, 2 * n2, nk_pad * ct), BF16),
                        pltpu.VMEM((par, 2, nk_pad, n2, ct), F32)],
        compiler_params=_params(("parallel", "parallel")),
        name="hyena",
    )(ua, ua, ua, ga, conv_w, conv_w, conv_w, cb, cb, cb, sk, kf, _twiddle1(n1, n2, ct),
      _bf16_split(_dft_block(n2))[0])


def _hyena_filter_kernel(z_ref, w1_ref, b1_ref, w2_ref, b2_ref, w3_ref, del_ref,
                         o_ref, nrm_ref, *, seq):
    hp = lax.Precision.HIGHEST
    tm = o_ref.shape[0]
    half = tm // 2
    wd = o_ref.shape[1]
    i = pl.program_id(0)
    h = jnp.sin(jnp.dot(z_ref[...], w1_ref[...], precision=hp,
                        preferred_element_type=F32) + b1_ref[...])
    h = jnp.sin(jnp.dot(h, w2_ref[...], precision=hp, preferred_element_type=F32) + b2_ref[...])
    h = jnp.dot(h, w3_ref[...], precision=hp, preferred_element_type=F32)

    @pl.when(i == 0)
    def _():
        nrm_ref[...] = jnp.zeros_like(nrm_ref)

    for part in range(2):
        row0 = i * tm + part * half
        tcol = (row0 + lax.broadcasted_iota(jnp.int32, (half, 1), 0)).astype(F32) * (
            1.0 / (seq - 1))
        hp_part = h[:, part * wd:(part + 1) * wd] * jnp.exp(-tcol * del_ref[...])
        o_ref[part * half:(part + 1) * half, :] = hp_part
        nrm_ref[...] += jnp.sum(jnp.abs(hp_part), axis=0, keepdims=True)


def _hyena_filter(seq, w1, b1, w2, b2, w3):
    bands = (EMB_DIM - 1) // 2
    ang = (2.0 * np.pi * np.arange(seq)[:, None] / seq) * np.linspace(1e-4, bands - 1, bands)
    z = np.zeros((seq, LANES), np.float32)
    z[:, 0] = np.linspace(0.0, 1.0, seq)
    z[:, 1:1 + bands] = np.cos(ang)
    z[:, 1 + bands:EMB_DIM] = -np.sin(ang)
    deltas = np.abs(np.linspace(MIN_DECAY, MAX_DECAY, D_A)).astype(np.float32)
    deltas2 = np.concatenate([deltas, deltas])[None, :]
    tm = min(seq, FILTER_TILE)
    half = tm // 2
    fo = w1.shape[1]
    zp = jnp.asarray(z.reshape(seq // tm, 2, half, LANES).transpose(0, 2, 1, 3)
                     .reshape(seq // 2, 2 * LANES))
    eye2 = jnp.eye(2, dtype=F32)
    w1p = jnp.kron(eye2, jnp.pad(w1, ((0, LANES - EMB_DIM), (0, 0))))
    w2p = jnp.kron(eye2, w2)
    w3p = jnp.kron(eye2, w3)
    b1p = jnp.tile(b1.reshape(1, fo), (1, 2))
    b2p = jnp.tile(b2.reshape(1, fo), (1, 2))
    fixed = lambda i: (0, 0)
    h, nrm = pl.pallas_call(
        functools.partial(_hyena_filter_kernel, seq=seq),
        grid=(seq // tm,),
        in_specs=[pl.BlockSpec((half, 2 * LANES), lambda i: (i, 0)),
                  pl.BlockSpec((2 * LANES, 2 * fo), fixed),
                  pl.BlockSpec((1, 2 * fo), fixed),
                  pl.BlockSpec((2 * fo, 2 * fo), fixed),
                  pl.BlockSpec((1, 2 * fo), fixed),
                  pl.BlockSpec((2 * fo, 4 * D_A), fixed),
                  pl.BlockSpec((1, 2 * D_A), fixed)],
        out_specs=[pl.BlockSpec((tm, 2 * D_A), lambda i: (i, 0)),
                   pl.BlockSpec((1, 2 * D_A), fixed)],
        out_shape=[jax.ShapeDtypeStruct((seq, 2 * D_A), F32),
                   jax.ShapeDtypeStruct((1, 2 * D_A), F32)],
        compiler_params=_params(("arbitrary",)),
        name="hyena_filter",
    )(zp, w1p, b1p, w2p, b2p, w3p, jnp.asarray(deltas2))
    return h, 1.0 / (nrm[:, :D_A] + nrm[:, D_A:])


def _attn_kernel(cq_ref, cos_ref, sin_ref, wa_ref, wb_ref, k_ref, v_ref, gb_ref, o_ref, *, nseq):
    nt = (((1,), (1,)), ((), ()))
    tq = cq_ref.shape[1]
    cos = jnp.concatenate([cos_ref[...]] * N_HEADS, axis=-1)
    sin = jnp.concatenate([sin_ref[...]] * N_HEADS, axis=-1)
    lane = lax.broadcasted_iota(jnp.int32, (tq, 2 * V_DIM), 1)

    def one_sequence(sq):
        cq = cq_ref[sq]
        q = (_dot(cq, wa_ref[...]) * cos + _dot(cq, wb_ref[...]) * sin) * (ATTN_SCALE * LOG2_E)
        q = q.astype(BF16)

        def scores(h):
            hsl = slice(h * HEAD_PAD, (h + 1) * HEAD_PAD)
            return lax.dot_general(q[:, hsl], k_ref[sq, :, hsl], nt, preferred_element_type=F32)

        look = 1 if k_ref.shape[1] > 2 * tq else N_HEADS - 1
        pending = [scores(h) for h in range(look)]
        outs = []
        for h in range(N_HEADS):
            if h + look < N_HEADS:
                pending.append(scores(h + look))
            sb = pending.pop(0).astype(BF16)
            p = jnp.exp2(sb - jnp.max(sb, axis=-1, keepdims=True))
            of = _dot(p, v_ref[sq, :, h * HEAD_PAD:(h + 1) * HEAD_PAD])
            outs.append(of / of[:, V_DIM:V_DIM + 1])
            if h % 2 == 1:
                vsl = slice((h // 2) * 2 * V_DIM, (h // 2 + 1) * 2 * V_DIM)
                o = jnp.where(lane < V_DIM, outs[h - 1], pltpu.roll(outs[h], V_DIM, 1))
                o_ref[sq, :, vsl] = (o * _silu(gb_ref[sq, :, vsl])).astype(BF16)

    if nseq == 1:
        one_sequence(0)
    else:
        def body(trip, carry):
            one_sequence(2 * trip)
            one_sequence(2 * trip + 1)
            return carry

        lax.fori_loop(0, nseq // 2, body, 0)


def _attention(cq, cos, sin, wa, wb, k, v, gb, tq, nseq=1):
    bsz, lq, _ = cq.shape
    lk = k.shape[1]
    fixed = lambda b, i: (0, 0)
    qrow = lambda b, i: (b, i, 0)
    kv = lambda b, i: (b, 0, 0)
    pos = lambda b, i: (i, 0)
    return pl.pallas_call(
        functools.partial(_attn_kernel, nseq=nseq),
        grid=(bsz // nseq, lq // tq),
        in_specs=[pl.BlockSpec((nseq, tq, Q_RANK), qrow),
                  pl.BlockSpec((tq, LANES), pos),
                  pl.BlockSpec((tq, LANES), pos),
                  pl.BlockSpec((Q_RANK, N_HEADS * HEAD_PAD), fixed),
                  pl.BlockSpec((Q_RANK, N_HEADS * HEAD_PAD), fixed),
                  pl.BlockSpec((nseq, lk, N_HEADS * HEAD_PAD), kv),
                  pl.BlockSpec((nseq, lk, N_HEADS * HEAD_PAD), kv),
                  pl.BlockSpec((nseq, tq, D_B), qrow)],
        out_specs=pl.BlockSpec((nseq, tq, D_B), qrow),
        out_shape=jax.ShapeDtypeStruct((bsz, lq, D_B), BF16),
        compiler_params=_params(("parallel", "arbitrary")),
        name="mla_attention",
    )(cq, cos, sin, wa, wb, k, v, gb)


def _mid_kernel(x_ref, ya_ref, ob_ref, gt_ref, wo32_ref, sc_ref, sh_ref, wi32_ref,
                x1_ref, u_ref, g_ref, wo_ref, wi_ref):
    @pl.when(pl.program_id(0) == 0)
    def _():
        wo_ref[...] = wo32_ref[...].astype(BF16)
        wi_ref[...] = wi32_ref[...].astype(BF16)

    y = _dot(ya_ref[...].astype(BF16), wo_ref[0:D_A, :]) + _dot(ob_ref[...], wo_ref[D_A:, :])
    x1 = x_ref[...] + gt_ref[...] * y
    x1_ref[...] = x1.astype(x1_ref.dtype)
    h = (_rms(x1) * sc_ref[...] + sh_ref[...]).astype(BF16)
    u_ref[...] = _dot(h, wi_ref[:, :D_C]).astype(BF16)
    g_ref[...] = _dot(h, wi_ref[:, D_C:]).astype(BF16)


def _mid(x2d, ya, ob, gt, wo, sc, sh, wi, tiles_per_mod, tm):
    t = x2d.shape[0]
    row = lambda i: (i, 0)
    fixed = lambda i: (0, 0)
    mod = lambda i: (i // tiles_per_mod, 0, 0)
    return pl.pallas_call(
        _mid_kernel,
        grid=(t // tm,),
        in_specs=[pl.BlockSpec((tm, D_MODEL), row),
                  pl.BlockSpec((tm, D_A), row),
                  pl.BlockSpec((tm, D_B), row),
                  pl.BlockSpec((None, 1, D_MODEL), mod),
                  pl.BlockSpec((D_A + D_B, D_MODEL), fixed),
                  pl.BlockSpec((None, 1, D_MODEL), mod),
                  pl.BlockSpec((None, 1, D_MODEL), mod),
                  pl.BlockSpec((D_MODEL, 2 * D_C), fixed)],
        out_specs=[pl.BlockSpec((tm, D_MODEL), row),
                   pl.BlockSpec((tm, D_C), row),
                   pl.BlockSpec((tm, D_C), row)],
        out_shape=[jax.ShapeDtypeStruct((t, D_MODEL), BF16),
                   jax.ShapeDtypeStruct((t, D_C), BF16),
                   jax.ShapeDtypeStruct((t, D_C), BF16)],
        scratch_shapes=[pltpu.VMEM((D_A + D_B, D_MODEL), BF16),
                        pltpu.VMEM((D_MODEL, 2 * D_C), BF16)],
        compiler_params=_params(("arbitrary",)),
        name="outproj_even_inproj_odd",
    )(x2d, ya, ob, gt, wo, sc, sh, wi)


def _fnet_weights_kernel(w_ref, cs_ref, o_ref):
    hp = lax.Precision.HIGHEST
    o_ref[...] = jnp.dot(cs_ref[...], w_ref[...], precision=hp,
                         preferred_element_type=F32).astype(BF16)


def _fnet_weights(fnet_w, seq):
    c, s = _dft_cos_sin(G_C)
    cs = jnp.asarray((np.concatenate([c, s], axis=0) / math.sqrt(seq * G_C)).astype(np.float32))
    return pl.pallas_call(
        _fnet_weights_kernel,
        grid=(N_GROUPS_C,),
        in_specs=[pl.BlockSpec((None, G_C, G_C), lambda g: (g, 0, 0)),
                  pl.BlockSpec((2 * G_C, G_C), lambda g: (0, 0))],
        out_specs=pl.BlockSpec((None, 2 * G_C, G_C), lambda g: (g, 0, 0)),
        out_shape=jax.ShapeDtypeStruct((N_GROUPS_C, 2 * G_C, G_C), BF16),
        compiler_params=_params(("parallel",)),
        name="fnet_weights",
    )(fnet_w, cs)


def _fnet_long_kernel(u_ref, tw1_ref, m_ref, ab_ref, o_ref, sd_scr, a_scr, *, n1, n2):
    h = n1 // 2
    ct = o_ref.shape[-1]
    for i, n in enumerate(range(1, h)):
        a = u_ref[n].astype(F32)
        b = u_ref[n1 - n].astype(F32)
        sd_scr[0, i] = a + b
        sd_scr[1, i] = a - b
    x0 = u_ref[0].astype(F32)
    xh = u_ref[h].astype(F32)
    zero = jnp.zeros((n2, ct), BF16)

    def stage1(k1):
        ar = x0 + xh if k1 % 2 == 0 else x0 - xh
        ai = None
        for i, n in enumerate(range(1, h)):
            th = 2.0 * math.pi * n * k1 / n1
            ar = _acc(ar, math.cos(th), sd_scr[0, i])
            ai = _acc(ai, -math.sin(th), sd_scr[1, i])
        if k1 > 0:
            ar, ai = _cmul(ar, ai, tw1_ref[0, k1 - 1], tw1_ref[1, k1 - 1])
        lanes = slice(k1 * ct, (k1 + 1) * ct)
        a_scr[0:n2, lanes] = ar.astype(BF16)
        a_scr[n2:2 * n2, lanes] = zero if ai is None else ai.astype(BF16)

    def group_maps(k1, xr, xi):
        for gi in range(ct // G_C):
            gl = slice(gi * G_C, (gi + 1) * G_C)
            y = _dot(jnp.concatenate([xr[:, gl], xi[:, gl]], axis=1).astype(BF16), ab_ref[gi])
            o_ref[k1, :, gl] = y.astype(o_ref.dtype)

    def dft(k1):
        stage1(k1)
        rows = 4 * n2 if 0 < k1 < h else 2 * n2
        return _dot(m_ref[0:rows, :], a_scr[:, k1 * ct:(k1 + 1) * ct])

    x_next = dft(0)
    for k1 in range(h + 1):
        x = x_next
        if k1 < h:
            x_next = dft(k1 + 1)
        group_maps(k1, x[:n2], x[n2:2 * n2])
        if 0 < k1 < h:
            group_maps(n1 - k1, x[2 * n2:3 * n2], x[3 * n2:])


def _fnet_long(u, ab, n1, n2):
    bsz, seq, _ = u.shape
    h = n1 // 2
    ng = 2
    ct = ng * G_C
    c, s = _dft_cos_sin(n2)
    ce, se = _dft_cos_sin(n2, shift=1)
    m = _bf16_split(np.block([[c, s], [-s, c], [ce, -se], [-se, -ce]]))[0]
    out = pl.pallas_call(
        functools.partial(_fnet_long_kernel, n1=n1, n2=n2),
        grid=(bsz, N_GROUPS_C // ng),
        in_specs=[pl.BlockSpec((None, n1, n2, ct), lambda b, g: (b, 0, 0, g)),
                  pl.BlockSpec((2, h, n2, ct), lambda b, g: (0, 0, 0, 0)),
                  pl.BlockSpec((4 * n2, 2 * n2), lambda b, g: (0, 0)),
                  pl.BlockSpec((ng, 2 * G_C, G_C), lambda b, g: (g, 0, 0))],
        out_specs=pl.BlockSpec((None, n1, n2, ct), lambda b, g: (b, 0, 0, g)),
        out_shape=jax.ShapeDtypeStruct((bsz, n1, n2, D_C), BF16),
        scratch_shapes=[pltpu.VMEM((2, h - 1, n2, ct), F32),
                        pltpu.VMEM((2 * n2, (h + 1) * ct), BF16)],
        compiler_params=_params(("parallel", "parallel")),
        name="fnet_long",
    )(u.reshape(bsz, n1, n2, D_C), _twiddle1(n1, n2, ct), m, ab)
    return out.transpose(0, 2, 1, 3).reshape(bsz, seq, D_C)


def _fnet_short_kernel(u_ref, f_ref, ab_ref, o_ref, *, seq, scale):
    def body(trip, carry):
        sqs = (2 * trip, 2 * trip + 1)
        xs = [_dot(f_ref[...], u_ref[sq]) for sq in sqs]
        for sq, x in zip(sqs, xs):
            xr, xi = x[:seq], x[seq:]
            for g in range(N_GROUPS_C):
                sl = slice(g * G_C, (g + 1) * G_C)
                xin = jnp.concatenate([xr[:, sl], xi[:, sl]], axis=1).astype(BF16)
                o_ref[sq, :, sl] = (_dot(xin, ab_ref[g]) * scale).astype(o_ref.dtype)
        return carry

    lax.fori_loop(0, u_ref.shape[0] // 2, body, 0)


def _fnet_short(u, ab, scale, nseq):
    bsz, seq, _ = u.shape
    c, s = _dft_cos_sin(seq)
    f = _bf16_split(np.concatenate([c, -s], axis=0))[0]
    return pl.pallas_call(
        functools.partial(_fnet_short_kernel, seq=seq, scale=scale),
        grid=(bsz // nseq,),
        in_specs=[pl.BlockSpec((nseq, seq, D_C), lambda b: (b, 0, 0)),
                  pl.BlockSpec((2 * seq, seq), lambda b: (0, 0)),
                  pl.BlockSpec((N_GROUPS_C, 2 * G_C, G_C), lambda b: (0, 0, 0))],
        out_specs=pl.BlockSpec((nseq, seq, D_C), lambda b: (b, 0, 0)),
        out_shape=jax.ShapeDtypeStruct((bsz, seq, D_C), BF16),
        compiler_params=_params(("parallel",)),
        name="fnet_short",
    )(u, f, ab)


def _final_kernel(x1_ref, y_ref, g_ref, gt_ref, wo32_ref, fg_ref, o_ref, wo_ref):
    @pl.when(pl.program_id(0) == 0)
    def _():
        wo_ref[...] = wo32_ref[...].astype(BF16)

    z = (y_ref[...].astype(F32) * _silu(g_ref[...].astype(F32))).astype(BF16)
    x2 = x1_ref[...].astype(F32) + gt_ref[...] * _dot(z, wo_ref[...])
    o_ref[...] = _rms(x2) * fg_ref[...]


def _final(x1, y, g, gt, wo, fg, tiles_per_mod, tm):
    t = x1.shape[0]
    row = lambda i: (i, 0)
    fixed = lambda i: (0, 0)
    mod = lambda i: (i // tiles_per_mod, 0, 0)
    return pl.pallas_call(
        _final_kernel,
        grid=(t // tm,),
        in_specs=[pl.BlockSpec((tm, D_MODEL), row),
                  pl.BlockSpec((tm, D_C), row),
                  pl.BlockSpec((tm, D_C), row),
                  pl.BlockSpec((None, 1, D_MODEL), mod),
                  pl.BlockSpec((D_C, D_MODEL), fixed),
                  pl.BlockSpec((1, D_MODEL), fixed)],
        out_specs=pl.BlockSpec((tm, D_MODEL), row),
        out_shape=jax.ShapeDtypeStruct((t, D_MODEL), F32),
        scratch_shapes=[pltpu.VMEM((D_C, D_MODEL), BF16)],
        compiler_params=_params(("arbitrary",)),
        name="outproj_odd_final",
    )(x1, y, g, gt, wo, fg)


def _rope_swap(w):
    nf = ROPE_DIM // 4
    w4 = w.reshape(w.shape[:-1] + (2, 2, nf))
    return jnp.stack([-w4[..., 1, :], w4[..., 0, :]], axis=-2).reshape(w.shape)


def _pack_w_in_even(w):
    ua_ga_cq_ckv = w[:, :4 * D_A + Q_RANK + KV_RANK]
    o = 4 * D_A + Q_RANK + KV_RANK
    krope = w[:, o:o + ROPE_DIM]
    gb = w[:, o + ROPE_DIM:]
    zpad = jnp.zeros((w.shape[0], LANES - ROPE_DIM), w.dtype)
    return jnp.concatenate([ua_ga_cq_ckv, gb, krope, zpad, _rope_swap(krope), zpad],
                           axis=1).astype(BF16)


def _pack_w_uq(w):
    w3 = w.reshape(Q_RANK, N_HEADS, QK_DIM)
    zq = jnp.zeros((Q_RANK, N_HEADS, HEAD_PAD - QK_DIM), w.dtype)
    wa = jnp.concatenate([w3, zq], axis=-1)
    zn = jnp.zeros((Q_RANK, N_HEADS, NOPE_DIM), w.dtype)
    wb = jnp.concatenate([zn, _rope_swap(w3[..., NOPE_DIM:]), zq], axis=-1)
    shape = (Q_RANK, N_HEADS * HEAD_PAD)
    return wa.reshape(shape).astype(BF16), wb.reshape(shape).astype(BF16)


def _pack_w_ukv(w):
    w3 = w.reshape(KV_RANK, N_HEADS, NOPE_DIM + V_DIM)
    zk = jnp.zeros((KV_RANK, N_HEADS, HEAD_PAD - NOPE_DIM), w.dtype)
    wk_c = jnp.concatenate([w3[..., :NOPE_DIM], zk], axis=-1).reshape(KV_RANK, -1)
    eye = jnp.eye(LANES, ROPE_DIM, dtype=w.dtype)
    rope_rows = jnp.concatenate([jnp.zeros((LANES, NOPE_DIM), w.dtype), eye,
                                 jnp.zeros((LANES, HEAD_PAD - QK_DIM), w.dtype)], axis=-1)
    wk_r = jnp.tile(rope_rows, (1, N_HEADS))
    wk = jnp.concatenate([wk_c, wk_r], axis=0)
    zv = jnp.zeros((KV_RANK, N_HEADS, HEAD_PAD - V_DIM), w.dtype)
    wv = jnp.concatenate([w3[..., NOPE_DIM:], zv], axis=-1).reshape(KV_RANK, -1)
    ones = np.zeros((1, N_HEADS, HEAD_PAD), np.float32)
    ones[:, :, V_DIM] = 1.0
    return wk.astype(BF16), wv.astype(BF16), jnp.asarray(ones.reshape(1, -1))


def _rope_tables(seq, rotate):
    pos = np.arange(seq)
    half = ROPE_DIM // 2
    inv = ROPE_BASE ** (-np.arange(0, half, 2, dtype=np.float64) / half)
    r = (pos // GRID_W)[:, None] * inv
    c = (pos % GRID_W)[:, None] * inv
    ang = np.concatenate([r, r, c, c], axis=-1) * (1.0 if rotate else 0.0)
    cos_q = np.zeros((seq, LANES))
    sin_q = np.zeros((seq, LANES))
    cos_k = np.zeros((seq, LANES))
    sin_k = np.zeros((seq, LANES))
    cos_q[:, :NOPE_DIM] = 1.0
    cos_q[:, NOPE_DIM:QK_DIM] = np.cos(ang)
    sin_q[:, NOPE_DIM:QK_DIM] = np.sin(ang)
    cos_k[:, :ROPE_DIM] = np.cos(ang)
    sin_k[:, :ROPE_DIM] = np.sin(ang)
    return tuple(jnp.asarray(t.astype(np.float32)) for t in (cos_q, sin_q, cos_k, sin_k))


def kernel(x_prompt, x_sample, c, cache_ckv, cache_krope, c_ctx, norm_g, w_mod, b_mod,
           w_in_e, conv_w, conv_b, filt_w1, filt_b1, filt_w2, filt_b2, filt_w3, hyena_skip,
           q_norm_g, kv_norm_g, w_uq, w_ukv, w_out_e, w_in_o, fnet_w, w_out_o, final_g):
    nb_p, seq_p, _ = x_prompt.shape
    nb_s, seq_s, _ = x_sample.shape
    tp = nb_p * seq_p
    ts = nb_s * seq_s
    tm = ROW_TILE
    short_nseq = SHORT_SEQ_ROWS // seq_p

    cond = jnp.concatenate([c_ctx[None, :], c, jnp.zeros((8 - 1 - nb_s, D_MODEL), F32)], axis=0)
    mod = _ada_params(cond, w_mod, b_mod)
    shift = mod[:, :, None, :D_MODEL]
    scale = (1.0 + mod[:, :, None, D_MODEL:2 * D_MODEL]) * norm_g[:, None, None, :]
    gate = mod[:, :, None, 2 * D_MODEL:]

    xp = x_prompt.reshape(tp, D_MODEL)
    xs = x_sample.reshape(ts, D_MODEL)

    w_in = _pack_w_in_even(w_in_e[0])
    qg = q_norm_g[0].reshape(1, Q_RANK)
    kvg = kv_norm_g[0].reshape(1, KV_RANK)
    wq_a, wq_b = _pack_w_uq(w_uq[0])
    wk, wv, vone = _pack_w_ukv(w_ukv[0])
    cos_qp, sin_qp, cos_kp, sin_kp = _rope_tables(tm, rotate=False)
    cos_qs, sin_qs, cos_ks, sin_ks = _rope_tables(seq_s, rotate=True)
    pr_p = _inproj_even(xp, scale[0, :1], shift[0, :1], w_in, qg, kvg, cos_kp, sin_kp, wk, wv,
                        vone, tm, True)
    pr_s = _inproj_even(xs, scale[0, 1:1 + nb_s], shift[0, 1:1 + nb_s], w_in, qg, kvg,
                        cos_ks, sin_ks, wk, wv, vone, tm, False,
                        cache=(cache_ckv, cache_krope, 0))
    ua_p, ga_p, cq_p, gb_p, k_p, v_p, ckv_p, kra_p = pr_p
    ua_s, ga_s, cq_s, gb_s, k_s, v_s = pr_s

    filt = (filt_w1[0], filt_b1[0], filt_w2[0], filt_b2[0], filt_w3[0])
    ya = []
    for ua, ga, nb, seq in ((ua_p, ga_p, nb_p, seq_p), (ua_s, ga_s, nb_s, seq_s)):
        h, inv_norm = _hyena_filter(seq, *filt)
        kf = _spectrum(h, inv_norm, seq)
        ya.append(_hyena(ua.reshape(nb, seq, 3 * D_A), ga.reshape(nb, seq, D_A),
                         conv_w[0], conv_b[0], hyena_skip[0], kf))
    ya_p = ya[0].reshape(tp, D_A)
    ya_s = ya[1].reshape(ts, D_A)

    ob_p = _attention(cq_p.reshape(nb_p, seq_p, Q_RANK), cos_qp, sin_qp, wq_a, wq_b,
                      k_p.reshape(nb_p, seq_p, -1), v_p.reshape(nb_p, seq_p, -1),
                      gb_p.reshape(nb_p, seq_p, D_B), seq_p, nseq=short_nseq)
    ob_s = _attention(cq_s.reshape(nb_s, seq_s, Q_RANK), cos_qs, sin_qs, wq_a, wq_b, k_s, v_s,
                      gb_s.reshape(nb_s, seq_s, D_B), ATTN_Q_TILE)

    wo_e = w_out_e[0]
    wi_o = w_in_o[0]
    tw = WIDE_ROW_TILE
    x1_p, u_p, g_p = _mid(xp, ya_p, ob_p.reshape(tp, D_B), gate[0, :1], wo_e,
                          scale[1, :1], shift[1, :1], wi_o, tp // tw, tw)
    x1_s, u_s, g_s = _mid(xs, ya_s, ob_s.reshape(ts, D_B), gate[0, 1:1 + nb_s], wo_e,
                          scale[1, 1:1 + nb_s], shift[1, 1:1 + nb_s], wi_o, seq_s // tw, tw)

    ab = _fnet_weights(fnet_w[0], seq_s)
    y_p = _fnet_short(u_p.reshape(nb_p, seq_p, D_C), ab, math.sqrt(seq_s / seq_p), short_nseq)
    y_s = _fnet_long(u_s.reshape(nb_s, seq_s, D_C), ab, FNET_RADIX, seq_s // FNET_RADIX)
    wo_o = w_out_o[0]
    fg = final_g.reshape(1, D_MODEL)
    out_p = _final(x1_p, y_p.reshape(tp, D_C), g_p, gate[1, :1], wo_o, fg, tp // tw, tw)
    out_s = _final(x1_s, y_s.reshape(ts, D_C), g_s, gate[1, 1:1 + nb_s], wo_o, fg,
                   seq_s // tw, tw)

    state_ckv = ckv_p.reshape(nb_p, 1, seq_p, KV_RANK)
    state_krope = kra_p[:, :ROPE_DIM].reshape(nb_p, 1, seq_p, ROPE_DIM)
    return (out_p.reshape(nb_p, seq_p, D_MODEL), out_s.reshape(nb_s, seq_s, D_MODEL),
            state_ckv, state_krope)
```

```python
import functools
import math

import numpy as np
import jax
import jax.numpy as jnp
from jax import lax
from jax.experimental import pallas as pl
from jax.experimental.pallas import tpu as pltpu

F32 = jnp.float32
BF16 = jnp.bfloat16

D_MODEL = 1024
GRID_W = 64
EPS = 1e-6
D_A = 512
EMB_DIM = 33
DECAY_TARGET = 1e-2
MIN_DECAY = math.log(DECAY_TARGET) / 1.5
MAX_DECAY = math.log(DECAY_TARGET) / 0.3
N_HEADS = 8
Q_RANK = 256
KV_RANK = 256
NOPE_DIM = 64
ROPE_DIM = 32
V_DIM = 64
ROPE_BASE = 10000.0
D_B = N_HEADS * V_DIM
QK_DIM = NOPE_DIM + ROPE_DIM
ATTN_SCALE = 1.0 / math.sqrt(QK_DIM)
LOG2_E = math.log2(math.e)
D_C = 1024
N_GROUPS_C = 8
G_C = D_C // N_GROUPS_C

LANES = 128
MXU_LANES = 2 * LANES
HEAD_PAD = LANES
VMEM_LIMIT = 56 * 1024 * 1024

ROW_TILE = 512
WIDE_ROW_TILE = 2 * ROW_TILE
ATTN_Q_TILE = 512
SHORT_SEQ_ROWS = 1024
DFT_MAX_N2 = 512
FNET_RADIX = 16
FILTER_TILE = 1024

_C_UA, _C_GA, _C_CQ, _C_CKV, _C_GB, _C_KRA, _C_KRB, _C_END = (
    0, 1536, 2048, 2304, 2560, 3072, 3200, 3328)


def _params(sem):
    return pltpu.CompilerParams(dimension_semantics=sem, vmem_limit_bytes=VMEM_LIMIT)


def _rms(x):
    return x * lax.rsqrt(jnp.mean(x * x, axis=-1, keepdims=True) + EPS)


def _silu(g):
    return g * jax.nn.sigmoid(g)


def _dot(a, b):
    return jnp.dot(a, b, preferred_element_type=F32)


def _ada_kernel(cond_ref, w_ref, b_ref, o_ref):
    c = _silu(cond_ref[...])
    o_ref[...] = jnp.dot(c, w_ref[...], preferred_element_type=F32,
                         precision=lax.Precision.HIGHEST) + b_ref[...]


def _ada_params(cond, w_mod, b_mod):
    depth = w_mod.shape[0]
    n = cond.shape[0]
    tn = 1024
    return pl.pallas_call(
        _ada_kernel,
        grid=(depth, 3 * D_MODEL // tn),
        in_specs=[pl.BlockSpec((n, D_MODEL), lambda i, j: (0, 0)),
                  pl.BlockSpec((None, D_MODEL, tn), lambda i, j: (i, 0, j)),
                  pl.BlockSpec((None, 1, tn), lambda i, j: (i, 0, j))],
        out_specs=pl.BlockSpec((None, n, tn), lambda i, j: (i, 0, j)),
        out_shape=jax.ShapeDtypeStruct((depth, n, 3 * D_MODEL), F32),
        compiler_params=_params(("arbitrary", "arbitrary")),
        name="ada_params",
    )(cond, w_mod, b_mod.reshape(depth, 1, 3 * D_MODEL))


def _inproj_even_kernel(x_ref, sc_ref, sh_ref, w_ref, qg_ref, kvg_ref, cos_ref, sin_ref,
                        wk_ref, wv_ref, vone_ref, *refs, nctx, emit_state):
    if nctx:
        cckv_ref, ckr_ref = refs[:2]
        refs = refs[2:]
    ua_ref, ga_ref, cq_ref, gb_ref, k_ref, v_ref = refs[:6]
    state_refs = refs[6:]

    def tokens():
        h = (_rms(x_ref[...]) * sc_ref[...] + sh_ref[...]).astype(BF16)

        def proj(a, b):
            return _dot(h, w_ref[:, a:b])

        ua_ref[...] = proj(_C_UA, _C_GA)
        ga_ref[...] = proj(_C_GA, _C_CQ)
        cq_ref[...] = (_rms(proj(_C_CQ, _C_CKV)) * qg_ref[...]).astype(BF16)
        gb_ref[...] = proj(_C_GB, _C_KRA)
        ckv = _rms(proj(_C_CKV, _C_GB)) * kvg_ref[...]
        kr2 = proj(_C_KRA, _C_END)
        kra = kr2[:, :LANES]
        krb = kr2[:, LANES:]
        c = ckv.astype(BF16)
        kr = pltpu.roll(kra * cos_ref[...] + krb * sin_ref[...], NOPE_DIM, 1)
        k = _dot(c, wk_ref[0:KV_RANK, :]) + jnp.concatenate([kr] * N_HEADS, axis=-1)
        k_ref[...] = k.astype(BF16)
        v_ref[...] = (_dot(c, wv_ref[...]) + vone_ref[...]).astype(BF16)
        if emit_state:
            state_refs[0][...] = ckv
            state_refs[1][...] = kra

    if not nctx:
        tokens()
        return

    t = pl.program_id(1)

    @pl.when(t < nctx)
    def _():
        c = cckv_ref[...].astype(BF16)
        kr = ckr_ref[...].astype(BF16)
        k_ref[...] = (_dot(c, wk_ref[0:KV_RANK, :])
                      + _dot(kr, wk_ref[KV_RANK:KV_RANK + ROPE_DIM, :])).astype(BF16)
        v_ref[...] = (_dot(c, wv_ref[...]) + vone_ref[...]).astype(BF16)

    pl.when(t >= nctx)(tokens)


def _inproj_even(x2d, sc, sh, w, qg, kvg, cos_k, sin_k, wk, wv, vone, tm, emit_state,
                 cache=None):
    t = x2d.shape[0]
    nb = sc.shape[0]
    tiles = t // nb // tm
    tiles_per_seq = cos_k.shape[0] // tm
    nctx = 0 if cache is None else cache[0].shape[2] // tm
    tok = lambda b, s: b * tiles + jnp.maximum(s - nctx, 0)
    row = lambda b, s: (tok(b, s), 0)
    fixed = lambda b, s: (0, 0)
    mod = lambda b, s: (b, 0, 0)
    pos = lambda b, s: (jnp.maximum(s - nctx, 0) % tiles_per_seq, 0)
    kvo = lambda b, s: (b, s, 0)
    widths = (3 * D_A, D_A, Q_RANK, D_B)
    dtypes = (F32, F32, BF16, F32)
    in_specs = [pl.BlockSpec((tm, D_MODEL), row),
                pl.BlockSpec((None, 1, D_MODEL), mod),
                pl.BlockSpec((None, 1, D_MODEL), mod),
                pl.BlockSpec((D_MODEL, _C_END), fixed),
                pl.BlockSpec((1, Q_RANK), fixed),
                pl.BlockSpec((1, KV_RANK), fixed),
                pl.BlockSpec((tm, LANES), pos),
                pl.BlockSpec((tm, LANES), pos),
                pl.BlockSpec((KV_RANK + LANES, N_HEADS * HEAD_PAD), fixed),
                pl.BlockSpec((KV_RANK, N_HEADS * HEAD_PAD), fixed),
                pl.BlockSpec((1, N_HEADS * HEAD_PAD), fixed)]
    args = [x2d, sc, sh, w, qg, kvg, cos_k, sin_k, wk, wv, vone]
    if nctx:
        layer = cache[2]
        ctx = lambda b, s: (b, layer, jnp.minimum(s, nctx - 1), 0)
        in_specs += [pl.BlockSpec((None, None, tm, KV_RANK), ctx),
                     pl.BlockSpec((None, None, tm, ROPE_DIM), ctx)]
        args += [cache[0], cache[1]]
    out_specs = [pl.BlockSpec((tm, wd), row) for wd in widths]
    out_shape = [jax.ShapeDtypeStruct((t, wd), dt) for wd, dt in zip(widths, dtypes)]
    for wd in (N_HEADS * HEAD_PAD, N_HEADS * HEAD_PAD):
        out_specs.append(pl.BlockSpec((None, tm, wd), kvo))
        out_shape.append(jax.ShapeDtypeStruct((nb, (nctx + tiles) * tm, wd), BF16))
    if emit_state:
        for wd in (KV_RANK, LANES):
            out_specs.append(pl.BlockSpec((tm, wd), row))
            out_shape.append(jax.ShapeDtypeStruct((t, wd), F32))
    return pl.pallas_call(
        functools.partial(_inproj_even_kernel, nctx=nctx, emit_state=emit_state),
        grid=(nb, nctx + tiles),
        in_specs=in_specs,
        out_specs=out_specs,
        out_shape=out_shape,
        compiler_params=_params(("parallel", "arbitrary")),
        name="inproj_even",
    )(*args)


def _acc(acc, coef, x):
    if x is None or abs(coef) < 1e-12:
        return acc
    if abs(abs(coef) - 1.0) < 1e-12:
        if acc is None:
            return x if coef > 0 else -x
        return acc + x if coef > 0 else acc - x
    return coef * x if acc is None else acc + coef * x


def _add(a, b):
    if a is None:
        return b
    return a if b is None else a + b


def _sub(a, b):
    if b is None:
        return a
    return -b if a is None else a - b


def _cmul(ar, ai, br, bi):
    def mul(x, y):
        return None if x is None or y is None else x * y
    return _sub(mul(ar, br), mul(ai, bi)), _add(mul(ar, bi), mul(ai, br))


def _radix_parts(h):
    singles = [0] + ([h // 2] if h >= 2 and h % 2 == 0 else [])
    pairs = [(n, h - n) for n in range(1, (h + 1) // 2)]
    return singles, pairs


def _radix_fwd(k1, h, single, pair_sum, pair_diff):
    singles, pairs = _radix_parts(h)
    ar = ai = None
    for n in singles:
        th = 2.0 * math.pi * n * k1 / (2 * h)
        x = single(n)
        ar = _acc(ar, math.cos(th), x)
        ai = _acc(ai, -math.sin(th), x)
    for i, (n, _) in enumerate(pairs):
        th = 2.0 * math.pi * n * k1 / (2 * h)
        re_src, im_src = (pair_sum, pair_diff) if k1 % 2 == 0 else (pair_diff, pair_sum)
        ar = _acc(ar, math.cos(th), re_src(i))
        ai = _acc(ai, -math.sin(th), im_src(i))
    return ar, ai


def _radix_inv(h, n_total, bpr, bpi, emit):
    singles, pairs = _radix_parts(h)

    def coefs(n, k1):
        th = 2.0 * math.pi * n * k1 / (2 * h)
        w = (1.0 if k1 in (0, h) else 2.0) / n_total
        return w * math.cos(th), w * math.sin(th)

    for n in singles:
        y = None
        for k1 in range(h + 1):
            c, s = coefs(n, k1)
            y = _acc(y, c, bpr(k1) if abs(c) > 1e-12 else None)
            y = _acc(y, -s, bpi(k1) if abs(s) > 1e-12 else None)
        emit(n, y)
    for n, m in pairs:
        acc = {(0, "c"): None, (0, "s"): None, (1, "c"): None, (1, "s"): None}
        for k1 in range(h + 1):
            c, s = coefs(n, k1)
            par = k1 % 2
            acc[(par, "c")] = _acc(acc[(par, "c")], c, bpr(k1) if abs(c) > 1e-12 else None)
            acc[(par, "s")] = _acc(acc[(par, "s")], s, bpi(k1) if abs(s) > 1e-12 else None)
        even_m, even_p = _sub(acc[(0, "c")], acc[(0, "s")]), _add(acc[(0, "c")], acc[(0, "s")])
        odd_m, odd_p = _sub(acc[(1, "c")], acc[(1, "s")]), _add(acc[(1, "c")], acc[(1, "s")])
        emit(n, _add(even_m, odd_m))
        emit(m, _sub(even_p, odd_p))


def _next_twiddle(tw, tw_ref):
    row = 0 if tw is None else tw[2] + 1
    return tw_ref[0, row], tw_ref[1, row], row


def _spectrum_kernel(hf_ref, hb_ref, inv_ref, tw1_ref, fh_ref, kf_ref, sd_scr,
                     *, h, n2, nk_pad):
    ct = hf_ref.shape[-1]
    row = lax.broadcasted_iota(jnp.int32, (n2, ct), 0)
    _, pairs = _radix_parts(h)

    def hb(n):
        return jnp.where(row == 0, 0.0, hb_ref[0]) if n == 0 else hb_ref[n]

    for i, (n, m) in enumerate(pairs):
        sd_scr[0, 0, i] = hf_ref[n] + hf_ref[m]
        sd_scr[0, 1, i] = hf_ref[n] - hf_ref[m]
        sd_scr[1, 0, i] = hb(n) + hb(m)
        sd_scr[1, 1, i] = hb(n) - hb(m)

    inv = inv_ref[...]
    tw = None
    for k1 in range(h + 1):
        far, fai = _radix_fwd(k1, h, lambda n: hf_ref[n],
                              lambda i: sd_scr[0, 0, i], lambda i: sd_scr[0, 1, i])
        bar, bai = _radix_fwd(k1, h, hb, lambda i: sd_scr[1, 0, i], lambda i: sd_scr[1, 1, i])
        if k1 > 0:
            tw = _next_twiddle(tw, tw1_ref)
            far, fai = _cmul(far, fai, tw[0], tw[1])
            bar, bai = _cmul(bar, bai, tw[0], tw[1])
        zero = jnp.zeros((n2, ct), F32)
        af = jnp.concatenate([zero if far is None else far, zero if fai is None else fai], axis=0)
        ab = jnp.concatenate([zero if bar is None else bar, zero if bai is None else bai], axis=0)
        a = jnp.concatenate([af, ab], axis=1)
        x = _dot(fh_ref[...], a.astype(BF16))
        lanes = slice(k1 * ct, (k1 + 1) * ct)
        kf_ref[0, :, lanes] = (x[:n2, :ct] + x[:n2, ct:]) * inv
        kf_ref[1, :, lanes] = (x[n2:, :ct] - x[n2:, ct:]) * inv
    if nk_pad > h + 1:
        pad = slice((h + 1) * ct, nk_pad * ct)
        kf_ref[:, :, pad] = jnp.zeros((2, n2, (nk_pad - h - 1) * ct), F32)


def _hyena_kernel(x0_ref, x1_ref, v_ref, ga_ref, *refs, **plan):
    o_ref = refs[10]
    scratch = refs[11:]
    par = scratch[0].shape[0]

    def body(trip, carry):
        for slot in range(par):
            sq = trip * par + slot
            _hyena_sequence(x0_ref.at[sq], x1_ref.at[sq], v_ref.at[sq], ga_ref.at[sq],
                            *refs[:10], o_ref.at[sq], *[s.at[slot] for s in scratch], **plan)
        return carry

    trips = o_ref.shape[0] // par
    if trips == 1:
        body(0, 0)
    else:
        lax.fori_loop(0, trips, body, 0)


def _hyena_sequence(x0_ref, x1_ref, v_ref, ga_ref,
                    w0_ref, w1_ref, wv_ref, b0_ref, b1_ref, bv_ref, skip_ref,
                    kf_ref, tw1_ref, f_ref,
                    o_ref, v_scr, sd_scr, a_scr, b_scr, *, seq, nslab, n2, nk_pad):
    ct = o_ref.shape[-1]
    h = nslab
    chunk = MXU_LANES
    kpc = chunk // ct
    row = lax.broadcasted_iota(jnp.int32, (n2, ct), 0)

    def short_conv(ref, w_ref, b_ref, s, start):
        u = ref[pl.ds(start, n2), :]
        if isinstance(s, int):
            zero_row = jnp.zeros((1, ct), F32)
            prev = zero_row if s == 0 else ref[pl.ds(start - 1, 1), :]
            nxt = zero_row if s == nslab - 1 else ref[pl.ds(start + n2, 1), :]
        else:
            prev = ref[pl.ds(jnp.maximum(start - 1, 0), 1), :]
            prev = jnp.where(s == 0, 0.0, prev)
            nxt = ref[pl.ds(jnp.minimum(start + n2, seq - 1), 1), :]
            nxt = jnp.where(s == nslab - 1, 0.0, nxt)
        up = jnp.where(row == 0, prev, pltpu.roll(u, 1, 0))
        un = jnp.where(row == n2 - 1, nxt, pltpu.roll(u, n2 - 1, 0))
        return b_ref[...] + up * w_ref[0:1, :] + u * w_ref[1:2, :] + un * w_ref[2:3, :]

    def interior_conv(ref, w_ref, b_ref, s, start):
        up = ref[pl.ds(start - 1, n2), :]
        u = ref[pl.ds(start, n2), :]
        un = ref[pl.ds(start + 1, n2), :]
        return b_ref[...] + up * w_ref[0:1, :] + u * w_ref[1:2, :] + un * w_ref[2:3, :]

    def prep_slab(s, start, conv):
        x0 = conv(x0_ref, w0_ref, b0_ref, s, start)
        x1 = conv(x1_ref, w1_ref, b1_ref, s, start)
        v_scr[s] = conv(v_ref, wv_ref, bv_ref, s, start) * x1
        o_ref[pl.ds(start, n2), :] = x0 * _silu(ga_ref[pl.ds(start, n2), :])

    def prep(s, carry):
        prep_slab(s, s * n2, interior_conv)
        return carry

    for s in sorted({0, nslab - 1}):
        prep_slab(s, s * n2, short_conv)
    if nslab > 2:
        lax.fori_loop(1, nslab - 1, prep, 0)

    _, pairs = _radix_parts(h)
    for i, (n, m) in enumerate(pairs):
        sd_scr[0, i] = v_scr[n] + v_scr[m]
        sd_scr[1, i] = v_scr[n] - v_scr[m]
    zero = jnp.zeros((n2, ct), BF16)

    def stage1(k1):
        lanes = slice(k1 * ct, (k1 + 1) * ct)
        ar = ai = None
        if k1 <= h:
            ar, ai = _radix_fwd(k1, h, lambda n: v_scr[n], lambda i: sd_scr[0, i],
                                lambda i: sd_scr[1, i])
        if 0 < k1 <= h:
            ar, ai = _cmul(ar, ai, tw1_ref[0, k1 - 1], tw1_ref[1, k1 - 1])
        a_scr[0:n2, lanes] = zero if ar is None else ar.astype(BF16)
        a_scr[n2:2 * n2, lanes] = zero if ai is None else ai.astype(BF16)

    def dft(j):
        for k1 in range(j * kpc, (j + 1) * kpc):
            stage1(k1)
        return _dot(f_ref[...], a_scr[:, j * chunk:(j + 1) * chunk])

    tw = None
    nchunk = nk_pad // kpc
    x_next = dft(0)
    for j in range(nchunk):
        cols = slice(j * chunk, (j + 1) * chunk)
        x = x_next
        if j + 1 < nchunk:
            x_next = dft(j + 1)
        xr, xi = x[:n2], x[n2:]
        kr = kf_ref[0, :, cols]
        ki = kf_ref[1, :, cols]
        z = jnp.concatenate([xr * kr - xi * ki, -(xr * ki + xi * kr)], axis=0)
        bt = _dot(f_ref[...], z.astype(BF16))
        for kk in range(kpc):
            k1 = j * kpc + kk
            if k1 > h:
                continue
            lanes = slice(kk * ct, (kk + 1) * ct)
            br = bt[:n2, lanes]
            bi = -bt[n2:, lanes]
            if k1 > 0:
                tw = _next_twiddle(tw, tw1_ref)
                br, bi = _cmul(br, bi, tw[0], -tw[1])
            b_scr[0, k1] = br
            if k1 not in (0, h):
                b_scr[1, k1] = bi

    def emit(n, y):
        rows = pl.ds(n * n2, n2)
        o_ref[rows, :] = (y + v_scr[n] * skip_ref[...]) * o_ref[rows, :]

    _radix_inv(h, 2 * seq, lambda k1: b_scr[0, k1], lambda k1: b_scr[1, k1], emit)


def _dft_cos_sin(n2, shift=0):
    idx = np.arange(n2)
    ang = 2.0 * np.pi * (((idx[:, None] + shift) * idx[None, :]) % n2) / n2
    return np.cos(ang), np.sin(ang)


def _bf16_split(x):
    x32 = jnp.asarray(x.astype(np.float32))
    hi = x32.astype(BF16)
    lo = (x32 - hi.astype(F32)).astype(BF16)
    return hi, lo


def _dft_block(n2):
    c, s = _dft_cos_sin(n2)
    return np.block([[c, s], [-s, c]])


def _hyena_plan(seq):
    n2 = min(seq, DFT_MAX_N2)
    n1 = 2 * seq // n2
    ct = LANES if seq > DFT_MAX_N2 else MXU_LANES
    kpc = MXU_LANES // ct
    nk_pad = -(-(n1 // 2 + 1) // kpc) * kpc
    return n1, n2, ct, nk_pad


def _twiddle1(n1, n2, ct):
    ang = 2.0 * np.pi * np.arange(1, n1 // 2 + 1)[:, None] * np.arange(n2)[None, :] / (n1 * n2)
    tw = np.stack([np.cos(ang), -np.sin(ang)]).astype(np.float32)
    return jnp.broadcast_to(jnp.asarray(tw)[..., None], tw.shape + (ct,))


def _spectrum(h, inv_norm, seq):
    n1, n2, ct, nk_pad = _hyena_plan(seq)
    nslab = n1 // 2
    nct = D_A // ct
    npair = len(_radix_parts(nslab)[1])
    ffwd = _bf16_split(_dft_block(n2))[0]
    h3 = h.reshape(nslab, n2, 2 * D_A)
    fixed2 = lambda c: (0, 0)
    return pl.pallas_call(
        functools.partial(_spectrum_kernel, h=nslab, n2=n2, nk_pad=nk_pad),
        grid=(nct,),
        in_specs=[pl.BlockSpec((nslab, n2, ct), lambda c: (0, 0, c)),
                  pl.BlockSpec((nslab, n2, ct), lambda c: (0, 0, nct + c)),
                  pl.BlockSpec((1, ct), lambda c: (0, c)),
                  pl.BlockSpec((2, nslab, n2, ct), lambda c: (0, 0, 0, 0)),
                  pl.BlockSpec((2 * n2, 2 * n2), fixed2)],
        out_specs=pl.BlockSpec((None, 2, n2, nk_pad * ct), lambda c: (c, 0, 0, 0)),
        out_shape=jax.ShapeDtypeStruct((nct, 2, n2, nk_pad * ct), F32),
        scratch_shapes=[pltpu.VMEM((2, 2, max(npair, 1), n2, ct), F32)],
        compiler_params=_params(("parallel",)),
        name="filter_spectrum",
    )(h3, h3, inv_norm, _twiddle1(n1, n2, ct), ffwd)


def _hyena(ua, ga, conv_w, conv_b, skip, kf):
    bsz, seq, _ = ua.shape
    n1, n2, ct, nk_pad = _hyena_plan(seq)
    nslab = n1 // 2
    nct = D_A // ct
    npair = len(_radix_parts(nslab)[1])
    cb = conv_b.reshape(1, 3 * D_A)
    sk = skip.reshape(1, D_A)
    nseq = max(1, SHORT_SEQ_ROWS // seq)
    par = min(nseq, 4)
    part = lambda p: pl.BlockSpec((nseq, seq, ct), lambda c, b: (b, 0, p * nct + c))
    wpart = lambda p: pl.BlockSpec((3, ct), lambda c, b: (0, p * nct + c))
    bpart = lambda p: pl.BlockSpec((1, ct), lambda c, b: (0, p * nct + c))
    return pl.pallas_call(
        functools.partial(_hyena_kernel, seq=seq, nslab=nslab, n2=n2, nk_pad=nk_pad),
        grid=(nct, bsz // nseq),
        in_specs=[part(0), part(1), part(2),
                  pl.BlockSpec((nseq, seq, ct), lambda c, b: (b, 0, c)),
                  wpart(0), wpart(1), wpart(2), bpart(0), bpart(1), bpart(2),
                  pl.BlockSpec((1, ct), lambda c, b: (0, c)),
                  pl.BlockSpec((None, 2, n2, nk_pad * ct), lambda c, b: (c, 0, 0, 0)),
                  pl.BlockSpec((2, nslab, n2, ct), lambda c, b: (0, 0, 0, 0)),
                  pl.BlockSpec((2 * n2, 2 * n2), lambda c, b: (0, 0))],
        out_specs=pl.BlockSpec((nseq, seq, ct), lambda c, b: (b, 0, c)),
        out_shape=jax.ShapeDtypeStruct((bsz, seq, D_A), F32),
        scratch_shapes=[pltpu.VMEM((par, nslab, n2, ct), F32),
                        pltpu.VMEM((par, 2, max(npair, 1), n2, ct), F32),
                        pltpu.VMEM((par, 2 * n2, nk_pad * ct), BF16),
                        pltpu.VMEM((par, 2, nk_pad, n2, ct), F32)],
        compiler_params=_params(("parallel", "parallel")),
        name="hyena",
    )(ua, ua, ua, ga, conv_w, conv_w, conv_w, cb, cb, cb, sk, kf, _twiddle1(n1, n2, ct),
      _bf16_split(_dft_block(n2))[0])


def _hyena_filter_kernel(z_ref, w1_ref, b1_ref, w2_ref, b2_ref, w3_ref, del_ref,
                         o_ref, nrm_ref, *, seq):
    hp = lax.Precision.HIGHEST
    tm = o_ref.shape[0]
    half = tm // 2
    wd = o_ref.shape[1]
    i = pl.program_id(0)
    h = jnp.sin(jnp.dot(z_ref[...], w1_ref[...], precision=hp,
                        preferred_element_type=F32) + b1_ref[...])
    h = jnp.sin(jnp.dot(h, w2_ref[...], precision=hp, preferred_element_type=F32) + b2_ref[...])
    h = jnp.dot(h, w3_ref[...], precision=hp, preferred_element_type=F32)

    @pl.when(i == 0)
    def _():
        nrm_ref[...] = jnp.zeros_like(nrm_ref)

    for part in range(2):
        row0 = i * tm + part * half
        tcol = (row0 + lax.broadcasted_iota(jnp.int32, (half, 1), 0)).astype(F32) * (
            1.0 / (seq - 1))
        hp_part = h[:, part * wd:(part + 1) * wd] * jnp.exp(-tcol * del_ref[...])
        o_ref[part * half:(part + 1) * half, :] = hp_part
        nrm_ref[...] += jnp.sum(jnp.abs(hp_part), axis=0, keepdims=True)


def _hyena_filter(seq, w1, b1, w2, b2, w3):
    bands = (EMB_DIM - 1) // 2
    ang = (2.0 * np.pi * np.arange(seq)[:, None] / seq) * np.linspace(1e-4, bands - 1, bands)
    z = np.zeros((seq, LANES), np.float32)
    z[:, 0] = np.linspace(0.0, 1.0, seq)
    z[:, 1:1 + bands] = np.cos(ang)
    z[:, 1 + bands:EMB_DIM] = -np.sin(ang)
    deltas = np.abs(np.linspace(MIN_DECAY, MAX_DECAY, D_A)).astype(np.float32)
    deltas2 = np.concatenate([deltas, deltas])[None, :]
    tm = min(seq, FILTER_TILE)
    half = tm // 2
    fo = w1.shape[1]
    zp = jnp.asarray(z.reshape(seq // tm, 2, half, LANES).transpose(0, 2, 1, 3)
                     .reshape(seq // 2, 2 * LANES))
    eye2 = jnp.eye(2, dtype=F32)
    w1p = jnp.kron(eye2, jnp.pad(w1, ((0, LANES - EMB_DIM), (0, 0))))
    w2p = jnp.kron(eye2, w2)
    w3p = jnp.kron(eye2, w3)
    b1p = jnp.tile(b1.reshape(1, fo), (1, 2))
    b2p = jnp.tile(b2.reshape(1, fo), (1, 2))
    fixed = lambda i: (0, 0)
    h, nrm = pl.pallas_call(
        functools.partial(_hyena_filter_kernel, seq=seq),
        grid=(seq // tm,),
        in_specs=[pl.BlockSpec((half, 2 * LANES), lambda i: (i, 0)),
                  pl.BlockSpec((2 * LANES, 2 * fo), fixed),
                  pl.BlockSpec((1, 2 * fo), fixed),
                  pl.BlockSpec((2 * fo, 2 * fo), fixed),
                  pl.BlockSpec((1, 2 * fo), fixed),
                  pl.BlockSpec((2 * fo, 4 * D_A), fixed),
                  pl.BlockSpec((1, 2 * D_A), fixed)],
        out_specs=[pl.BlockSpec((tm, 2 * D_A), lambda i: (i, 0)),
                   pl.BlockSpec((1, 2 * D_A), fixed)],
        out_shape=[jax.ShapeDtypeStruct((seq, 2 * D_A), F32),
                   jax.ShapeDtypeStruct((1, 2 * D_A), F32)],
        compiler_params=_params(("arbitrary",)),
        name="hyena_filter",
    )(zp, w1p, b1p, w2p, b2p, w3p, jnp.asarray(deltas2))
    return h, 1.0 / (nrm[:, :D_A] + nrm[:, D_A:])


def _attn_kernel(cq_ref, cos_ref, sin_ref, wa_ref, wb_ref, k_ref, v_ref, gb_ref, o_ref, *, nseq):
    nt = (((1,), (1,)), ((), ()))
    tq = cq_ref.shape[1]
    cos = jnp.concatenate([cos_ref[...]] * N_HEADS, axis=-1)
    sin = jnp.concatenate([sin_ref[...]] * N_HEADS, axis=-1)
    lane = lax.broadcasted_iota(jnp.int32, (tq, 2 * V_DIM), 1)

    def one_sequence(sq):
        cq = cq_ref[sq]
        q = (_dot(cq, wa_ref[...]) * cos + _dot(cq, wb_ref[...]) * sin) * (ATTN_SCALE * LOG2_E)
        q = q.astype(BF16)

        def scores(h):
            hsl = slice(h * HEAD_PAD, (h + 1) * HEAD_PAD)
            return lax.dot_general(q[:, hsl], k_ref[sq, :, hsl], nt, preferred_element_type=F32)

        look = 1 if k_ref.shape[1] > 2 * tq else N_HEADS - 1
        pending = [scores(h) for h in range(look)]
        outs = []
        for h in range(N_HEADS):
            if h + look < N_HEADS:
                pending.append(scores(h + look))
            sb = pending.pop(0).astype(BF16)
            p = jnp.exp2(sb - jnp.max(sb, axis=-1, keepdims=True))
            of = _dot(p, v_ref[sq, :, h * HEAD_PAD:(h + 1) * HEAD_PAD])
            outs.append(of / of[:, V_DIM:V_DIM + 1])
            if h % 2 == 1:
                vsl = slice((h // 2) * 2 * V_DIM, (h // 2 + 1) * 2 * V_DIM)
                o = jnp.where(lane < V_DIM, outs[h - 1], pltpu.roll(outs[h], V_DIM, 1))
                o_ref[sq, :, vsl] = (o * _silu(gb_ref[sq, :, vsl])).astype(BF16)

    if nseq == 1:
        one_sequence(0)
    else:
        def body(trip, carry):
            one_sequence(2 * trip)
            one_sequence(2 * trip + 1)
            return carry

        lax.fori_loop(0, nseq // 2, body, 0)


def _attention(cq, cos, sin, wa, wb, k, v, gb, tq, nseq=1):
    bsz, lq, _ = cq.shape
    lk = k.shape[1]
    fixed = lambda b, i: (0, 0)
    qrow = lambda b, i: (b, i, 0)
    kv = lambda b, i: (b, 0, 0)
    pos = lambda b, i: (i, 0)
    return pl.pallas_call(
        functools.partial(_attn_kernel, nseq=nseq),
        grid=(bsz // nseq, lq // tq),
        in_specs=[pl.BlockSpec((nseq, tq, Q_RANK), qrow),
                  pl.BlockSpec((tq, LANES), pos),
                  pl.BlockSpec((tq, LANES), pos),
                  pl.BlockSpec((Q_RANK, N_HEADS * HEAD_PAD), fixed),
                  pl.BlockSpec((Q_RANK, N_HEADS * HEAD_PAD), fixed),
                  pl.BlockSpec((nseq, lk, N_HEADS * HEAD_PAD), kv),
                  pl.BlockSpec((nseq, lk, N_HEADS * HEAD_PAD), kv),
                  pl.BlockSpec((nseq, tq, D_B), qrow)],
        out_specs=pl.BlockSpec((nseq, tq, D_B), qrow),
        out_shape=jax.ShapeDtypeStruct((bsz, lq, D_B), BF16),
        compiler_params=_params(("parallel", "arbitrary")),
        name="mla_attention",
    )(cq, cos, sin, wa, wb, k, v, gb)


def _mid_kernel(x_ref, ya_ref, ob_ref, gt_ref, wo32_ref, sc_ref, sh_ref, wi32_ref,
                x1_ref, u_ref, g_ref, wo_ref, wi_ref):
    @pl.when(pl.program_id(0) == 0)
    def _():
        wo_ref[...] = wo32_ref[...].astype(BF16)
        wi_ref[...] = wi32_ref[...].astype(BF16)

    y = _dot(ya_ref[...].astype(BF16), wo_ref[0:D_A, :]) + _dot(ob_ref[...], wo_ref[D_A:, :])
    x1 = x_ref[...] + gt_ref[...] * y
    x1_ref[...] = x1.astype(x1_ref.dtype)
    h = (_rms(x1) * sc_ref[...] + sh_ref[...]).astype(BF16)
    u_ref[...] = _dot(h, wi_ref[:, :D_C]).astype(BF16)
    g_ref[...] = _dot(h, wi_ref[:, D_C:]).astype(BF16)


def _mid(x2d, ya, ob, gt, wo, sc, sh, wi, tiles_per_mod, tm):
    t = x2d.shape[0]
    row = lambda i: (i, 0)
    fixed = lambda i: (0, 0)
    mod = lambda i: (i // tiles_per_mod, 0, 0)
    return pl.pallas_call(
        _mid_kernel,
        grid=(t // tm,),
        in_specs=[pl.BlockSpec((tm, D_MODEL), row),
                  pl.BlockSpec((tm, D_A), row),
                  pl.BlockSpec((tm, D_B), row),
                  pl.BlockSpec((None, 1, D_MODEL), mod),
                  pl.BlockSpec((D_A + D_B, D_MODEL), fixed),
                  pl.BlockSpec((None, 1, D_MODEL), mod),
                  pl.BlockSpec((None, 1, D_MODEL), mod),
                  pl.BlockSpec((D_MODEL, 2 * D_C), fixed)],
        out_specs=[pl.BlockSpec((tm, D_MODEL), row),
                   pl.BlockSpec((tm, D_C), row),
                   pl.BlockSpec((tm, D_C), row)],
        out_shape=[jax.ShapeDtypeStruct((t, D_MODEL), BF16),
                   jax.ShapeDtypeStruct((t, D_C), BF16),
                   jax.ShapeDtypeStruct((t, D_C), BF16)],
        scratch_shapes=[pltpu.VMEM((D_A + D_B, D_MODEL), BF16),
                        pltpu.VMEM((D_MODEL, 2 * D_C), BF16)],
        compiler_params=_params(("arbitrary",)),
        name="outproj_even_inproj_odd",
    )(x2d, ya, ob, gt, wo, sc, sh, wi)


def _fnet_weights_kernel(w_ref, cs_ref, o_ref):
    hp = lax.Precision.HIGHEST
    o_ref[...] = jnp.dot(cs_ref[...], w_ref[...], precision=hp,
                         preferred_element_type=F32).astype(BF16)


def _fnet_weights(fnet_w, seq):
    c, s = _dft_cos_sin(G_C)
    cs = jnp.asarray((np.concatenate([c, s], axis=0) / math.sqrt(seq * G_C)).astype(np.float32))
    return pl.pallas_call(
        _fnet_weights_kernel,
        grid=(N_GROUPS_C,),
        in_specs=[pl.BlockSpec((None, G_C, G_C), lambda g: (g, 0, 0)),
                  pl.BlockSpec((2 * G_C, G_C), lambda g: (0, 0))],
        out_specs=pl.BlockSpec((None, 2 * G_C, G_C), lambda g: (g, 0, 0)),
        out_shape=jax.ShapeDtypeStruct((N_GROUPS_C, 2 * G_C, G_C), BF16),
        compiler_params=_params(("parallel",)),
        name="fnet_weights",
    )(fnet_w, cs)


def _fnet_long_kernel(u_ref, tw1_ref, m_ref, ab_ref, o_ref, sd_scr, a_scr, *, n1, n2):
    h = n1 // 2
    ct = o_ref.shape[-1]
    for i, n in enumerate(range(1, h)):
        a = u_ref[n].astype(F32)
        b = u_ref[n1 - n].astype(F32)
        sd_scr[0, i] = a + b
        sd_scr[1, i] = a - b
    x0 = u_ref[0].astype(F32)
    xh = u_ref[h].astype(F32)
    zero = jnp.zeros((n2, ct), BF16)

    def stage1(k1):
        ar = x0 + xh if k1 % 2 == 0 else x0 - xh
        ai = None
        for i, n in enumerate(range(1, h)):
            th = 2.0 * math.pi * n * k1 / n1
            ar = _acc(ar, math.cos(th), sd_scr[0, i])
            ai = _acc(ai, -math.sin(th), sd_scr[1, i])
        if k1 > 0:
            ar, ai = _cmul(ar, ai, tw1_ref[0, k1 - 1], tw1_ref[1, k1 - 1])
        lanes = slice(k1 * ct, (k1 + 1) * ct)
        a_scr[0:n2, lanes] = ar.astype(BF16)
        a_scr[n2:2 * n2, lanes] = zero if ai is None else ai.astype(BF16)

    def group_maps(k1, xr, xi):
        for gi in range(ct // G_C):
            gl = slice(gi * G_C, (gi + 1) * G_C)
            y = _dot(jnp.concatenate([xr[:, gl], xi[:, gl]], axis=1).astype(BF16), ab_ref[gi])
            o_ref[k1, :, gl] = y.astype(o_ref.dtype)

    def dft(k1):
        stage1(k1)
        rows = 4 * n2 if 0 < k1 < h else 2 * n2
        return _dot(m_ref[0:rows, :], a_scr[:, k1 * ct:(k1 + 1) * ct])

    x_next = dft(0)
    for k1 in range(h + 1):
        x = x_next
        if k1 < h:
            x_next = dft(k1 + 1)
        group_maps(k1, x[:n2], x[n2:2 * n2])
        if 0 < k1 < h:
            group_maps(n1 - k1, x[2 * n2:3 * n2], x[3 * n2:])


def _fnet_long(u, ab, n1, n2):
    bsz, seq, _ = u.shape
    h = n1 // 2
    ng = 2
    ct = ng * G_C
    c, s = _dft_cos_sin(n2)
    ce, se = _dft_cos_sin(n2, shift=1)
    m = _bf16_split(np.block([[c, s], [-s, c], [ce, -se], [-se, -ce]]))[0]
    out = pl.pallas_call(
        functools.partial(_fnet_long_kernel, n1=n1, n2=n2),
        grid=(bsz, N_GROUPS_C // ng),
        in_specs=[pl.BlockSpec((None, n1, n2, ct), lambda b, g: (b, 0, 0, g)),
                  pl.BlockSpec((2, h, n2, ct), lambda b, g: (0, 0, 0, 0)),
                  pl.BlockSpec((4 * n2, 2 * n2), lambda b, g: (0, 0)),
                  pl.BlockSpec((ng, 2 * G_C, G_C), lambda b, g: (g, 0, 0))],
        out_specs=pl.BlockSpec((None, n1, n2, ct), lambda b, g: (b, 0, 0, g)),
        out_shape=jax.ShapeDtypeStruct((bsz, n1, n2, D_C), BF16),
        scratch_shapes=[pltpu.VMEM((2, h - 1, n2, ct), F32),
                        pltpu.VMEM((2 * n2, (h + 1) * ct), BF16)],
        compiler_params=_params(("parallel", "parallel")),
        name="fnet_long",
    )(u.reshape(bsz, n1, n2, D_C), _twiddle1(n1, n2, ct), m, ab)
    return out.transpose(0, 2, 1, 3).reshape(bsz, seq, D_C)


def _fnet_short_kernel(u_ref, f_ref, ab_ref, o_ref, *, seq, scale):
    def body(trip, carry):
        sqs = (2 * trip, 2 * trip + 1)
        xs = [_dot(f_ref[...], u_ref[sq]) for sq in sqs]
        for sq, x in zip(sqs, xs):
            xr, xi = x[:seq], x[seq:]
            for g in range(N_GROUPS_C):
                sl = slice(g * G_C, (g + 1) * G_C)
                xin = jnp.concatenate([xr[:, sl], xi[:, sl]], axis=1).astype(BF16)
                o_ref[sq, :, sl] = (_dot(xin, ab_ref[g]) * scale).astype(o_ref.dtype)
        return carry

    lax.fori_loop(0, u_ref.shape[0] // 2, body, 0)


def _fnet_short(u, ab, scale, nseq):
    bsz, seq, _ = u.shape
    c, s = _dft_cos_sin(seq)
    f = _bf16_split(np.concatenate([c, -s], axis=0))[0]
    return pl.pallas_call(
        functools.partial(_fnet_short_kernel, seq=seq, scale=scale),
        grid=(bsz // nseq,),
        in_specs=[pl.BlockSpec((nseq, seq, D_C), lambda b: (b, 0, 0)),
                  pl.BlockSpec((2 * seq, seq), lambda b: (0, 0)),
                  pl.BlockSpec((N_GROUPS_C, 2 * G_C, G_C), lambda b: (0, 0, 0))],
        out_specs=pl.BlockSpec((nseq, seq, D_C), lambda b: (b, 0, 0)),
        out_shape=jax.ShapeDtypeStruct((bsz, seq, D_C), BF16),
        compiler_params=_params(("parallel",)),
        name="fnet_short",
    )(u, f, ab)


def _final_kernel(x1_ref, y_ref, g_ref, gt_ref, wo32_ref, fg_ref, o_ref, wo_ref):
    @pl.when(pl.program_id(0) == 0)
    def _():
        wo_ref[...] = wo32_ref[...].astype(BF16)

    z = (y_ref[...].astype(F32) * _silu(g_ref[...].astype(F32))).astype(BF16)
    x2 = x1_ref[...].astype(F32) + gt_ref[...] * _dot(z, wo_ref[...])
    o_ref[...] = _rms(x2) * fg_ref[...]


def _final(x1, y, g, gt, wo, fg, tiles_per_mod, tm):
    t = x1.shape[0]
    row = lambda i: (i, 0)
    fixed = lambda i: (0, 0)
    mod = lambda i: (i // tiles_per_mod, 0, 0)
    return pl.pallas_call(
        _final_kernel,
        grid=(t // tm,),
        in_specs=[pl.BlockSpec((tm, D_MODEL), row),
                  pl.BlockSpec((tm, D_C), row),
                  pl.BlockSpec((tm, D_C), row),
                  pl.BlockSpec((None, 1, D_MODEL), mod),
                  pl.BlockSpec((D_C, D_MODEL), fixed),
                  pl.BlockSpec((1, D_MODEL), fixed)],
        out_specs=pl.BlockSpec((tm, D_MODEL), row),
        out_shape=jax.ShapeDtypeStruct((t, D_MODEL), F32),
        scratch_shapes=[pltpu.VMEM((D_C, D_MODEL), BF16)],
        compiler_params=_params(("arbitrary",)),
        name="outproj_odd_final",
    )(x1, y, g, gt, wo, fg)


def _rope_swap(w):
    nf = ROPE_DIM // 4
    w4 = w.reshape(w.shape[:-1] + (2, 2, nf))
    return jnp.stack([-w4[..., 1, :], w4[..., 0, :]], axis=-2).reshape(w.shape)


def _pack_w_in_even(w):
    ua_ga_cq_ckv = w[:, :4 * D_A + Q_RANK + KV_RANK]
    o = 4 * D_A + Q_RANK + KV_RANK
    krope = w[:, o:o + ROPE_DIM]
    gb = w[:, o + ROPE_DIM:]
    zpad = jnp.zeros((w.shape[0], LANES - ROPE_DIM), w.dtype)
    return jnp.concatenate([ua_ga_cq_ckv, gb, krope, zpad, _rope_swap(krope), zpad],
                           axis=1).astype(BF16)


def _pack_w_uq(w):
    w3 = w.reshape(Q_RANK, N_HEADS, QK_DIM)
    zq = jnp.zeros((Q_RANK, N_HEADS, HEAD_PAD - QK_DIM), w.dtype)
    wa = jnp.concatenate([w3, zq], axis=-1)
    zn = jnp.zeros((Q_RANK, N_HEADS, NOPE_DIM), w.dtype)
    wb = jnp.concatenate([zn, _rope_swap(w3[..., NOPE_DIM:]), zq], axis=-1)
    shape = (Q_RANK, N_HEADS * HEAD_PAD)
    return wa.reshape(shape).astype(BF16), wb.reshape(shape).astype(BF16)


def _pack_w_ukv(w):
    w3 = w.reshape(KV_RANK, N_HEADS, NOPE_DIM + V_DIM)
    zk = jnp.zeros((KV_RANK, N_HEADS, HEAD_PAD - NOPE_DIM), w.dtype)
    wk_c = jnp.concatenate([w3[..., :NOPE_DIM], zk], axis=-1).reshape(KV_RANK, -1)
    eye = jnp.eye(LANES, ROPE_DIM, dtype=w.dtype)
    rope_rows = jnp.concatenate([jnp.zeros((LANES, NOPE_DIM), w.dtype), eye,
                                 jnp.zeros((LANES, HEAD_PAD - QK_DIM), w.dtype)], axis=-1)
    wk_r = jnp.tile(rope_rows, (1, N_HEADS))
    wk = jnp.concatenate([wk_c, wk_r], axis=0)
    zv = jnp.zeros((KV_RANK, N_HEADS, HEAD_PAD - V_DIM), w.dtype)
    wv = jnp.concatenate([w3[..., NOPE_DIM:], zv], axis=-1).reshape(KV_RANK, -1)
    ones = np.zeros((1, N_HEADS, HEAD_PAD), np.float32)
    ones[:, :, V_DIM] = 1.0
    return wk.astype(BF16), wv.astype(BF16), jnp.asarray(ones.reshape(1, -1))


def _rope_tables(seq, rotate):
    pos = np.arange(seq)
    half = ROPE_DIM // 2
    inv = ROPE_BASE ** (-np.arange(0, half, 2, dtype=np.float64) / half)
    r = (pos // GRID_W)[:, None] * inv
    c = (pos % GRID_W)[:, None] * inv
    ang = np.concatenate([r, r, c, c], axis=-1) * (1.0 if rotate else 0.0)
    cos_q = np.zeros((seq, LANES))
    sin_q = np.zeros((seq, LANES))
    cos_k = np.zeros((seq, LANES))
    sin_k = np.zeros((seq, LANES))
    cos_q[:, :NOPE_DIM] = 1.0
    cos_q[:, NOPE_DIM:QK_DIM] = np.cos(ang)
    sin_q[:, NOPE_DIM:QK_DIM] = np.sin(ang)
    cos_k[:, :ROPE_DIM] = np.cos(ang)
    sin_k[:, :ROPE_DIM] = np.sin(ang)
    return tuple(jnp.asarray(t.astype(np.float32)) for t in (cos_q, sin_q, cos_k, sin_k))


def kernel(x_prompt, x_sample, c, cache_ckv, cache_krope, c_ctx, norm_g, w_mod, b_mod,
           w_in_e, conv_w, conv_b, filt_w1, filt_b1, filt_w2, filt_b2, filt_w3, hyena_skip,
           q_norm_g, kv_norm_g, w_uq, w_ukv, w_out_e, w_in_o, fnet_w, w_out_o, final_g):
    nb_p, seq_p, _ = x_prompt.shape
    nb_s, seq_s, _ = x_sample.shape
    tp = nb_p * seq_p
    ts = nb_s * seq_s
    tm = ROW_TILE
    short_nseq = SHORT_SEQ_ROWS // seq_p

    cond = jnp.concatenate([c_ctx[None, :], c, jnp.zeros((8 - 1 - nb_s, D_MODEL), F32)], axis=0)
    mod = _ada_params(cond, w_mod, b_mod)
    shift = mod[:, :, None, :D_MODEL]
    scale = (1.0 + mod[:, :, None, D_MODEL:2 * D_MODEL]) * norm_g[:, None, None, :]
    gate = mod[:, :, None, 2 * D_MODEL:]

    xp = x_prompt.reshape(tp, D_MODEL)
    xs = x_sample.reshape(ts, D_MODEL)

    w_in = _pack_w_in_even(w_in_e[0])
    qg = q_norm_g[0].reshape(1, Q_RANK)
    kvg = kv_norm_g[0].reshape(1, KV_RANK)
    wq_a, wq_b = _pack_w_uq(w_uq[0])
    wk, wv, vone = _pack_w_ukv(w_ukv[0])
    cos_qp, sin_qp, cos_kp, sin_kp = _rope_tables(tm, rotate=False)
    cos_qs, sin_qs, cos_ks, sin_ks = _rope_tables(seq_s, rotate=True)
    pr_p = _inproj_even(xp, scale[0, :1], shift[0, :1], w_in, qg, kvg, cos_kp, sin_kp, wk, wv,
                        vone, tm, True)
    pr_s = _inproj_even(xs, scale[0, 1:1 + nb_s], shift[0, 1:1 + nb_s], w_in, qg, kvg,
                        cos_ks, sin_ks, wk, wv, vone, tm, False,
                        cache=(cache_ckv, cache_krope, 0))
    ua_p, ga_p, cq_p, gb_p, k_p, v_p, ckv_p, kra_p = pr_p
    ua_s, ga_s, cq_s, gb_s, k_s, v_s = pr_s

    filt = (filt_w1[0], filt_b1[0], filt_w2[0], filt_b2[0], filt_w3[0])
    ya = []
    for ua, ga, nb, seq in ((ua_p, ga_p, nb_p, seq_p), (ua_s, ga_s, nb_s, seq_s)):
        h, inv_norm = _hyena_filter(seq, *filt)
        kf = _spectrum(h, inv_norm, seq)
        ya.append(_hyena(ua.reshape(nb, seq, 3 * D_A), ga.reshape(nb, seq, D_A),
                         conv_w[0], conv_b[0], hyena_skip[0], kf))
    ya_p = ya[0].reshape(tp, D_A)
    ya_s = ya[1].reshape(ts, D_A)

    ob_p = _attention(cq_p.reshape(nb_p, seq_p, Q_RANK), cos_qp, sin_qp, wq_a, wq_b,
                      k_p.reshape(nb_p, seq_p, -1), v_p.reshape(nb_p, seq_p, -1),
                      gb_p.reshape(nb_p, seq_p, D_B), seq_p, nseq=short_nseq)
    ob_s = _attention(cq_s.reshape(nb_s, seq_s, Q_RANK), cos_qs, sin_qs, wq_a, wq_b, k_s, v_s,
                      gb_s.reshape(nb_s, seq_s, D_B), ATTN_Q_TILE)

    wo_e = w_out_e[0]
    wi_o = w_in_o[0]
    tw = WIDE_ROW_TILE
    x1_p, u_p, g_p = _mid(xp, ya_p, ob_p.reshape(tp, D_B), gate[0, :1], wo_e,
                          scale[1, :1], shift[1, :1], wi_o, tp // tw, tw)
    x1_s, u_s, g_s = _mid(xs, ya_s, ob_s.reshape(ts, D_B), gate[0, 1:1 + nb_s], wo_e,
                          scale[1, 1:1 + nb_s], shift[1, 1:1 + nb_s], wi_o, seq_s // tw, tw)

    ab = _fnet_weights(fnet_w[0], seq_s)
    y_p = _fnet_short(u_p.reshape(nb_p, seq_p, D_C), ab, math.sqrt(seq_s / seq_p), short_nseq)
    y_s = _fnet_long(u_s.reshape(nb_s, seq_s, D_C), ab, FNET_RADIX, seq_s // FNET_RADIX)
    wo_o = w_out_o[0]
    fg = final_g.reshape(1, D_MODEL)
    out_p = _final(x1_p, y_p.reshape(tp, D_C), g_p, gate[1, :1], wo_o, fg, tp // tw, tw)
    out_s = _final(x1_s, y_s.reshape(ts, D_C), g_s, gate[1, 1:1 + nb_s], wo_o, fg,
                   seq_s // tw, tw)

    state_ckv = ckv_p.reshape(nb_p, 1, seq_p, KV_RANK)
    state_krope = kra_p[:, :ROPE_DIM].reshape(nb_p, 1, seq_p, ROPE_DIM)
    return (out_p.reshape(nb_p, seq_p, D_MODEL), out_s.reshape(nb_s, seq_s, D_MODEL),
            state_ckv, state_krope)
```
